```python
import math
import jax
import jax.numpy as jnp
from jax import lax
import numpy as np

D_MODEL = 2048
BATCH = 4
SEQ = 4096
DEPTH = 2

F32 = jnp.float32
RMS_EPS = 1e-6
N_BRANCH = 3
BRANCH_WIDTH = D_MODEL // 2
DN_HEAD_DIM = 128
DN_HEADS = BRANCH_WIDTH // DN_HEAD_DIM
DN_WIDTH = DN_HEADS * DN_HEAD_DIM
DN_CONV = 4
DN_CHUNK = 64
FOX_HEAD_DIM = 128
FOX_HEADS = BRANCH_WIDTH // FOX_HEAD_DIM
FOX_WIDTH = FOX_HEADS * FOX_HEAD_DIM
Q_BLOCK = 128
SWA_HEAD_DIM = 64
SWA_Q_HEADS = BRANCH_WIDTH // SWA_HEAD_DIM
SWA_GROUP = 8
SWA_KV_HEADS = SWA_Q_HEADS // SWA_GROUP
SWA_Q_WIDTH = SWA_Q_HEADS * SWA_HEAD_DIM
SWA_KV_WIDTH = SWA_KV_HEADS * SWA_HEAD_DIM
SWA_WINDOW = 128
ROPE_THETA = 10000.0
MAX_POS_OFFSET = 2048
D_FF = 7 * D_MODEL // 2
N_EXPERTS = 8
TOP_K = 2
N_DENSE = (DEPTH + 1) // 2
N_MOE = DEPTH // 2
IN_SPLIT_WIDTHS = (DN_WIDTH, DN_WIDTH, DN_WIDTH, DN_WIDTH, DN_HEADS, DN_HEADS,
                   FOX_WIDTH, FOX_WIDTH, FOX_WIDTH, FOX_HEADS,
                   SWA_Q_WIDTH, SWA_KV_WIDTH, SWA_KV_WIDTH,
                   N_BRANCH * D_MODEL)
IN_WIDTH = sum(IN_SPLIT_WIDTHS)

kernel_name = "hybrid_gdn_fox_swa_moe_adaln"


def rms_norm(x, gain):
    xf = x.astype(F32)
    y = xf * lax.rsqrt(jnp.mean(xf * xf, axis=-1, keepdims=True) + RMS_EPS)
    return (y * gain.astype(F32)).astype(x.dtype)


def l2_normalize(x):
    return x * lax.rsqrt(jnp.sum(x * x, axis=-1, keepdims=True) + RMS_EPS)


def modulate(h, shift, scale):
    return h * (1 + scale[:, None, :]) + shift[:, None, :]


def causal_depthwise_conv(x, w):
    k_width, ch = w.shape
    return lax.conv_general_dilated(x, w[:, None, :].astype(x.dtype), window_strides=(1,),
                                    padding=[(k_width - 1, 0)],
                                    dimension_numbers=('NWC', 'WIO', 'NWC'),
                                    feature_group_count=ch)


def rope(x, cos, sin):
    x1, x2 = jnp.split(x.astype(F32), 2, axis=-1)
    return jnp.concatenate([x1 * cos - x2 * sin, x2 * cos + x1 * sin], axis=-1).astype(x.dtype)


def chunk_gated_delta_rule(q, k, v, g, beta):
    bsz, seq, heads, dk = q.shape
    dv = v.shape[-1]
    n_chunks = seq // DN_CHUNK

    def to_chunks(t):
        return t.reshape(bsz, n_chunks, DN_CHUNK, heads, -1).transpose(1, 0, 3, 2, 4)

    qc, kc, vc = to_chunks(q), to_chunks(k), to_chunks(v)
    gc = jnp.cumsum(g.reshape(bsz, n_chunks, DN_CHUNK, heads).transpose(1, 0, 3, 2), axis=-1)
    bc = beta.reshape(bsz, n_chunks, DN_CHUNK, heads).transpose(1, 0, 3, 2)[..., None]
    causal = jnp.tril(jnp.ones((DN_CHUNK, DN_CHUNK), dtype=bool))
    strict = jnp.tril(jnp.ones((DN_CHUNK, DN_CHUNK), dtype=bool), k=-1)
    decay = jnp.exp(jnp.where(causal, gc[..., :, None] - gc[..., None, :], -jnp.inf))
    kb = kc * bc
    lower = jnp.where(strict, jnp.einsum('nbhcd,nbhed->nbhce', kb, kc) * decay, 0.0)
    rhs = jnp.concatenate([vc * bc, kb * jnp.exp(gc)[..., None]], axis=-1)
    sol = lax.linalg.triangular_solve(lower, rhs, left_side=True, lower=True, unit_diagonal=True)
    u, w = sol[..., :dv], sol[..., dv:]
    qk = jnp.einsum('nbhcd,nbhed->nbhce', qc, kc) * decay
    q_dec = qc * jnp.exp(gc)[..., None]
    k_dec = kc * jnp.exp(gc[..., -1:] - gc)[..., None]
    g_last = jnp.exp(gc[..., -1])

    def step(state, inp):
        qk_i, qd_i, kd_i, u_i, w_i, gl_i = inp
        v_new = u_i - jnp.einsum('bhcd,bhdv->bhcv', w_i, state)
        o_i = jnp.einsum('bhcd,bhdv->bhcv', qd_i, state) + jnp.einsum('bhce,bhev->bhcv', qk_i, v_new)
        state = state * gl_i[..., None, None] + jnp.einsum('bhcd,bhcv->bhdv', kd_i, v_new)
        return state, o_i

    state0 = jnp.zeros((bsz, heads, dk, dv), F32)
    _, o = lax.scan(step, state0, (qk, q_dec, k_dec, u, w, g_last))
    return o.transpose(1, 0, 3, 2, 4).reshape(bsz, seq, heads, dv)


def gated_deltanet(q, k, v, z, beta_logit, a_logit, conv_w, a_log, dt_bias, norm_gain):
    bsz, seq, _ = q.shape
    qkv = jax.nn.silu(causal_depthwise_conv(jnp.concatenate([q, k, v], axis=-1), conv_w)).astype(F32)
    q, k, v = jnp.split(qkv, 3, axis=-1)
    q = l2_normalize(q.reshape(bsz, seq, DN_HEADS, DN_HEAD_DIM)) * (DN_HEAD_DIM ** -0.5)
    k = l2_normalize(k.reshape(bsz, seq, DN_HEADS, DN_HEAD_DIM))
    v = v.reshape(bsz, seq, DN_HEADS, DN_HEAD_DIM)
    beta = jax.nn.sigmoid(beta_logit.astype(F32))
    g = -jnp.exp(a_log.astype(F32)) * jax.nn.softplus(a_logit.astype(F32) + dt_bias.astype(F32))
    o = chunk_gated_delta_rule(q, k, v, g, beta)
    zf = z.astype(F32).reshape(bsz, seq, DN_HEADS, DN_HEAD_DIM)
    o = rms_norm(o, norm_gain) * jax.nn.silu(zf)
    return o.reshape(bsz, seq, DN_WIDTH).astype(z.dtype)


def forgetting_attention(q, k, v, f_logit, b_forget):
    bsz, seq, _ = q.shape
    q = q.reshape(bsz, seq, FOX_HEADS, FOX_HEAD_DIM)
    k = k.reshape(bsz, seq, FOX_HEADS, FOX_HEAD_DIM)
    v = v.reshape(bsz, seq, FOX_HEADS, FOX_HEAD_DIM)
    log_f = jax.nn.log_sigmoid(f_logit.astype(F32) + b_forget.astype(F32))
    cum = jnp.cumsum(log_f, axis=1).transpose(0, 2, 1)
    n_blocks = seq // Q_BLOCK
    q_blocks = q.reshape(bsz, n_blocks, Q_BLOCK, FOX_HEADS, FOX_HEAD_DIM).transpose(1, 0, 2, 3, 4)
    c_blocks = cum.reshape(bsz, FOX_HEADS, n_blocks, Q_BLOCK).transpose(2, 0, 1, 3)
    k_pos = jnp.arange(seq)
    scale = FOX_HEAD_DIM ** -0.5

    def block(args):
        idx, q_i, c_i = args
        s = jnp.einsum('bqhd,bkhd->bhqk', q_i, k).astype(F32) * scale
        s = s + c_i[..., :, None] - cum[:, :, None, :]
        q_pos = idx * Q_BLOCK + jnp.arange(Q_BLOCK)
        s = jnp.where(k_pos[None, :] <= q_pos[:, None], s, -jnp.inf)
        p = jax.nn.softmax(s, axis=-1).astype(v.dtype)
        return jnp.einsum('bhqk,bkhd->bqhd', p, v)

    o = lax.map(block, (jnp.arange(n_blocks), q_blocks, c_blocks))
    return o.transpose(1, 0, 2, 3, 4).reshape(bsz, seq, FOX_WIDTH)


def sliding_window_attention(q, k, v, sinks, cos, sin):
    bsz, seq, _ = q.shape
    win = SWA_WINDOW
    n_blocks = seq // win
    q = rope(q.reshape(bsz, seq, SWA_Q_HEADS, SWA_HEAD_DIM), cos, sin)
    k = rope(k.reshape(bsz, seq, SWA_KV_HEADS, SWA_HEAD_DIM), cos, sin)
    v = v.reshape(bsz, seq, SWA_KV_HEADS, SWA_HEAD_DIM)
    qb = q.reshape(bsz, n_blocks, win, SWA_KV_HEADS, SWA_GROUP, SWA_HEAD_DIM)

    def band(t):
        tp = jnp.pad(t, ((0, 0), (win, 0), (0, 0), (0, 0))).reshape(bsz, n_blocks + 1, win, SWA_KV_HEADS, SWA_HEAD_DIM)
        return jnp.concatenate([tp[:, :-1], tp[:, 1:]], axis=2)

    kb, vb = band(k), band(v)
    s = jnp.einsum('bnqhgd,bnkhd->bnhgqk', qb, kb).astype(F32) * (SWA_HEAD_DIM ** -0.5)
    q_off = jnp.arange(win)
    k_off = jnp.arange(2 * win) - win
    rel = q_off[:, None] - k_off[None, :]
    k_abs = jnp.arange(n_blocks)[:, None] * win + k_off[None, :]
    mask = ((rel >= 0) & (rel < win))[None] & (k_abs >= 0)[:, None, :]
    s = jnp.where(mask[None, :, None, None], s, -jnp.inf)
    sink = jnp.broadcast_to(sinks.astype(F32).reshape(SWA_KV_HEADS, SWA_GROUP)[None, None, :, :, None, None],
                            s.shape[:-1] + (1,))
    p = jax.nn.softmax(jnp.concatenate([s, sink], axis=-1), axis=-1)[..., :2 * win]
    o = jnp.einsum('bnhgqk,bnkhd->bnqhgd', p.astype(v.dtype), vb)
    return o.reshape(bsz, seq, SWA_Q_WIDTH)


def token_mixer(h, cos, sin, w_in, conv_w, a_log, dt_bias, dn_norm, b_forget, sinks, w_branch, w_out):
    bsz, seq, _ = h.shape
    offsets = tuple(int(o) for o in np.cumsum(IN_SPLIT_WIDTHS)[:-1])
    (dn_q, dn_k, dn_v, dn_z, dn_beta, dn_a, fox_q, fox_k, fox_v, fox_f,
     swa_q, swa_k, swa_v, gate_logits) = jnp.split(jnp.einsum('bsd,de->bse', h, w_in), offsets, axis=-1)
    o_a = gated_deltanet(dn_q, dn_k, dn_v, dn_z, dn_beta, dn_a, conv_w, a_log, dt_bias, dn_norm)
    o_b = forgetting_attention(fox_q, fox_k, fox_v, fox_f, b_forget)
    o_c = sliding_window_attention(swa_q, swa_k, swa_v, sinks, cos, sin)
    gates = jax.nn.sigmoid(gate_logits.astype(F32)).astype(h.dtype).reshape(bsz, seq, N_BRANCH, D_MODEL)
    merged = (gates[:, :, 0] * jnp.einsum('bsc,cd->bsd', o_a, w_branch[0])
              + gates[:, :, 1] * jnp.einsum('bsc,cd->bsd', o_b, w_branch[1])
              + gates[:, :, 2] * jnp.einsum('bsc,cd->bsd', o_c, w_branch[2]))
    return jnp.einsum('bsd,de->bse', merged, w_out)


def swiglu(h, w_gate, w_up, w_down):
    return jnp.einsum('bsf,fd->bsd', jax.nn.silu(jnp.einsum('bsd,df->bsf', h, w_gate)) * jnp.einsum('bsd,df->bsf', h, w_up), w_down)


def moe_swiglu(h, w_router, w_gate, w_up, w_down):
    logits = jnp.einsum('bsd,de->bse', h, w_router).astype(F32)
    top_val, top_idx = lax.top_k(logits, TOP_K)
    top_w = jax.nn.softmax(top_val, axis=-1)
    combine = jnp.sum(jax.nn.one_hot(top_idx, N_EXPERTS, dtype=F32) * top_w[..., None], axis=-2).astype(h.dtype)
    out = combine[..., 0:1] * swiglu(h, w_gate[0], w_up[0], w_down[0])
    for e in range(1, N_EXPERTS):
        out = out + combine[..., e:e + 1] * swiglu(h, w_gate[e], w_up[e], w_down[e])
    return out


def setup_inputs(seed: int = 0) -> dict:
    key = jax.random.key(seed)
    ks = jax.random.split(key, 24)

    def nrm(k, shape, scale):
        return jax.random.normal(k, shape, F32) * scale

    x = nrm(ks[0], (BATCH, SEQ, D_MODEL), 1.0)
    c = nrm(ks[1], (BATCH, D_MODEL), 1.0)
    positions = (jax.random.randint(ks[2], (BATCH, 1), 0, MAX_POS_OFFSET, dtype=jnp.int32)
                 + jnp.arange(SEQ, dtype=jnp.int32)[None, :]).astype(jnp.int32)
    w_ada = nrm(ks[3], (DEPTH, D_MODEL, 6 * D_MODEL), 0.5 * D_MODEL ** -0.5)
    b_ada = nrm(ks[4], (DEPTH, 6 * D_MODEL), 0.02)
    norm_mix = 1.0 + nrm(ks[5], (DEPTH, D_MODEL), 0.02)
    w_in = nrm(ks[6], (DEPTH, D_MODEL, IN_WIDTH), D_MODEL ** -0.5)
    conv_w = nrm(ks[7], (DEPTH, DN_CONV, 3 * DN_WIDTH), DN_CONV ** -0.5)
    dn_a_log = jnp.log(jax.random.uniform(ks[8], (DEPTH, DN_HEADS), F32, 1.0, 16.0))
    dt = jnp.exp(jax.random.uniform(ks[9], (DEPTH, DN_HEADS), F32, math.log(1e-3), math.log(1e-1)))
    dn_dt_bias = dt + jnp.log(-jnp.expm1(-dt))
    dn_norm = 1.0 + nrm(ks[10], (DEPTH, DN_HEAD_DIM), 0.02)
    fox_b_forget = jax.random.uniform(ks[11], (DEPTH, FOX_HEADS), F32, 1.0, 4.0)
    swa_sinks = nrm(ks[12], (DEPTH, SWA_Q_HEADS), 1.0)
    w_branch = nrm(ks[13], (DEPTH, N_BRANCH, BRANCH_WIDTH, D_MODEL), BRANCH_WIDTH ** -0.5)
    w_out = nrm(ks[14], (DEPTH, D_MODEL, D_MODEL), D_MODEL ** -0.5)
    norm_ffn = 1.0 + nrm(ks[15], (DEPTH, D_MODEL), 0.02)
    ffn_w_gate = nrm(ks[16], (N_DENSE, D_MODEL, D_FF), D_MODEL ** -0.5)
    ffn_w_up = nrm(ks[17], (N_DENSE, D_MODEL, D_FF), D_MODEL ** -0.5)
    ffn_w_down = nrm(ks[18], (N_DENSE, D_FF, D_MODEL), D_FF ** -0.5)
    moe_router = nrm(ks[19], (N_MOE, D_MODEL, N_EXPERTS), D_MODEL ** -0.5)
    moe_w_gate = nrm(ks[20], (N_MOE, N_EXPERTS, D_MODEL, D_FF), D_MODEL ** -0.5)
    moe_w_up = nrm(ks[21], (N_MOE, N_EXPERTS, D_MODEL, D_FF), D_MODEL ** -0.5)
    moe_w_down = nrm(ks[22], (N_MOE, N_EXPERTS, D_FF, D_MODEL), D_FF ** -0.5)
    final_norm = 1.0 + nrm(ks[23], (D_MODEL,), 0.02)
    return {"x": x, "c": c, "positions": positions, "w_ada": w_ada, "b_ada": b_ada,
            "norm_mix": norm_mix, "w_in": w_in, "conv_w": conv_w, "dn_a_log": dn_a_log,
            "dn_dt_bias": dn_dt_bias, "dn_norm": dn_norm, "fox_b_forget": fox_b_forget,
            "swa_sinks": swa_sinks, "w_branch": w_branch, "w_out": w_out, "norm_ffn": norm_ffn,
            "ffn_w_gate": ffn_w_gate, "ffn_w_up": ffn_w_up, "ffn_w_down": ffn_w_down,
            "moe_router": moe_router, "moe_w_gate": moe_w_gate, "moe_w_up": moe_w_up,
            "moe_w_down": moe_w_down, "final_norm": final_norm}


def reference(x, c, positions, w_ada, b_ada, norm_mix, w_in, conv_w, dn_a_log, dn_dt_bias, dn_norm,
              fox_b_forget, swa_sinks, w_branch, w_out, norm_ffn, ffn_w_gate, ffn_w_up, ffn_w_down,
              moe_router, moe_w_gate, moe_w_up, moe_w_down, final_norm):
    inv_freq = ROPE_THETA ** (-jnp.arange(0, SWA_HEAD_DIM, 2, dtype=F32) / SWA_HEAD_DIM)
    ang = positions.astype(F32)[..., None] * inv_freq
    cos, sin = jnp.cos(ang)[:, :, None, :], jnp.sin(ang)[:, :, None, :]
    c_act = jax.nn.silu(c)
    for layer in range(DEPTH):
        mod = jnp.einsum('bd,de->be', c_act, w_ada[layer]) + b_ada[layer]
        sh1, sc1, g1, sh2, sc2, g2 = jnp.split(mod, 6, axis=-1)
        h = modulate(rms_norm(x, norm_mix[layer]), sh1, sc1)
        x = x + g1[:, None, :] * token_mixer(h, cos, sin, w_in[layer], conv_w[layer], dn_a_log[layer],
                                             dn_dt_bias[layer], dn_norm[layer], fox_b_forget[layer],
                                             swa_sinks[layer], w_branch[layer], w_out[layer])
        h = modulate(rms_norm(x, norm_ffn[layer]), sh2, sc2)
        if layer % 2 == 0:
            i = layer // 2
            f = swiglu(h, ffn_w_gate[i], ffn_w_up[i], ffn_w_down[i])
        else:
            i = layer // 2
            f = moe_swiglu(h, moe_router[i], moe_w_gate[i], moe_w_up[i], moe_w_down[i])
        x = x + g2[:, None, :] * f
    return rms_norm(x, final_norm)
```

```python
import functools

import jax
import jax.numpy as jnp
from jax import lax
from jax.experimental import pallas as pl
from jax.experimental.pallas import tpu as pltpu

F32 = jnp.float32
BF16 = jnp.bfloat16
EPS = 1e-6

D = 2048
HD = 128
NH = 8
BW = 1024
CHUNK = 64
SWA_D = 64
SWA_QH = 16
SWA_KVH = 2
SWA_G = 8
WIN = 128
D_FF = 7168
NE = 8
LANES = 128

P_GATE = 0
P_DN = 6144
P_DNZ = 9216
P_FQ, P_FK, P_FV = 10240, 11264, 12288
P_SQ, P_SK, P_SV = 13312, 14336, 14464
NP = 14592
L_BETA, L_G, L_F = 0, 8, 16

VMEM_LIMIT = 56 * 1024 * 1024


def _cparams(sem):
    return pltpu.CompilerParams(dimension_semantics=sem, vmem_limit_bytes=VMEM_LIMIT)


def _sigmoid(x):
    return 1.0 / (1.0 + jnp.exp(-x))


def _split3(x):
    hi = x.astype(BF16)
    r = x - hi.astype(F32)
    mid = r.astype(BF16)
    lo = (r - mid.astype(F32)).astype(BF16)
    return hi, mid, lo


def _dot(a, b):
    return jnp.dot(a, b, preferred_element_type=F32)


def _dot_hi(a, b):
    a0, a1, a2 = _split3(a)
    b0, b1, b2 = _split3(b)
    return (_dot(a0, b0) + (_dot(a0, b1) + _dot(a1, b0))
            + (_dot(a0, b2) + _dot(a1, b1) + _dot(a2, b0)))


def _ada_kernel(c_ref, w_ref, b_ref, o_ref):
    c = c_ref[...]
    ca = c * _sigmoid(c)
    o_ref[...] = _dot_hi(ca, w_ref[...]) + b_ref[...]


def _ada(c_pad, w_ada, b_ada):
    depth, d, n = w_ada.shape
    tn = 1536
    return pl.pallas_call(
        _ada_kernel,
        grid=(depth, n // tn),
        in_specs=[pl.BlockSpec((8, d), lambda l, j: (0, 0)),
                  pl.BlockSpec((None, d, tn), lambda l, j: (l, 0, j)),
                  pl.BlockSpec((None, 1, tn), lambda l, j: (l, 0, j))],
        out_specs=pl.BlockSpec((None, 8, tn), lambda l, j: (l, 0, j)),
        out_shape=jax.ShapeDtypeStruct((depth, 8, n), F32),
        compiler_params=_cparams(("arbitrary", "arbitrary")),
        name="ada",
    )(c_pad, w_ada, b_ada.reshape(depth, 1, n))


def _normmod(x_ref, gain_ref, sc_ref, sh_ref):
    x = x_ref[...]
    ms = jnp.mean(x * x, axis=-1, keepdims=True)
    y = x * lax.rsqrt(ms + EPS) * gain_ref[...]
    return y * (1.0 + sc_ref[...]) + sh_ref[...]


def _normmod_kernel(x_ref, gain_ref, sc_ref, sh_ref, h_ref):
    h_ref[...] = _normmod(x_ref, gain_ref, sc_ref, sh_ref).astype(BF16)


def _normmod_route_kernel(x_ref, gain_ref, sc_ref, sh_ref, wr_ref, h_ref, r_ref):
    h = _normmod(x_ref, gain_ref, sc_ref, sh_ref)
    h_ref[...] = h
    logits = _dot_hi(h, wr_ref[...])
    lane = lax.broadcasted_iota(jnp.int32, logits.shape, 1)
    neg = jnp.float32(-jnp.inf)
    l1 = jnp.where(lane < NE, logits, neg)
    m1 = jnp.max(l1, axis=-1, keepdims=True)
    i1 = jnp.min(jnp.where(l1 == m1, lane, LANES), axis=-1, keepdims=True)
    l2 = jnp.where(lane == i1, neg, l1)
    m2 = jnp.max(l2, axis=-1, keepdims=True)
    i2 = jnp.min(jnp.where(l2 == m2, lane, LANES), axis=-1, keepdims=True)
    e = jnp.exp(m2 - m1)
    w1 = 1.0 / (1.0 + e)
    w2 = e / (1.0 + e)
    r_ref[...] = jnp.where(lane == 0, i1.astype(F32),
                           jnp.where(lane == 1, i2.astype(F32),
                                     jnp.where(lane == 2, w1, jnp.where(lane == 3, w2, 0.0))))


def _norm_modulate(x, gain, sc, sh, seq, w_router=None):
    m, d = x.shape
    tm = 512
    tpb = seq // tm
    in_specs = [pl.BlockSpec((tm, d), lambda i: (i, 0)),
                pl.BlockSpec((1, d), lambda i: (0, 0)),
                pl.BlockSpec((None, 1, d), lambda i: (i // tpb, 0, 0)),
                pl.BlockSpec((None, 1, d), lambda i: (i // tpb, 0, 0))]
    args = [x, gain.reshape(1, d), sc, sh]
    if w_router is None:
        return pl.pallas_call(
            _normmod_kernel, grid=(m // tm,), in_specs=in_specs,
            out_specs=pl.BlockSpec((tm, d), lambda i: (i, 0)),
            out_shape=jax.ShapeDtypeStruct((m, d), BF16),
            compiler_params=_cparams(("arbitrary",)), name="normmod",
        )(*args)
    wr = jnp.pad(w_router, ((0, 0), (0, LANES - NE)))
    return pl.pallas_call(
        _normmod_route_kernel, grid=(m // tm,),
        in_specs=in_specs + [pl.BlockSpec((d, LANES), lambda i: (0, 0))],
        out_specs=[pl.BlockSpec((tm, d), lambda i: (i, 0)),
                   pl.BlockSpec((tm, LANES), lambda i: (i, 0))],
        out_shape=[jax.ShapeDtypeStruct((m, d), F32),
                   jax.ShapeDtypeStruct((m, LANES), F32)],
        compiler_params=_cparams(("arbitrary",)), name="normmod_route",
    )(*args, wr)


def _mm_kernel(a_ref, w_ref, o_ref):
    o_ref[...] = _dot(a_ref[...], w_ref[...]).astype(o_ref.dtype)


def _matmul(a, w, out_dtype, tm, tn, name):
    m, k = a.shape
    n = w.shape[1]
    return pl.pallas_call(
        _mm_kernel, grid=(n // tn, m // tm),
        in_specs=[pl.BlockSpec((tm, k), lambda j, i: (i, 0)),
                  pl.BlockSpec((k, tn), lambda j, i: (0, j))],
        out_specs=pl.BlockSpec((tm, tn), lambda j, i: (i, j)),
        out_shape=jax.ShapeDtypeStruct((m, n), out_dtype),
        compiler_params=_cparams(("arbitrary", "arbitrary")), name=name,
    )(a, w)


def _mm_resid_kernel(a_ref, w_ref, x_ref, g_ref, o_ref):
    o_ref[...] = x_ref[...] + g_ref[...] * _dot(a_ref[...], w_ref[...])


def _matmul_resid(a, w, x, gate, seq, tm, tn, name):
    m, k = a.shape
    n = w.shape[1]
    tpb = seq // tm
    return pl.pallas_call(
        _mm_resid_kernel, grid=(n // tn, m // tm),
        in_specs=[pl.BlockSpec((tm, k), lambda j, i: (i, 0)),
                  pl.BlockSpec((k, tn), lambda j, i: (0, j)),
                  pl.BlockSpec((tm, tn), lambda j, i: (i, j)),
                  pl.BlockSpec((None, 1, tn), lambda j, i: (i // tpb, 0, j))],
        out_specs=pl.BlockSpec((tm, tn), lambda j, i: (i, j)),
        out_shape=jax.ShapeDtypeStruct((m, n), F32),
        compiler_params=_cparams(("arbitrary", "arbitrary")), name=name,
    )(a, w, x, gate)


def _prep_kernel(x_ref, par_ref, col_ref, rowf_ref, rowg_ref, carry_ref, *, ts):
    @pl.when(pl.program_id(1) == 0)
    def _():
        carry_ref[...] = jnp.zeros_like(carry_ref)

    x = x_ref[...]
    lane = lax.broadcasted_iota(jnp.int32, x.shape, 1)
    is_g = (lane >= L_G) & (lane < L_G + NH)
    is_f = (lane >= L_F) & (lane < L_F + NH)
    a_rate = jnp.exp(par_ref[0:1, :])
    z = x + par_ref[1:2, :]
    t = jnp.log1p(jnp.exp(-jnp.abs(z)))
    sig = _sigmoid(x)
    g = -a_rate * (jnp.maximum(z, 0.0) + t)
    logf = jnp.minimum(z, 0.0) - t
    vals = jnp.where(is_g, g, jnp.where(is_f, logf, 0.0))
    r = lax.broadcasted_iota(jnp.int32, (ts, ts), 0)
    c = lax.broadcasted_iota(jnp.int32, (ts, ts), 1)
    tri_full = jnp.where(c <= r, 1.0, 0.0).astype(BF16)
    tri_blk = jnp.where((c <= r) & ((r // CHUNK) == (c // CHUNK)), 1.0, 0.0).astype(BF16)
    v0, v1, v2 = _split3(vals)
    cs_full = _dot(tri_full, v0) + _dot(tri_full, v1) + _dot(tri_full, v2)
    cs_blk = _dot(tri_blk, v0) + _dot(tri_blk, v1) + _dot(tri_blk, v2)
    cum = cs_full + carry_ref[...]
    carry_ref[...] = cum[ts - 1:ts, :]
    out = jnp.where(lane < NH, sig, jnp.where(is_g, cs_blk, jnp.where(is_f, cum, 0.0)))
    col_ref[...] = out
    out_t = out.T
    rowf_ref[...] = out_t[L_F:L_F + NH, :]
    for ci in range(ts // CHUNK):
        rowg_ref[ci] = out_t[0:32, ci * CHUNK:(ci + 1) * CHUNK]


def _prep(small, par, bsz, seq):
    m = small.shape[0]
    ts = 256
    nt = seq // ts
    return pl.pallas_call(
        functools.partial(_prep_kernel, ts=ts),
        grid=(bsz, nt),
        in_specs=[pl.BlockSpec((ts, LANES), lambda b, j: (b * nt + j, 0)),
                  pl.BlockSpec((8, LANES), lambda b, j: (0, 0))],
        out_specs=[pl.BlockSpec((ts, LANES), lambda b, j: (b * nt + j, 0)),
                   pl.BlockSpec((NH, ts), lambda b, j: (0, b * nt + j)),
                   pl.BlockSpec((ts // CHUNK, 32, CHUNK), lambda b, j: (b * nt + j, 0, 0))],
        out_shape=[jax.ShapeDtypeStruct((m, LANES), F32),
                   jax.ShapeDtypeStruct((NH, m), F32),
                   jax.ShapeDtypeStruct((m // CHUNK, 32, CHUNK), F32)],
        scratch_shapes=[pltpu.VMEM((1, LANES), F32)],
        compiler_params=_cparams(("arbitrary", "arbitrary")), name="prep",
    )(small, par)


def _bmm(a, b):
    return lax.dot_general(a.astype(BF16), b.astype(BF16), (((2,), (1,)), ((0,), (0,))),
                           preferred_element_type=F32)


def _bmm_nt(a, b):
    return lax.dot_general(a.astype(BF16), b.astype(BF16), (((2,), (2,)), ((0,), (0,))),
                           preferred_element_type=F32)


def _delta_kernel(qkv_ref, halo_ref, z_ref, cw_ref, col_ref, rowg_ref, gain_ref, o_ref, s_ref, *, nc):
    j = pl.program_id(1)

    @pl.when(j == 0)
    def _():
        s_ref[...] = jnp.zeros_like(s_ref)

    ts = nc * CHUNK
    n = NH * nc
    x = qkv_ref[...].astype(F32)
    halo = jnp.where(j == 0, 0.0, halo_ref[...].astype(F32))
    xx = jnp.concatenate([halo, x], axis=0)
    cw = cw_ref[...]
    acc = x * cw[3:4, :]
    for dlt in (1, 2, 3):
        acc = acc + pltpu.roll(xx, dlt, axis=0)[8:, :] * cw[3 - dlt:4 - dlt, :]
    act = acc * _sigmoid(acc)

    cv = col_ref[...]
    rg = rowg_ref[...]
    qs, ks, vs, betas, gcs, grs = [], [], [], [], [], []
    for h in range(NH):
        qh = act[:, h * HD:(h + 1) * HD]
        kh = act[:, BW + h * HD:BW + (h + 1) * HD]
        vh = act[:, 2 * BW + h * HD:2 * BW + (h + 1) * HD]
        qh = qh * (lax.rsqrt(jnp.sum(qh * qh, axis=-1, keepdims=True) + EPS) * (HD ** -0.5))
        kh = kh * lax.rsqrt(jnp.sum(kh * kh, axis=-1, keepdims=True) + EPS)
        qs.append(qh.reshape(nc, CHUNK, HD))
        ks.append(kh.reshape(nc, CHUNK, HD))
        vs.append(vh.reshape(nc, CHUNK, HD))
        betas.append(cv[:, L_BETA + h:L_BETA + h + 1].reshape(nc, CHUNK, 1))
        gcs.append(cv[:, L_G + h:L_G + h + 1].reshape(nc, CHUNK, 1))
        grs.append(rg[:, L_G + h:L_G + h + 1, :])
    q = jnp.concatenate(qs, axis=0)
    k = jnp.concatenate(ks, axis=0)
    v = jnp.concatenate(vs, axis=0)
    beta = jnp.concatenate(betas, axis=0)
    gc = jnp.concatenate(gcs, axis=0)
    gr = jnp.concatenate(grs, axis=0)

    ri = lax.broadcasted_iota(jnp.int32, (CHUNK, CHUNK), 0)
    ci = lax.broadcasted_iota(jnp.int32, (CHUNK, CHUNK), 1)
    causal = (ci <= ri)[None]
    strict = (ci < ri)[None]
    decay = jnp.exp(jnp.where(causal, gc - gr, -jnp.inf))
    kb = k * beta
    lower = jnp.where(strict, _bmm_nt(kb, k) * decay, 0.0)
    qk = _bmm_nt(q, k) * decay

    eye = jnp.where(ci == ri, 1.0, 0.0)[None]
    t_inv = eye - lower
    pw = _bmm(lower, lower)
    for it in range(5):
        t_inv = t_inv + _bmm(t_inv, pw)
        if it < 4:
            pw = _bmm(pw, pw)

    eg = jnp.exp(gc)
    rhs = jnp.concatenate([v * beta, kb * eg], axis=-1)
    sol = _bmm(t_inv, rhs)
    u = sol[:, :, :HD]
    w = sol[:, :, HD:]
    q_dec = q * eg
    g_end = gc[:, CHUNK - 1:CHUNK, :]
    k_dec = k * jnp.exp(g_end - gc)
    g_last = jnp.exp(g_end)

    def pick(t, c):
        return t.reshape((NH, nc) + t.shape[1:])[:, c]

    state = s_ref[...]
    outs = []
    for c in range(nc):
        v_new = pick(u, c) - _bmm(pick(w, c), state)
        o_c = _bmm(pick(q_dec, c), state) + _bmm(pick(qk, c), v_new)
        kd_t = jnp.swapaxes(pick(k_dec, c), 1, 2)
        state = state * pick(g_last, c) + _bmm(kd_t, v_new)
        outs.append(o_c)
    s_ref[...] = state

    gain = gain_ref[...]
    zf = z_ref[...].astype(F32)
    cols = []
    for h in range(NH):
        oh = jnp.concatenate([outs[c][h] for c in range(nc)], axis=0)
        oh = oh * lax.rsqrt(jnp.mean(oh * oh, axis=-1, keepdims=True) + EPS) * gain
        zh = zf[:, h * HD:(h + 1) * HD]
        cols.append(oh * (zh * _sigmoid(zh)))
    o_ref[...] = jnp.concatenate(cols, axis=1).astype(BF16)


def _delta(proj, conv_w8, col, rowg, gain, bsz, seq):
    m = proj.shape[0]
    nc = 2
    ts = nc * CHUNK
    nt = seq // ts
    qkv_blk = P_DN // (3 * BW)
    return pl.pallas_call(
        functools.partial(_delta_kernel, nc=nc),
        grid=(bsz, nt),
        in_specs=[pl.BlockSpec((ts, 3 * BW), lambda b, j: (b * nt + j, qkv_blk)),
                  pl.BlockSpec((8, 3 * BW), lambda b, j: (jnp.maximum((b * nt + j) * (ts // 8) - 1, 0), qkv_blk)),
                  pl.BlockSpec((ts, BW), lambda b, j: (b * nt + j, P_DNZ // BW)),
                  pl.BlockSpec((8, 3 * BW), lambda b, j: (0, 0)),
                  pl.BlockSpec((ts, LANES), lambda b, j: (b * nt + j, 0)),
                  pl.BlockSpec((nc, 32, CHUNK), lambda b, j: (b * nt + j, 0, 0)),
                  pl.BlockSpec((1, HD), lambda b, j: (0, 0))],
        out_specs=pl.BlockSpec((ts, BW), lambda b, j: (b * nt + j, 0)),
        out_shape=jax.ShapeDtypeStruct((m, BW), BF16),
        scratch_shapes=[pltpu.VMEM((NH, HD, HD), F32)],
        compiler_params=_cparams(("arbitrary", "arbitrary")), name="delta",
    )(proj, proj, proj, conv_w8, col, rowg, gain.reshape(1, HD))


def _fox_kernel(q_ref, k_ref, v_ref, ccol_ref, crow_ref, o_ref, m_ref, l_ref, acc_ref, *, tq, tk, nk):
    h = pl.program_id(1)
    qi = pl.program_id(2)
    ki = pl.program_id(3)

    @pl.when(ki == 0)
    def _():
        m_ref[...] = jnp.full_like(m_ref, -jnp.inf)
        l_ref[...] = jnp.zeros_like(l_ref)
        acc_ref[...] = jnp.zeros_like(acc_ref)

    def step(masked):
        s = lax.dot_general(q_ref[...], k_ref[...], (((1,), (1,)), ((), ())),
                            preferred_element_type=F32) * (HD ** -0.5)
        lane = lax.broadcasted_iota(jnp.int32, (tq, LANES), 1)
        cq = jnp.sum(jnp.where(lane == L_F + h, ccol_ref[...], 0.0), axis=1, keepdims=True)
        ck = crow_ref[pl.ds(h, 1), :]
        s = s + cq - ck
        if masked:
            qpos = qi * tq + lax.broadcasted_iota(jnp.int32, (tq, tk), 0)
            kpos = ki * tk + lax.broadcasted_iota(jnp.int32, (tq, tk), 1)
            s = jnp.where(kpos <= qpos, s, -jnp.inf)
        m_prev = m_ref[...]
        m_new = jnp.maximum(m_prev, jnp.max(s, axis=1, keepdims=True))
        alpha = jnp.exp(m_prev - m_new)
        p = jnp.exp(s - m_new)
        l_ref[...] = alpha * l_ref[...] + jnp.sum(p, axis=1, keepdims=True)
        acc_ref[...] = alpha * acc_ref[...] + _dot(p.astype(BF16), v_ref[...])
        m_ref[...] = m_new

    first_row, last_row = qi * tq, qi * tq + tq - 1
    first_col, last_col = ki * tk, ki * tk + tk - 1

    @pl.when(last_col <= first_row)
    def _():
        step(False)

    @pl.when((last_col > first_row) & (first_col <= last_row))
    def _():
        step(True)

    @pl.when(ki == nk - 1)
    def _():
        o_ref[...] = (acc_ref[...] / l_ref[...]).astype(BF16)


def _fox(proj, col, rowf, bsz, seq):
    m = proj.shape[0]
    tq = tk = 512
    nq, nk = seq // tq, seq // tk

    def kv_blk(qi, ki):
        return jnp.minimum(ki, (qi * tq + tq - 1) // tk)

    return pl.pallas_call(
        functools.partial(_fox_kernel, tq=tq, tk=tk, nk=nk),
        grid=(bsz, NH, nq, nk),
        in_specs=[pl.BlockSpec((tq, HD), lambda b, h, qi, ki: (b * nq + qi, P_FQ // HD + h)),
                  pl.BlockSpec((tk, HD), lambda b, h, qi, ki: (b * nk + kv_blk(qi, ki), P_FK // HD + h)),
                  pl.BlockSpec((tk, HD), lambda b, h, qi, ki: (b * nk + kv_blk(qi, ki), P_FV // HD + h)),
                  pl.BlockSpec((tq, LANES), lambda b, h, qi, ki: (b * nq + qi, 0)),
                  pl.BlockSpec((NH, tk), lambda b, h, qi, ki: (0, b * nk + kv_blk(qi, ki)))],
        out_specs=pl.BlockSpec((tq, HD), lambda b, h, qi, ki: (b * nq + qi, h)),
        out_shape=jax.ShapeDtypeStruct((m, BW), BF16),
        scratch_shapes=[pltpu.VMEM((tq, 1), F32), pltpu.VMEM((tq, 1), F32), pltpu.VMEM((tq, HD), F32)],
        compiler_params=_cparams(("arbitrary", "arbitrary", "arbitrary", "arbitrary")), name="fox",
    )(proj, proj, proj, col, rowf)


def _swap_halves(x):
    w = x.shape[-1]
    lane = lax.broadcasted_iota(jnp.int32, x.shape, x.ndim - 1)
    return jnp.where((lane % SWA_D) < SWA_D // 2, pltpu.roll(x, w - SWA_D // 2, axis=x.ndim - 1),
                     pltpu.roll(x, SWA_D // 2, axis=x.ndim - 1))


def _swa_kernel(sink_ref, q_ref, kc_ref, kp_ref, vc_ref, vp_ref, cc_ref, sc_ref, cp_ref, sp_ref, o_ref, *, nblk):
    i = pl.program_id(0)
    first = (i % nblk) == 0
    cos_c, sin_c = cc_ref[...], sc_ref[...]
    q = q_ref[...].astype(F32)
    q = q * jnp.tile(cos_c, (1, SWA_QH // 2)) + _swap_halves(q) * jnp.tile(sin_c, (1, SWA_QH // 2))
    kc = kc_ref[...].astype(F32)
    kc = kc * cos_c + _swap_halves(kc) * sin_c
    kp = kp_ref[...].astype(F32)
    kp = kp * cp_ref[...] + _swap_halves(kp) * sp_ref[...]
    kk = jnp.concatenate([kp, kc], axis=0).astype(BF16)
    vv = jnp.concatenate([vp_ref[...], vc_ref[...]], axis=0)
    r = lax.broadcasted_iota(jnp.int32, (WIN, 2 * WIN), 0)
    c = lax.broadcasted_iota(jnp.int32, (WIN, 2 * WIN), 1)
    mask = (c > r) & (c <= r + WIN) & ((c >= WIN) | jnp.logical_not(first))
    qb = q.astype(BF16)
    outs = []
    for hq in range(SWA_QH):
        g = hq // SWA_G
        qh = qb[:, hq * SWA_D:(hq + 1) * SWA_D]
        kh = kk[:, g * SWA_D:(g + 1) * SWA_D]
        vh = vv[:, g * SWA_D:(g + 1) * SWA_D]
        s = lax.dot_general(qh, kh, (((1,), (1,)), ((), ())), preferred_element_type=F32) * (SWA_D ** -0.5)
        s = jnp.where(mask, s, -jnp.inf)
        sink = sink_ref[hq]
        mx = jnp.maximum(jnp.max(s, axis=1, keepdims=True), sink)
        p = jnp.exp(s - mx)
        den = jnp.sum(p, axis=1, keepdims=True) + jnp.exp(sink - mx)
        outs.append(_dot(p.astype(BF16), vh) / den)
    o_ref[...] = jnp.concatenate(outs, axis=1).astype(BF16)


def _swa(proj, sinks, cos4, sin4, bsz, seq):
    m = proj.shape[0]
    nblk = seq // WIN
    prev = lambda i: jnp.maximum(i - 1, 0)
    kcol, vcol = P_SK // LANES, P_SV // LANES
    return pl.pallas_call(
        functools.partial(_swa_kernel, nblk=nblk),
        grid=(m // WIN,),
        in_specs=[pl.BlockSpec(memory_space=pltpu.SMEM),
                  pl.BlockSpec((WIN, BW), lambda i: (i, P_SQ // BW)),
                  pl.BlockSpec((WIN, LANES), lambda i: (i, kcol)),
                  pl.BlockSpec((WIN, LANES), lambda i: (prev(i), kcol)),
                  pl.BlockSpec((WIN, LANES), lambda i: (i, vcol)),
                  pl.BlockSpec((WIN, LANES), lambda i: (prev(i), vcol)),
                  pl.BlockSpec((WIN, LANES), lambda i: (i, 0)),
                  pl.BlockSpec((WIN, LANES), lambda i: (i, 0)),
                  pl.BlockSpec((WIN, LANES), lambda i: (prev(i), 0)),
                  pl.BlockSpec((WIN, LANES), lambda i: (prev(i), 0))],
        out_specs=pl.BlockSpec((WIN, BW), lambda i: (i, 0)),
        out_shape=jax.ShapeDtypeStruct((m, BW), BF16),
        compiler_params=_cparams(("arbitrary",)), name="swa",
    )(sinks, proj, proj, proj, proj, proj, cos4, sin4, cos4, sin4)


def _merge_kernel(oa_ref, ob_ref, oc_ref, ga_ref, gb_ref, gc_ref, w_ref, o_ref):
    acc = _sigmoid(ga_ref[...].astype(F32)) * _dot(oa_ref[...], w_ref[0])
    acc = acc + _sigmoid(gb_ref[...].astype(F32)) * _dot(ob_ref[...], w_ref[1])
    acc = acc + _sigmoid(gc_ref[...].astype(F32)) * _dot(oc_ref[...], w_ref[2])
    o_ref[...] = acc.astype(BF16)


def _merge(o_a, o_b, o_c, proj, w_branch):
    m = o_a.shape[0]
    tm, tn = 512, 512
    gblk = lambda b: (lambda j, i: (i, (P_GATE + b * D) // tn + j))
    oblk = pl.BlockSpec((tm, BW), lambda j, i: (i, 0))
    return pl.pallas_call(
        _merge_kernel, grid=(D // tn, m // tm),
        in_specs=[oblk, oblk, oblk,
                  pl.BlockSpec((tm, tn), gblk(0)), pl.BlockSpec((tm, tn), gblk(1)), pl.BlockSpec((tm, tn), gblk(2)),
                  pl.BlockSpec((3, BW, tn), lambda j, i: (0, 0, j))],
        out_specs=pl.BlockSpec((tm, tn), lambda j, i: (i, j)),
        out_shape=jax.ShapeDtypeStruct((m, D), BF16),
        compiler_params=_cparams(("arbitrary", "arbitrary")), name="merge",
    )(o_a, o_b, o_c, proj, proj, proj, w_branch)


def _swiglu_acc(h, wg_ref, wu_ref, wd_ref, acc_ref):
    a = _dot(h, wg_ref[...])
    b = _dot(h, wu_ref[...])
    t = (a * _sigmoid(a) * b).astype(BF16)
    acc_ref[...] += _dot(t, wd_ref[...])


def _ffn_kernel(h_ref, wg_ref, wu_ref, wd_ref, x_ref, g_ref, o_ref, acc_ref, *, nf):
    f = pl.program_id(1)

    @pl.when(f == 0)
    def _():
        acc_ref[...] = jnp.zeros_like(acc_ref)

    _swiglu_acc(h_ref[...], wg_ref, wu_ref, wd_ref, acc_ref)

    @pl.when(f == nf - 1)
    def _():
        o_ref[...] = x_ref[...] + g_ref[...] * acc_ref[...]


def _ffn_dense(h, wg, wu, wd, x, gate, seq):
    m, d = h.shape
    ff = wg.shape[1]
    tm, tf = 512, 512
    nf = ff // tf
    tpb = seq // tm
    return pl.pallas_call(
        functools.partial(_ffn_kernel, nf=nf), grid=(m // tm, nf),
        in_specs=[pl.BlockSpec((tm, d), lambda i, f: (i, 0)),
                  pl.BlockSpec((d, tf), lambda i, f: (0, f)),
                  pl.BlockSpec((d, tf), lambda i, f: (0, f)),
                  pl.BlockSpec((tf, d), lambda i, f: (f, 0)),
                  pl.BlockSpec((tm, d), lambda i, f: (i, 0)),
                  pl.BlockSpec((None, 1, d), lambda i, f: (i // tpb, 0, 0))],
        out_specs=pl.BlockSpec((tm, d), lambda i, f: (i, 0)),
        out_shape=jax.ShapeDtypeStruct((m, d), F32),
        scratch_shapes=[pltpu.VMEM((tm, d), F32)],
        compiler_params=_cparams(("arbitrary", "arbitrary")), name="ffn_dense",
    )(h, wg, wu, wd, x, gate)


def _ffn_group_kernel(te_ref, tv_ref, h_ref, wg_ref, wu_ref, wd_ref, rw_ref, o_ref, acc_ref, *, nf):
    i = pl.program_id(0)
    f = pl.program_id(1)
    live = tv_ref[i] > 0

    @pl.when(f == 0)
    def _():
        acc_ref[...] = jnp.zeros_like(acc_ref)

    @pl.when(live)
    def _():
        _swiglu_acc(h_ref[...].astype(BF16), wg_ref, wu_ref, wd_ref, acc_ref)

    @pl.when(f == nf - 1)
    def _():
        o_ref[...] = acc_ref[...] * rw_ref[:, 0:1]


def _ffn_grouped(hs, wg, wu, wd, row_w, tile_expert, tile_valid, tm):
    r, d = hs.shape
    ff = wg.shape[2]
    tf = 512
    nf = ff // tf

    def fsel(i, f, tv):
        return jnp.where(tv[i] > 0, f, nf - 1)

    grid_spec = pltpu.PrefetchScalarGridSpec(
        num_scalar_prefetch=2, grid=(r // tm, nf),
        in_specs=[pl.BlockSpec((tm, d), lambda i, f, te, tv: (i, 0)),
                  pl.BlockSpec((None, d, tf), lambda i, f, te, tv: (te[i], 0, fsel(i, f, tv))),
                  pl.BlockSpec((None, d, tf), lambda i, f, te, tv: (te[i], 0, fsel(i, f, tv))),
                  pl.BlockSpec((None, tf, d), lambda i, f, te, tv: (te[i], fsel(i, f, tv), 0)),
                  pl.BlockSpec((tm, LANES), lambda i, f, te, tv: (i, 0))],
        out_specs=pl.BlockSpec((tm, d), lambda i, f, te, tv: (i, 0)),
        scratch_shapes=[pltpu.VMEM((tm, d), F32)])
    return pl.pallas_call(
        functools.partial(_ffn_group_kernel, nf=nf), grid_spec=grid_spec,
        out_shape=jax.ShapeDtypeStruct((r, d), F32),
        compiler_params=_cparams(("arbitrary", "arbitrary")), name="ffn_grouped",
    )(tile_expert, tile_valid, hs, wg, wu, wd, row_w)


GATHER_ROWS = 2048
GATHER_WINDOW = 32


def _gather_kernel(idx_ref, src_ref, dst_ref, sem):
    base = pl.program_id(0) * GATHER_ROWS

    def row_copy(r):
        return pltpu.make_async_copy(src_ref.at[idx_ref[r]], dst_ref.at[base + r], sem)

    def body(r, carry):
        row_copy(r).start()

        @pl.when(r >= GATHER_WINDOW)
        def _():
            row_copy(r - GATHER_WINDOW).wait()

        return carry

    lax.fori_loop(0, GATHER_ROWS, body, 0)

    def drain(r, carry):
        row_copy(r).wait()
        return carry

    lax.fori_loop(GATHER_ROWS - GATHER_WINDOW, GATHER_ROWS, drain, 0)


def _gather_rows(src, idx):
    n = idx.shape[0]
    return pl.pallas_call(
        _gather_kernel, grid=(n // GATHER_ROWS,),
        in_specs=[pl.BlockSpec((GATHER_ROWS,), lambda i: (i,), memory_space=pltpu.SMEM),
                  pl.BlockSpec(memory_space=pl.ANY)],
        out_specs=pl.BlockSpec(memory_space=pl.ANY),
        out_shape=jax.ShapeDtypeStruct((n,) + src.shape[1:], src.dtype),
        scratch_shapes=[pltpu.SemaphoreType.DMA(())],
        compiler_params=_cparams(("arbitrary",)), name="gather_rows",
    )(idx, src)


def _combine_kernel(x_ref, y0_ref, y1_ref, g_ref, o_ref):
    o_ref[...] = x_ref[...] + g_ref[...] * (y0_ref[...] + y1_ref[...])


def _combine(x, y0, y1, gate, seq):
    m, d = x.shape
    tm = 512
    tpb = seq // tm
    blk = pl.BlockSpec((tm, d), lambda i: (i, 0))
    return pl.pallas_call(
        _combine_kernel, grid=(m // tm,),
        in_specs=[blk, blk, blk, pl.BlockSpec((None, 1, d), lambda i: (i // tpb, 0, 0))],
        out_specs=blk, out_shape=jax.ShapeDtypeStruct((m, d), F32),
        compiler_params=_cparams(("arbitrary",)), name="combine",
    )(x, y0, y1, gate)


def _final_norm_kernel(x_ref, gain_ref, o_ref):
    x = x_ref[...]
    o_ref[...] = x * lax.rsqrt(jnp.mean(x * x, axis=-1, keepdims=True) + EPS) * gain_ref[...]


def _final_norm(x, gain):
    m, d = x.shape
    tm = 512
    blk = pl.BlockSpec((tm, d), lambda i: (i, 0))
    return pl.pallas_call(
        _final_norm_kernel, grid=(m // tm,),
        in_specs=[blk, pl.BlockSpec((1, d), lambda i: (0, 0))],
        out_specs=blk, out_shape=jax.ShapeDtypeStruct((m, d), F32),
        compiler_params=_cparams(("arbitrary",)), name="final_norm",
    )(x, gain.reshape(1, d))


MOE_TM = 512


def _moe(x, h_f32, route, wg, wu, wd, gate, seq):
    m, d = x.shape
    tm = MOE_TM
    rows = 2 * m + NE * tm
    n_tiles = rows // tm
    e_flat = route[:, 0:2].astype(jnp.int32).reshape(-1)
    w_flat = route[:, 2:4].reshape(-1)
    onehot = (e_flat[:, None] == jnp.arange(NE, dtype=jnp.int32)[None, :]).astype(jnp.int32)
    incl = jnp.cumsum(onehot, axis=0)
    rank = jnp.sum((incl - onehot) * onehot, axis=1)
    counts = incl[-1]
    padded = ((counts + tm - 1) // tm) * tm
    ends = jnp.cumsum(padded)
    starts = ends - padded
    pos = jnp.sum(onehot * starts[None, :], axis=1) + rank
    src_token = jnp.zeros((rows,), jnp.int32).at[pos].set(jnp.arange(2 * m, dtype=jnp.int32) // 2)
    row_w = jnp.zeros((rows,), F32).at[pos].set(w_flat)
    tile_start = jnp.arange(n_tiles, dtype=jnp.int32) * tm
    tile_expert = jnp.minimum(jnp.sum((tile_start[:, None] >= ends[None, :]).astype(jnp.int32), axis=1), NE - 1)
    tile_valid = (tile_start < ends[-1]).astype(jnp.int32)

    hs = _gather_rows(h_f32.reshape(m, d // LANES, LANES), src_token).reshape(rows, d)
    ys = _ffn_grouped(hs, wg, wu, wd, jnp.broadcast_to(row_w[:, None], (rows, LANES)),
                      tile_expert, tile_valid, tm)
    pos2 = pos.reshape(m, 2)
    back = _gather_rows(ys.reshape(rows, d // LANES, LANES), jnp.concatenate([pos2[:, 0], pos2[:, 1]]))
    back = back.reshape(2, m, d)
    return _combine(x, back[0], back[1], gate, seq)


def _regroup_w_in(w):
    dn = w[:, 0:4096]
    small = jnp.concatenate([w[:, 4096:4112], w[:, 7184:7192]], axis=1)
    fox = w[:, 4112:7184]
    swa = w[:, 7192:8472]
    gates = w[:, 8472:14616]
    main = jnp.concatenate([gates, dn, fox, swa], axis=1).astype(BF16)
    small = jnp.pad(small, ((0, 0), (0, LANES - small.shape[1]))).astype(BF16)
    return main, small


def kernel(x, c, positions, w_ada, b_ada, norm_mix, w_in, conv_w, dn_a_log, dn_dt_bias, dn_norm,
           fox_b_forget, swa_sinks, w_branch, w_out, norm_ffn, ffn_w_gate, ffn_w_up, ffn_w_down,
           moe_router, moe_w_gate, moe_w_up, moe_w_down, final_norm):
    bsz, seq, d = x.shape
    depth = w_ada.shape[0]
    m = bsz * seq
    xf = x.reshape(m, d)

    inv_freq = 10000.0 ** (-jnp.arange(0, SWA_D, 2, dtype=F32) / SWA_D)
    ang = positions.astype(F32).reshape(m, 1) * inv_freq[None, :]
    cos, sin = jnp.cos(ang), jnp.sin(ang)
    cos4 = jnp.tile(cos, (1, 4))
    sin4 = jnp.tile(jnp.concatenate([-sin, sin], axis=1), (1, 2))

    c_pad = jnp.pad(c, ((0, 8 - bsz), (0, 0)))
    mod = _ada(c_pad, w_ada, b_ada)[:, :bsz].reshape(depth, bsz, 6, 1, d)

    for layer in range(depth):
        sh1, sc1, g1, sh2, sc2, g2 = (mod[layer, :, t] for t in range(6))
        w_main, w_small = _regroup_w_in(w_in[layer])
        h = _norm_modulate(xf, norm_mix[layer], sc1, sh1, seq)
        proj = _matmul(h, w_main, BF16, 512, 2432, "in_proj")
        small = _matmul(h, w_small, F32, 1024, LANES, "in_proj_small")
        par = jnp.zeros((8, LANES), F32)
        par = par.at[0, L_G:L_G + NH].set(dn_a_log[layer])
        par = par.at[1, L_G:L_G + NH].set(dn_dt_bias[layer])
        par = par.at[1, L_F:L_F + NH].set(fox_b_forget[layer])
        col, rowf, rowg = _prep(small, par, bsz, seq)
        conv_w8 = jnp.pad(conv_w[layer], ((0, 4), (0, 0)))
        o_a = _delta(proj, conv_w8, col, rowg, dn_norm[layer], bsz, seq)
        o_b = _fox(proj, col, rowf, bsz, seq)
        o_c = _swa(proj, swa_sinks[layer], cos4, sin4, bsz, seq)
        merged = _merge(o_a, o_b, o_c, proj, w_branch[layer].astype(BF16))
        xf = _matmul_resid(merged, w_out[layer].astype(BF16), xf, g1, seq, 512, 1024, "out_proj")
        if layer % 2 == 0:
            i = layer // 2
            h2 = _norm_modulate(xf, norm_ffn[layer], sc2, sh2, seq)
            xf = _ffn_dense(h2, ffn_w_gate[i].astype(BF16), ffn_w_up[i].astype(BF16),
                            ffn_w_down[i].astype(BF16), xf, g2, seq)
        else:
            i = layer // 2
            h2, route = _norm_modulate(xf, norm_ffn[layer], sc2, sh2, seq, w_router=moe_router[i])
            xf = _moe(xf, h2, route, moe_w_gate[i].astype(BF16), moe_w_up[i].astype(BF16),
                      moe_w_down[i].astype(BF16), g2, seq)
    return _final_norm(xf, final_norm).reshape(bsz, seq, d)
```

```python
import functools

import jax
import jax.numpy as jnp
from jax import lax
from jax.experimental import pallas as pl
from jax.experimental.pallas import tpu as pltpu

F32 = jnp.float32
BF16 = jnp.bfloat16
EPS = 1e-6

D = 2048
HD = 128
NH = 8
BW = 1024
CHUNK = 64
SWA_D = 64
SWA_QH = 16
SWA_KVH = 2
SWA_G = 8
WIN = 128
D_FF = 7168
NE = 8
LANES = 128

P_GATE = 0
P_DN = 6144
P_DNZ = 9216
P_FQ, P_FK, P_FV = 10240, 11264, 12288
P_SQ, P_SK, P_SV = 13312, 14336, 14464
NP = 14592
L_BETA, L_G, L_F = 0, 8, 16

VMEM_LIMIT = 56 * 1024 * 1024


def _cparams(sem):
    return pltpu.CompilerParams(dimension_semantics=sem, vmem_limit_bytes=VMEM_LIMIT)


def _sigmoid(x):
    return 1.0 / (1.0 + jnp.exp(-x))


def _split3(x):
    hi = x.astype(BF16)
    r = x - hi.astype(F32)
    mid = r.astype(BF16)
    lo = (r - mid.astype(F32)).astype(BF16)
    return hi, mid, lo


def _dot(a, b):
    return jnp.dot(a, b, preferred_element_type=F32)


def _dot_hi(a, b):
    a0, a1, a2 = _split3(a)
    b0, b1, b2 = _split3(b)
    return (_dot(a0, b0) + (_dot(a0, b1) + _dot(a1, b0))
            + (_dot(a0, b2) + _dot(a1, b1) + _dot(a2, b0)))


SLAB = D // LANES


def _store_slabs(ref, val):
    tm = val.shape[0]
    for a in range(SLAB):
        ref[pl.ds(a, tm, stride=SLAB), :] = val[:, a * LANES:(a + 1) * LANES]


def _load_slab_chunk(ref, a, tm):
    return ref[pl.ds(a, tm, stride=SLAB), :]


def _ada_kernel(c_ref, w_ref, b_ref, o_ref):
    c = c_ref[...]
    ca = c * _sigmoid(c)
    o_ref[...] = _dot_hi(ca, w_ref[...]) + b_ref[...]


def _ada(c_pad, w_ada, b_ada):
    depth, d, n = w_ada.shape
    tn = 1536
    return pl.pallas_call(
        _ada_kernel,
        grid=(depth, n // tn),
        in_specs=[pl.BlockSpec((8, d), lambda l, j: (0, 0)),
                  pl.BlockSpec((None, d, tn), lambda l, j: (l, 0, j)),
                  pl.BlockSpec((None, 1, tn), lambda l, j: (l, 0, j))],
        out_specs=pl.BlockSpec((None, 8, tn), lambda l, j: (l, 0, j)),
        out_shape=jax.ShapeDtypeStruct((depth, 8, n), F32),
        compiler_params=_cparams(("arbitrary", "arbitrary")),
        name="ada",
    )(c_pad, w_ada, b_ada.reshape(depth, 1, n))


def _normmod(x_ref, gain_ref, sc_ref, sh_ref):
    x = x_ref[...]
    ms = jnp.mean(x * x, axis=-1, keepdims=True)
    y = x * lax.rsqrt(ms + EPS) * gain_ref[...]
    return y * (1.0 + sc_ref[...]) + sh_ref[...]


def _normmod_kernel(x_ref, gain_ref, sc_ref, sh_ref, h_ref):
    h_ref[...] = _normmod(x_ref, gain_ref, sc_ref, sh_ref).astype(BF16)


def _normmod_route_kernel(x_ref, gain_ref, sc_ref, sh_ref, wr_ref, h_ref, r_ref):
    h = _normmod(x_ref, gain_ref, sc_ref, sh_ref)
    _store_slabs(h_ref, h)
    logits = _dot_hi(h, wr_ref[...])
    lane = lax.broadcasted_iota(jnp.int32, logits.shape, 1)
    neg = jnp.float32(-jnp.inf)
    l1 = jnp.where(lane < NE, logits, neg)
    m1 = jnp.max(l1, axis=-1, keepdims=True)
    i1 = jnp.min(jnp.where(l1 == m1, lane, LANES), axis=-1, keepdims=True)
    l2 = jnp.where(lane == i1, neg, l1)
    m2 = jnp.max(l2, axis=-1, keepdims=True)
    i2 = jnp.min(jnp.where(l2 == m2, lane, LANES), axis=-1, keepdims=True)
    e = jnp.exp(m2 - m1)
    w1 = 1.0 / (1.0 + e)
    w2 = e / (1.0 + e)
    r_ref[...] = jnp.where(lane == 0, i1.astype(F32),
                           jnp.where(lane == 1, i2.astype(F32),
                                     jnp.where(lane == 2, w1, jnp.where(lane == 3, w2, 0.0))))


def _norm_modulate(x, gain, sc, sh, seq, w_router=None):
    m, d = x.shape
    tm = 512
    tpb = seq // tm
    in_specs = [pl.BlockSpec((tm, d), lambda i: (i, 0)),
                pl.BlockSpec((1, d), lambda i: (0, 0)),
                pl.BlockSpec((None, 1, d), lambda i: (i // tpb, 0, 0)),
                pl.BlockSpec((None, 1, d), lambda i: (i // tpb, 0, 0))]
    args = [x, gain.reshape(1, d), sc, sh]
    if w_router is None:
        return pl.pallas_call(
            _normmod_kernel, grid=(m // tm,), in_specs=in_specs,
            out_specs=pl.BlockSpec((tm, d), lambda i: (i, 0)),
            out_shape=jax.ShapeDtypeStruct((m, d), BF16),
            compiler_params=_cparams(("arbitrary",)), name="normmod",
        )(*args)
    wr = jnp.pad(w_router, ((0, 0), (0, LANES - NE)))
    return pl.pallas_call(
        _normmod_route_kernel, grid=(m // tm,),
        in_specs=in_specs + [pl.BlockSpec((d, LANES), lambda i: (0, 0))],
        out_specs=[pl.BlockSpec((tm * SLAB, LANES), lambda i: (i, 0)),
                   pl.BlockSpec((tm, LANES), lambda i: (i, 0))],
        out_shape=[jax.ShapeDtypeStruct((m * SLAB, LANES), F32),
                   jax.ShapeDtypeStruct((m, LANES), F32)],
        compiler_params=_cparams(("arbitrary",)), name="normmod_route",
    )(*args, wr)


def _mm_kernel(a_ref, w_ref, o_ref):
    o_ref[...] = _dot(a_ref[...], w_ref[...]).astype(o_ref.dtype)


def _matmul(a, w, out_dtype, tm, tn, name):
    m, k = a.shape
    n = w.shape[1]
    return pl.pallas_call(
        _mm_kernel, grid=(n // tn, m // tm),
        in_specs=[pl.BlockSpec((tm, k), lambda j, i: (i, 0)),
                  pl.BlockSpec((k, tn), lambda j, i: (0, j))],
        out_specs=pl.BlockSpec((tm, tn), lambda j, i: (i, j)),
        out_shape=jax.ShapeDtypeStruct((m, n), out_dtype),
        compiler_params=_cparams(("arbitrary", "arbitrary")), name=name,
    )(a, w)


def _mm_resid_kernel(a_ref, w_ref, x_ref, g_ref, o_ref):
    o_ref[...] = x_ref[...] + g_ref[...] * _dot(a_ref[...], w_ref[...])


def _matmul_resid(a, w, x, gate, seq, tm, tn, name):
    m, k = a.shape
    n = w.shape[1]
    tpb = seq // tm
    return pl.pallas_call(
        _mm_resid_kernel, grid=(n // tn, m // tm),
        in_specs=[pl.BlockSpec((tm, k), lambda j, i: (i, 0)),
                  pl.BlockSpec((k, tn), lambda j, i: (0, j)),
                  pl.BlockSpec((tm, tn), lambda j, i: (i, j)),
                  pl.BlockSpec((None, 1, tn), lambda j, i: (i // tpb, 0, j))],
        out_specs=pl.BlockSpec((tm, tn), lambda j, i: (i, j)),
        out_shape=jax.ShapeDtypeStruct((m, n), F32),
        compiler_params=_cparams(("arbitrary", "arbitrary")), name=name,
    )(a, w, x, gate)


def _prep_kernel(x_ref, par_ref, col_ref, rowf_ref, rowg_ref, carry_ref, *, ts):
    @pl.when(pl.program_id(1) == 0)
    def _():
        carry_ref[...] = jnp.zeros_like(carry_ref)

    x = x_ref[...]
    lane = lax.broadcasted_iota(jnp.int32, x.shape, 1)
    is_g = (lane >= L_G) & (lane < L_G + NH)
    is_f = (lane >= L_F) & (lane < L_F + NH)
    a_rate = jnp.exp(par_ref[0:1, :])
    z = x + par_ref[1:2, :]
    t = jnp.log1p(jnp.exp(-jnp.abs(z)))
    sig = _sigmoid(x)
    g = -a_rate * (jnp.maximum(z, 0.0) + t)
    logf = jnp.minimum(z, 0.0) - t
    vals = jnp.where(is_g, g, jnp.where(is_f, logf, 0.0))
    r = lax.broadcasted_iota(jnp.int32, (ts, ts), 0)
    c = lax.broadcasted_iota(jnp.int32, (ts, ts), 1)
    tri_full = jnp.where(c <= r, 1.0, 0.0).astype(BF16)
    tri_blk = jnp.where((c <= r) & ((r // CHUNK) == (c // CHUNK)), 1.0, 0.0).astype(BF16)
    v0, v1, v2 = _split3(vals)
    cs_full = _dot(tri_full, v0) + _dot(tri_full, v1) + _dot(tri_full, v2)
    cs_blk = _dot(tri_blk, v0) + _dot(tri_blk, v1) + _dot(tri_blk, v2)
    cum = cs_full + carry_ref[...]
    carry_ref[...] = cum[ts - 1:ts, :]
    out = jnp.where(lane < NH, sig, jnp.where(is_g, cs_blk, jnp.where(is_f, cum, 0.0)))
    col_ref[...] = out
    out_t = out.T
    rowf_ref[...] = out_t[L_F:L_F + NH, :]
    for ci in range(ts // CHUNK):
        rowg_ref[ci] = out_t[0:32, ci * CHUNK:(ci + 1) * CHUNK]


def _prep(small, par, bsz, seq):
    m = small.shape[0]
    ts = 256
    nt = seq // ts
    return pl.pallas_call(
        functools.partial(_prep_kernel, ts=ts),
        grid=(bsz, nt),
        in_specs=[pl.BlockSpec((ts, LANES), lambda b, j: (b * nt + j, 0)),
                  pl.BlockSpec((8, LANES), lambda b, j: (0, 0))],
        out_specs=[pl.BlockSpec((ts, LANES), lambda b, j: (b * nt + j, 0)),
                   pl.BlockSpec((NH, ts), lambda b, j: (0, b * nt + j)),
                   pl.BlockSpec((ts // CHUNK, 32, CHUNK), lambda b, j: (b * nt + j, 0, 0))],
        out_shape=[jax.ShapeDtypeStruct((m, LANES), F32),
                   jax.ShapeDtypeStruct((NH, m), F32),
                   jax.ShapeDtypeStruct((m // CHUNK, 32, CHUNK), F32)],
        scratch_shapes=[pltpu.VMEM((1, LANES), F32)],
        compiler_params=_cparams(("arbitrary", "arbitrary")), name="prep",
    )(small, par)


def _bmm(a, b):
    return lax.dot_general(a.astype(BF16), b.astype(BF16), (((2,), (1,)), ((0,), (0,))),
                           preferred_element_type=F32)


def _bmm_nt(a, b):
    return lax.dot_general(a.astype(BF16), b.astype(BF16), (((2,), (2,)), ((0,), (0,))),
                           preferred_element_type=F32)


def _delta_kernel(qkv_ref, halo_ref, z_ref, cw_ref, col_ref, rowg_ref, gain_ref, o_ref, s_ref, *, nc):
    j = pl.program_id(1)

    @pl.when(j == 0)
    def _():
        s_ref[...] = jnp.zeros_like(s_ref)

    ts = nc * CHUNK
    n = NH * nc
    x = qkv_ref[...].astype(F32)
    halo = jnp.where(j == 0, 0.0, halo_ref[...].astype(F32))
    xx = jnp.concatenate([halo, x], axis=0)
    cw = cw_ref[...]
    acc = x * cw[3:4, :]
    for dlt in (1, 2, 3):
        acc = acc + pltpu.roll(xx, dlt, axis=0)[8:, :] * cw[3 - dlt:4 - dlt, :]
    act = acc * _sigmoid(acc)

    cv = col_ref[...]
    rg = rowg_ref[...]
    qs, ks, vs, betas, gcs, grs = [], [], [], [], [], []
    for h in range(NH):
        qh = act[:, h * HD:(h + 1) * HD]
        kh = act[:, BW + h * HD:BW + (h + 1) * HD]
        vh = act[:, 2 * BW + h * HD:2 * BW + (h + 1) * HD]
        qh = qh * (lax.rsqrt(jnp.sum(qh * qh, axis=-1, keepdims=True) + EPS) * (HD ** -0.5))
        kh = kh * lax.rsqrt(jnp.sum(kh * kh, axis=-1, keepdims=True) + EPS)
        qs.append(qh.reshape(nc, CHUNK, HD))
        ks.append(kh.reshape(nc, CHUNK, HD))
        vs.append(vh.reshape(nc, CHUNK, HD))
        betas.append(cv[:, L_BETA + h:L_BETA + h + 1].reshape(nc, CHUNK, 1))
        gcs.append(cv[:, L_G + h:L_G + h + 1].reshape(nc, CHUNK, 1))
        grs.append(rg[:, L_G + h:L_G + h + 1, :])
    q = jnp.concatenate(qs, axis=0)
    k = jnp.concatenate(ks, axis=0)
    v = jnp.concatenate(vs, axis=0)
    beta = jnp.concatenate(betas, axis=0)
    gc = jnp.concatenate(gcs, axis=0)
    gr = jnp.concatenate(grs, axis=0)

    ri = lax.broadcasted_iota(jnp.int32, (CHUNK, CHUNK), 0)
    ci = lax.broadcasted_iota(jnp.int32, (CHUNK, CHUNK), 1)
    causal = (ci <= ri)[None]
    strict = (ci < ri)[None]
    decay = jnp.exp(jnp.where(causal, gc - gr, -jnp.inf))
    kb = k * beta
    lower = jnp.where(strict, _bmm_nt(kb, k) * decay, 0.0)
    qk = _bmm_nt(q, k) * decay

    eye = jnp.where(ci == ri, 1.0, 0.0)[None]
    t_inv = eye - lower
    pw = _bmm(lower, lower)
    for it in range(5):
        t_inv = t_inv + _bmm(t_inv, pw)
        if it < 4:
            pw = _bmm(pw, pw)

    eg = jnp.exp(gc)
    rhs = jnp.concatenate([v * beta, kb * eg], axis=-1)
    sol = _bmm(t_inv, rhs)
    u = sol[:, :, :HD]
    w = sol[:, :, HD:]
    q_dec = q * eg
    g_end = gc[:, CHUNK - 1:CHUNK, :]
    k_dec = k * jnp.exp(g_end - gc)
    g_last = jnp.exp(g_end)

    def pick(t, c):
        return t.reshape((NH, nc) + t.shape[1:])[:, c]

    state = s_ref[...]
    outs = []
    for c in range(nc):
        v_new = pick(u, c) - _bmm(pick(w, c), state)
        o_c = _bmm(pick(q_dec, c), state) + _bmm(pick(qk, c), v_new)
        kd_t = jnp.swapaxes(pick(k_dec, c), 1, 2)
        state = state * pick(g_last, c) + _bmm(kd_t, v_new)
        outs.append(o_c)
    s_ref[...] = state

    gain = gain_ref[...]
    zf = z_ref[...].astype(F32)
    cols = []
    for h in range(NH):
        oh = jnp.concatenate([outs[c][h] for c in range(nc)], axis=0)
        oh = oh * lax.rsqrt(jnp.mean(oh * oh, axis=-1, keepdims=True) + EPS) * gain
        zh = zf[:, h * HD:(h + 1) * HD]
        cols.append(oh * (zh * _sigmoid(zh)))
    o_ref[...] = jnp.concatenate(cols, axis=1).astype(BF16)


def _delta(proj, conv_w8, col, rowg, gain, bsz, seq):
    m = proj.shape[0]
    nc = 2
    ts = nc * CHUNK
    nt = seq // ts
    qkv_blk = P_DN // (3 * BW)
    return pl.pallas_call(
        functools.partial(_delta_kernel, nc=nc),
        grid=(bsz, nt),
        in_specs=[pl.BlockSpec((ts, 3 * BW), lambda b, j: (b * nt + j, qkv_blk)),
                  pl.BlockSpec((8, 3 * BW), lambda b, j: (jnp.maximum((b * nt + j) * (ts // 8) - 1, 0), qkv_blk)),
                  pl.BlockSpec((ts, BW), lambda b, j: (b * nt + j, P_DNZ // BW)),
                  pl.BlockSpec((8, 3 * BW), lambda b, j: (0, 0)),
                  pl.BlockSpec((ts, LANES), lambda b, j: (b * nt + j, 0)),
                  pl.BlockSpec((nc, 32, CHUNK), lambda b, j: (b * nt + j, 0, 0)),
                  pl.BlockSpec((1, HD), lambda b, j: (0, 0))],
        out_specs=pl.BlockSpec((ts, BW), lambda b, j: (b * nt + j, 0)),
        out_shape=jax.ShapeDtypeStruct((m, BW), BF16),
        scratch_shapes=[pltpu.VMEM((NH, HD, HD), F32)],
        compiler_params=_cparams(("arbitrary", "arbitrary")), name="delta",
    )(proj, proj, proj, conv_w8, col, rowg, gain.reshape(1, HD))


FOX_T = 512


def _fox_kernel(q_ref, k_ref, v_ref, crow_ref, o_ref, m_ref, l_ref, acc_ref):
    t = FOX_T
    h = pl.program_id(1)
    qi = pl.program_id(2)
    m_ref[...] = jnp.full_like(m_ref, -jnp.inf)
    l_ref[...] = jnp.zeros_like(l_ref)
    acc_ref[...] = jnp.zeros_like(acc_ref)
    q = q_ref[...]

    def block(ki, masked):
        off = pl.multiple_of(ki * t, t)
        s = lax.dot_general(q, k_ref[pl.ds(off, t), :], (((1,), (1,)), ((), ())),
                            preferred_element_type=F32) * (HD ** -0.5)
        s = s - crow_ref[pl.ds(h, 1), pl.ds(off, t)]
        if masked:
            row = lax.broadcasted_iota(jnp.int32, (t, t), 0)
            col = lax.broadcasted_iota(jnp.int32, (t, t), 1)
            s = jnp.where(col <= row, s, -jnp.inf)
        m_prev = m_ref[...]
        m_new = jnp.maximum(m_prev, jnp.max(s, axis=1, keepdims=True))
        alpha = jnp.exp(m_prev - m_new)
        p = jnp.exp(s - jnp.tile(m_new, (1, t // LANES)))
        l_ref[...] = alpha * l_ref[...] + jnp.sum(p, axis=1, keepdims=True)
        acc_ref[...] = alpha * acc_ref[...] + _dot(p.astype(BF16), v_ref[pl.ds(off, t), :])
        m_ref[...] = m_new

    def body(ki, carry):
        block(ki, False)
        return carry

    lax.fori_loop(0, qi, body, 0)
    block(qi, True)
    o_ref[...] = (acc_ref[...] / l_ref[...]).astype(BF16)


def _fox(proj, rowf, bsz, seq):
    m = proj.shape[0]
    t = FOX_T
    nq = seq // t
    return pl.pallas_call(
        _fox_kernel,
        grid=(bsz, NH, nq),
        in_specs=[pl.BlockSpec((t, HD), lambda b, h, qi: (b * nq + qi, P_FQ // HD + h)),
                  pl.BlockSpec((seq, HD), lambda b, h, qi: (b, P_FK // HD + h)),
                  pl.BlockSpec((seq, HD), lambda b, h, qi: (b, P_FV // HD + h)),
                  pl.BlockSpec((NH, seq), lambda b, h, qi: (0, b))],
        out_specs=pl.BlockSpec((t, HD), lambda b, h, qi: (b * nq + qi, h)),
        out_shape=jax.ShapeDtypeStruct((m, BW), BF16),
        scratch_shapes=[pltpu.VMEM((t, HD), F32), pltpu.VMEM((t, HD), F32), pltpu.VMEM((t, HD), F32)],
        compiler_params=_cparams(("arbitrary", "arbitrary", "arbitrary")), name="fox",
    )(proj, proj, proj, rowf)


def _swap_halves(x):
    w = x.shape[-1]
    lane = lax.broadcasted_iota(jnp.int32, x.shape, x.ndim - 1)
    return jnp.where((lane % SWA_D) < SWA_D // 2, pltpu.roll(x, w - SWA_D // 2, axis=x.ndim - 1),
                     pltpu.roll(x, SWA_D // 2, axis=x.ndim - 1))


def _swa_kernel(sink_ref, q_ref, kc_ref, kp_ref, vc_ref, vp_ref, cc_ref, sc_ref, cp_ref, sp_ref, o_ref, *, nblk):
    i = pl.program_id(0)
    first = (i % nblk) == 0
    cos_c, sin_c = cc_ref[...], sc_ref[...]
    q = q_ref[...].astype(F32)
    q = q * jnp.tile(cos_c, (1, SWA_QH // 2)) + _swap_halves(q) * jnp.tile(sin_c, (1, SWA_QH // 2))
    kc = kc_ref[...].astype(F32)
    kc = kc * cos_c + _swap_halves(kc) * sin_c
    kp = kp_ref[...].astype(F32)
    kp = kp * cp_ref[...] + _swap_halves(kp) * sp_ref[...]
    kk = jnp.concatenate([kp, kc], axis=0).astype(BF16)
    vv = jnp.concatenate([vp_ref[...], vc_ref[...]], axis=0)
    r = lax.broadcasted_iota(jnp.int32, (WIN, 2 * WIN), 0)
    c = lax.broadcasted_iota(jnp.int32, (WIN, 2 * WIN), 1)
    mask = (c > r) & (c <= r + WIN) & ((c >= WIN) | jnp.logical_not(first))
    qb = q.astype(BF16)
    outs = []
    for hq in range(SWA_QH):
        g = hq // SWA_G
        qh = qb[:, hq * SWA_D:(hq + 1) * SWA_D]
        kh = kk[:, g * SWA_D:(g + 1) * SWA_D]
        vh = vv[:, g * SWA_D:(g + 1) * SWA_D]
        s = lax.dot_general(qh, kh, (((1,), (1,)), ((), ())), preferred_element_type=F32) * (SWA_D ** -0.5)
        s = jnp.where(mask, s, -jnp.inf)
        sink = sink_ref[hq]
        mx = jnp.maximum(jnp.max(s, axis=1, keepdims=True), sink)
        p = jnp.exp(s - mx)
        den = jnp.sum(p, axis=1, keepdims=True) + jnp.exp(sink - mx)
        outs.append(_dot(p.astype(BF16), vh) / den)
    o_ref[...] = jnp.concatenate(outs, axis=1).astype(BF16)


def _swa(proj, sinks, cos4, sin4, bsz, seq):
    m = proj.shape[0]
    nblk = seq // WIN
    prev = lambda i: jnp.maximum(i - 1, 0)
    kcol, vcol = P_SK // LANES, P_SV // LANES
    return pl.pallas_call(
        functools.partial(_swa_kernel, nblk=nblk),
        grid=(m // WIN,),
        in_specs=[pl.BlockSpec(memory_space=pltpu.SMEM),
                  pl.BlockSpec((WIN, BW), lambda i: (i, P_SQ // BW)),
                  pl.BlockSpec((WIN, LANES), lambda i: (i, kcol)),
                  pl.BlockSpec((WIN, LANES), lambda i: (prev(i), kcol)),
                  pl.BlockSpec((WIN, LANES), lambda i: (i, vcol)),
                  pl.BlockSpec((WIN, LANES), lambda i: (prev(i), vcol)),
                  pl.BlockSpec((WIN, LANES), lambda i: (i, 0)),
                  pl.BlockSpec((WIN, LANES), lambda i: (i, 0)),
                  pl.BlockSpec((WIN, LANES), lambda i: (prev(i), 0)),
                  pl.BlockSpec((WIN, LANES), lambda i: (prev(i), 0))],
        out_specs=pl.BlockSpec((WIN, BW), lambda i: (i, 0)),
        out_shape=jax.ShapeDtypeStruct((m, BW), BF16),
        compiler_params=_cparams(("arbitrary",)), name="swa",
    )(sinks, proj, proj, proj, proj, proj, cos4, sin4, cos4, sin4)


def _merge_kernel(oa_ref, ob_ref, oc_ref, ga_ref, gb_ref, gc_ref, w_ref, o_ref):
    acc = _sigmoid(ga_ref[...].astype(F32)) * _dot(oa_ref[...], w_ref[0])
    acc = acc + _sigmoid(gb_ref[...].astype(F32)) * _dot(ob_ref[...], w_ref[1])
    acc = acc + _sigmoid(gc_ref[...].astype(F32)) * _dot(oc_ref[...], w_ref[2])
    o_ref[...] = acc.astype(BF16)


def _merge(o_a, o_b, o_c, proj, w_branch):
    m = o_a.shape[0]
    tm, tn = 512, 512
    gblk = lambda b: (lambda j, i: (i, (P_GATE + b * D) // tn + j))
    oblk = pl.BlockSpec((tm, BW), lambda j, i: (i, 0))
    return pl.pallas_call(
        _merge_kernel, grid=(D // tn, m // tm),
        in_specs=[oblk, oblk, oblk,
                  pl.BlockSpec((tm, tn), gblk(0)), pl.BlockSpec((tm, tn), gblk(1)), pl.BlockSpec((tm, tn), gblk(2)),
                  pl.BlockSpec((3, BW, tn), lambda j, i: (0, 0, j))],
        out_specs=pl.BlockSpec((tm, tn), lambda j, i: (i, j)),
        out_shape=jax.ShapeDtypeStruct((m, D), BF16),
        compiler_params=_cparams(("arbitrary", "arbitrary")), name="merge",
    )(o_a, o_b, o_c, proj, proj, proj, w_branch)


def _swiglu_acc(h, wg_ref, wu_ref, wd_ref, acc_ref):
    a = _dot(h, wg_ref[...])
    b = _dot(h, wu_ref[...])
    t = (a * _sigmoid(a) * b).astype(BF16)
    acc_ref[...] += _dot(t, wd_ref[...])


def _ffn_kernel(h_ref, wg_ref, wu_ref, wd_ref, x_ref, g_ref, o_ref, acc_ref, *, nf):
    f = pl.program_id(1)

    @pl.when(f == 0)
    def _():
        acc_ref[...] = jnp.zeros_like(acc_ref)

    _swiglu_acc(h_ref[...], wg_ref, wu_ref, wd_ref, acc_ref)

    @pl.when(f == nf - 1)
    def _():
        o_ref[...] = x_ref[...] + g_ref[...] * acc_ref[...]


def _ffn_dense(h, wg, wu, wd, x, gate, seq):
    m, d = h.shape
    ff = wg.shape[1]
    tm, tf = 512, 512
    nf = ff // tf
    tpb = seq // tm
    return pl.pallas_call(
        functools.partial(_ffn_kernel, nf=nf), grid=(m // tm, nf),
        in_specs=[pl.BlockSpec((tm, d), lambda i, f: (i, 0)),
                  pl.BlockSpec((d, tf), lambda i, f: (0, f)),
                  pl.BlockSpec((d, tf), lambda i, f: (0, f)),
                  pl.BlockSpec((tf, d), lambda i, f: (f, 0)),
                  pl.BlockSpec((tm, d), lambda i, f: (i, 0)),
                  pl.BlockSpec((None, 1, d), lambda i, f: (i // tpb, 0, 0))],
        out_specs=pl.BlockSpec((tm, d), lambda i, f: (i, 0)),
        out_shape=jax.ShapeDtypeStruct((m, d), F32),
        scratch_shapes=[pltpu.VMEM((tm, d), F32)],
        compiler_params=_cparams(("arbitrary", "arbitrary")), name="ffn_dense",
    )(h, wg, wu, wd, x, gate)


def _start_row_gather(src_hbm, idx_ref, base, n, dst, sem):
    def body(r, carry):
        s = pl.multiple_of(idx_ref[base + r], SLAB)
        pltpu.make_async_copy(src_hbm.at[pl.ds(s, SLAB), :],
                              dst.at[pl.ds(pl.multiple_of(r * SLAB, SLAB), SLAB), :], sem).start()
        return carry

    lax.fori_loop(0, n, body, 0)


def _wait_row_gather(src_hbm, n, dst, sem):
    pltpu.make_async_copy(src_hbm.at[pl.ds(0, n * SLAB), :], dst, sem).wait()


def _ffn_group_kernel(te_ref, tv_ref, src_ref, h_hbm, wg_ref, wu_ref, wd_ref, rw_ref, o_ref,
                      buf, hbf_ref, acc_ref, sem, *, nf, tm, n_tiles):
    i = pl.program_id(0)
    f = pl.program_id(1)
    slot = i % 2
    live = tv_ref[i] > 0

    @pl.when(f == 0)
    def _():
        acc_ref[...] = jnp.zeros_like(acc_ref)

        @pl.when((i == 0) & live)
        def _():
            _start_row_gather(h_hbm, src_ref, 0, tm, buf.at[0], sem.at[0])

        nxt = jnp.minimum(i + 1, n_tiles - 1)

        @pl.when((i + 1 < n_tiles) & (tv_ref[nxt] > 0))
        def _():
            _start_row_gather(h_hbm, src_ref, nxt * tm, tm, buf.at[1 - slot], sem.at[1 - slot])

        @pl.when(live)
        def _():
            _wait_row_gather(h_hbm, tm, buf.at[slot], sem.at[slot])
            for a in range(SLAB):
                hbf_ref[:, a * LANES:(a + 1) * LANES] = _load_slab_chunk(buf.at[slot], a, tm).astype(BF16)

    @pl.when(live)
    def _():
        _swiglu_acc(hbf_ref[...], wg_ref, wu_ref, wd_ref, acc_ref)

    @pl.when(f == nf - 1)
    def _():
        _store_slabs(o_ref, acc_ref[...] * rw_ref[:, 0:1])


def _ffn_grouped(h_slabs, wg, wu, wd, row_w, tile_expert, tile_valid, src_rows, tm):
    rows = src_rows.shape[0]
    d, ff = wg.shape[1], wg.shape[2]
    tf = 512
    nf = ff // tf
    n_tiles = rows // tm

    def fsel(i, f, tv):
        return jnp.where(tv[i] > 0, f, nf - 1)

    grid_spec = pltpu.PrefetchScalarGridSpec(
        num_scalar_prefetch=3, grid=(n_tiles, nf),
        in_specs=[pl.BlockSpec(memory_space=pl.ANY),
                  pl.BlockSpec((None, d, tf), lambda i, f, te, tv, sr: (te[i], 0, fsel(i, f, tv))),
                  pl.BlockSpec((None, d, tf), lambda i, f, te, tv, sr: (te[i], 0, fsel(i, f, tv))),
                  pl.BlockSpec((None, tf, d), lambda i, f, te, tv, sr: (te[i], fsel(i, f, tv), 0)),
                  pl.BlockSpec((tm, LANES), lambda i, f, te, tv, sr: (i, 0))],
        out_specs=pl.BlockSpec((tm * SLAB, LANES), lambda i, f, te, tv, sr: (i, 0)),
        scratch_shapes=[pltpu.VMEM((2, tm * SLAB, LANES), F32),
                        pltpu.VMEM((tm, d), BF16),
                        pltpu.VMEM((tm, d), F32),
                        pltpu.SemaphoreType.DMA((2,))])
    return pl.pallas_call(
        functools.partial(_ffn_group_kernel, nf=nf, tm=tm, n_tiles=n_tiles), grid_spec=grid_spec,
        out_shape=jax.ShapeDtypeStruct((rows * SLAB, LANES), F32),
        compiler_params=_cparams(("arbitrary", "arbitrary")), name="ffn_grouped",
    )(tile_expert, tile_valid, src_rows, h_slabs, wg, wu, wd, row_w)


COMBINE_TM = 256


def _combine_kernel(pos_ref, x_ref, y_hbm, g_ref, gain_ref, o_ref, buf, sem, *, tm, nt, m, final):
    i = pl.program_id(0)
    slot = i % 2

    def start(tile, s):
        for k in range(2):
            _start_row_gather(y_hbm, pos_ref, k * m + tile * tm, tm, buf.at[s, k], sem.at[s])

    @pl.when(i == 0)
    def _():
        start(0, 0)

    @pl.when(i + 1 < nt)
    def _():
        start(jnp.minimum(i + 1, nt - 1), 1 - slot)

    for k in range(2):
        _wait_row_gather(y_hbm, tm, buf.at[slot, k], sem.at[slot])
    ss = jnp.zeros((tm, 1), F32)
    for a in range(SLAB):
        y = _load_slab_chunk(buf.at[slot, 0], a, tm) + _load_slab_chunk(buf.at[slot, 1], a, tm)
        xn = x_ref[:, a * LANES:(a + 1) * LANES] + g_ref[:, a * LANES:(a + 1) * LANES] * y
        ss = ss + jnp.sum(xn * xn, axis=-1, keepdims=True)
        o_ref[:, a * LANES:(a + 1) * LANES] = xn
    if final:
        inv = lax.rsqrt(ss * (1.0 / D) + EPS)
        for a in range(SLAB):
            cols = slice(a * LANES, (a + 1) * LANES)
            o_ref[:, cols] = o_ref[:, cols] * inv * gain_ref[:, cols]


def _combine(x, y_slabs, pos_rows, gate, seq, final_gain=None):
    m, d = x.shape
    tm = COMBINE_TM
    nt = m // tm
    tpb = seq // tm
    final = final_gain is not None
    gain = (final_gain if final else jnp.ones((d,), F32)).reshape(1, d)
    grid_spec = pltpu.PrefetchScalarGridSpec(
        num_scalar_prefetch=1, grid=(nt,),
        in_specs=[pl.BlockSpec((tm, d), lambda i, p: (i, 0)),
                  pl.BlockSpec(memory_space=pl.ANY),
                  pl.BlockSpec((None, 1, d), lambda i, p: (i // tpb, 0, 0)),
                  pl.BlockSpec((1, d), lambda i, p: (0, 0))],
        out_specs=pl.BlockSpec((tm, d), lambda i, p: (i, 0)),
        scratch_shapes=[pltpu.VMEM((2, 2, tm * SLAB, LANES), F32), pltpu.SemaphoreType.DMA((2,))])
    return pl.pallas_call(
        functools.partial(_combine_kernel, tm=tm, nt=nt, m=m, final=final), grid_spec=grid_spec,
        out_shape=jax.ShapeDtypeStruct((m, d), F32),
        compiler_params=_cparams(("arbitrary",)), name="combine",
    )(pos_rows, x, y_slabs, gate, gain)


def _final_norm_kernel(x_ref, gain_ref, o_ref):
    x = x_ref[...]
    o_ref[...] = x * lax.rsqrt(jnp.mean(x * x, axis=-1, keepdims=True) + EPS) * gain_ref[...]


def _final_norm(x, gain):
    m, d = x.shape
    tm = 512
    blk = pl.BlockSpec((tm, d), lambda i: (i, 0))
    return pl.pallas_call(
        _final_norm_kernel, grid=(m // tm,),
        in_specs=[blk, pl.BlockSpec((1, d), lambda i: (0, 0))],
        out_specs=blk, out_shape=jax.ShapeDtypeStruct((m, d), F32),
        compiler_params=_cparams(("arbitrary",)), name="final_norm",
    )(x, gain.reshape(1, d))


MOE_TM = 512


def _moe(x, h_slabs, route, wg, wu, wd, gate, seq, final_gain=None):
    m, d = x.shape
    tm = MOE_TM
    rows = 2 * m + NE * tm
    n_tiles = rows // tm
    e_flat = route[:, 0:2].astype(jnp.int32).reshape(-1)
    w_flat = route[:, 2:4].reshape(-1)
    onehot = (e_flat[:, None] == jnp.arange(NE, dtype=jnp.int32)[None, :]).astype(jnp.int32)
    incl = jnp.cumsum(onehot, axis=0)
    rank = jnp.sum((incl - onehot) * onehot, axis=1)
    counts = incl[-1]
    padded = ((counts + tm - 1) // tm) * tm
    ends = jnp.cumsum(padded)
    starts = ends - padded
    pos = jnp.sum(onehot * starts[None, :], axis=1) + rank
    src_token = jnp.zeros((rows,), jnp.int32).at[pos].set(jnp.arange(2 * m, dtype=jnp.int32) // 2)
    row_w = jnp.zeros((rows,), F32).at[pos].set(w_flat)
    tile_start = jnp.arange(n_tiles, dtype=jnp.int32) * tm
    tile_expert = jnp.minimum(jnp.sum((tile_start[:, None] >= ends[None, :]).astype(jnp.int32), axis=1), NE - 1)
    tile_valid = (tile_start < ends[-1]).astype(jnp.int32)

    ys = _ffn_grouped(h_slabs, wg, wu, wd, jnp.broadcast_to(row_w[:, None], (rows, LANES)),
                      tile_expert, tile_valid, src_token * SLAB, tm)
    pos_rows = pos.reshape(m, 2).T.reshape(-1) * SLAB
    return _combine(x, ys, pos_rows, gate, seq, final_gain)


def _regroup_w_in(w):
    dn = w[:, 0:4096]
    small = jnp.concatenate([w[:, 4096:4112], w[:, 7184:7192]], axis=1)
    fox = w[:, 4112:7184]
    swa = w[:, 7192:8472]
    gates = w[:, 8472:14616]
    main = jnp.concatenate([gates, dn, fox, swa], axis=1).astype(BF16)
    small = jnp.pad(small, ((0, 0), (0, LANES - small.shape[1]))).astype(BF16)
    return main, small


def kernel(x, c, positions, w_ada, b_ada, norm_mix, w_in, conv_w, dn_a_log, dn_dt_bias, dn_norm,
           fox_b_forget, swa_sinks, w_branch, w_out, norm_ffn, ffn_w_gate, ffn_w_up, ffn_w_down,
           moe_router, moe_w_gate, moe_w_up, moe_w_down, final_norm):
    bsz, seq, d = x.shape
    depth = w_ada.shape[0]
    m = bsz * seq
    xf = x.reshape(m, d)

    inv_freq = 10000.0 ** (-jnp.arange(0, SWA_D, 2, dtype=F32) / SWA_D)
    ang = positions.astype(F32).reshape(m, 1) * inv_freq[None, :]
    cos, sin = jnp.cos(ang), jnp.sin(ang)
    cos4 = jnp.tile(cos, (1, 4))
    sin4 = jnp.tile(jnp.concatenate([-sin, sin], axis=1), (1, 2))

    c_pad = jnp.pad(c, ((0, 8 - bsz), (0, 0)))
    mod = _ada(c_pad, w_ada, b_ada)[:, :bsz].reshape(depth, bsz, 6, 1, d)

    for layer in range(depth):
        sh1, sc1, g1, sh2, sc2, g2 = (mod[layer, :, t] for t in range(6))
        w_main, w_small = _regroup_w_in(w_in[layer])
        h = _norm_modulate(xf, norm_mix[layer], sc1, sh1, seq)
        proj = _matmul(h, w_main, BF16, 512, 2432, "in_proj")
        small = _matmul(h, w_small, F32, 1024, LANES, "in_proj_small")
        par = jnp.zeros((8, LANES), F32)
        par = par.at[0, L_G:L_G + NH].set(dn_a_log[layer])
        par = par.at[1, L_G:L_G + NH].set(dn_dt_bias[layer])
        par = par.at[1, L_F:L_F + NH].set(fox_b_forget[layer])
        col, rowf, rowg = _prep(small, par, bsz, seq)
        conv_w8 = jnp.pad(conv_w[layer], ((0, 4), (0, 0)))
        o_a = _delta(proj, conv_w8, col, rowg, dn_norm[layer], bsz, seq)
        o_b = _fox(proj, rowf, bsz, seq)
        o_c = _swa(proj, swa_sinks[layer], cos4, sin4, bsz, seq)
        merged = _merge(o_a, o_b, o_c, proj, w_branch[layer].astype(BF16))
        xf = _matmul_resid(merged, w_out[layer].astype(BF16), xf, g1, seq, 512, 1024, "out_proj")
        if layer % 2 == 0:
            i = layer // 2
            h2 = _norm_modulate(xf, norm_ffn[layer], sc2, sh2, seq)
            xf = _ffn_dense(h2, ffn_w_gate[i].astype(BF16), ffn_w_up[i].astype(BF16),
                            ffn_w_down[i].astype(BF16), xf, g2, seq)
        else:
            i = layer // 2
            h2, route = _norm_modulate(xf, norm_ffn[layer], sc2, sh2, seq, w_router=moe_router[i])
            last = layer == depth - 1
            xf = _moe(xf, h2, route, moe_w_gate[i].astype(BF16), moe_w_up[i].astype(BF16),
                      moe_w_down[i].astype(BF16), g2, seq, final_gain=final_norm if last else None)
            if last:
                return xf.reshape(bsz, seq, d)
    return _final_norm(xf, final_norm).reshape(bsz, seq, d)
```

```python
import functools

import jax
import jax.numpy as jnp
from jax import lax
from jax.experimental import pallas as pl
from jax.experimental.pallas import tpu as pltpu

F32 = jnp.float32
BF16 = jnp.bfloat16
EPS = 1e-6

D = 2048
HD = 128
NH = 8
BW = 1024
CHUNK = 64
SWA_D = 64
SWA_QH = 16
SWA_KVH = 2
SWA_G = 8
WIN = 128
D_FF = 7168
NE = 8
LANES = 128

P_GATE = 0
P_DN = 6144
P_DNZ = 9216
P_FQ, P_FK, P_FV = 10240, 11264, 12288
P_SQ, P_SK, P_SV = 13312, 14336, 14464
NP = 14592
L_BETA, L_G, L_F = 0, 8, 16

VMEM_LIMIT = 56 * 1024 * 1024


def _cparams(sem):
    return pltpu.CompilerParams(dimension_semantics=sem, vmem_limit_bytes=VMEM_LIMIT)


def _sigmoid(x):
    return 1.0 / (1.0 + jnp.exp(-x))


def _split3(x):
    hi = x.astype(BF16)
    r = x - hi.astype(F32)
    mid = r.astype(BF16)
    lo = (r - mid.astype(F32)).astype(BF16)
    return hi, mid, lo


def _dot(a, b):
    return jnp.dot(a, b, preferred_element_type=F32)


def _dot_hi(a, b):
    a0, a1, a2 = _split3(a)
    b0, b1, b2 = _split3(b)
    return (_dot(a0, b0) + (_dot(a0, b1) + _dot(a1, b0))
            + (_dot(a0, b2) + _dot(a1, b1) + _dot(a2, b0)))


def _ada_kernel(c_ref, w_ref, b_ref, o_ref):
    c = c_ref[...]
    ca = c * _sigmoid(c)
    o_ref[...] = _dot_hi(ca, w_ref[...]) + b_ref[...]


def _ada(c_pad, w_ada, b_ada):
    depth, d, n = w_ada.shape
    tn = 1536
    return pl.pallas_call(
        _ada_kernel,
        grid=(depth, n // tn),
        in_specs=[pl.BlockSpec((8, d), lambda l, j: (0, 0)),
                  pl.BlockSpec((None, d, tn), lambda l, j: (l, 0, j)),
                  pl.BlockSpec((None, 1, tn), lambda l, j: (l, 0, j))],
        out_specs=pl.BlockSpec((None, 8, tn), lambda l, j: (l, 0, j)),
        out_shape=jax.ShapeDtypeStruct((depth, 8, n), F32),
        compiler_params=_cparams(("arbitrary", "arbitrary")),
        name="ada",
    )(c_pad, w_ada, b_ada.reshape(depth, 1, n))


def _normmod(x_ref, gain_ref, sc_ref, sh_ref):
    x = x_ref[...]
    ms = jnp.mean(x * x, axis=-1, keepdims=True)
    y = x * lax.rsqrt(ms + EPS) * gain_ref[...]
    return y * (1.0 + sc_ref[...]) + sh_ref[...]


def _normmod_kernel(x_ref, gain_ref, sc_ref, sh_ref, h_ref):
    h_ref[...] = _normmod(x_ref, gain_ref, sc_ref, sh_ref).astype(BF16)


def _normmod_route_kernel(x_ref, gain_ref, sc_ref, sh_ref, wr_ref, h_ref, r_ref):
    h = _normmod(x_ref, gain_ref, sc_ref, sh_ref)
    h_ref[...] = h
    logits = _dot_hi(h, wr_ref[...])
    lane = lax.broadcasted_iota(jnp.int32, logits.shape, 1)
    neg = jnp.float32(-jnp.inf)
    l1 = jnp.where(lane < NE, logits, neg)
    m1 = jnp.max(l1, axis=-1, keepdims=True)
    i1 = jnp.min(jnp.where(l1 == m1, lane, LANES), axis=-1, keepdims=True)
    l2 = jnp.where(lane == i1, neg, l1)
    m2 = jnp.max(l2, axis=-1, keepdims=True)
    i2 = jnp.min(jnp.where(l2 == m2, lane, LANES), axis=-1, keepdims=True)
    e = jnp.exp(m2 - m1)
    w1 = 1.0 / (1.0 + e)
    w2 = e / (1.0 + e)
    r_ref[...] = jnp.where(lane == 0, i1.astype(F32),
                           jnp.where(lane == 1, i2.astype(F32),
                                     jnp.where(lane == 2, w1, jnp.where(lane == 3, w2, 0.0))))


def _norm_modulate(x, gain, sc, sh, seq, w_router=None):
    m, d = x.shape
    tm = 512
    tpb = seq // tm
    in_specs = [pl.BlockSpec((tm, d), lambda i: (i, 0)),
                pl.BlockSpec((1, d), lambda i: (0, 0)),
                pl.BlockSpec((None, 1, d), lambda i: (i // tpb, 0, 0)),
                pl.BlockSpec((None, 1, d), lambda i: (i // tpb, 0, 0))]
    args = [x, gain.reshape(1, d), sc, sh]
    if w_router is None:
        return pl.pallas_call(
            _normmod_kernel, grid=(m // tm,), in_specs=in_specs,
            out_specs=pl.BlockSpec((tm, d), lambda i: (i, 0)),
            out_shape=jax.ShapeDtypeStruct((m, d), BF16),
            compiler_params=_cparams(("arbitrary",)), name="normmod",
        )(*args)
    wr = jnp.pad(w_router, ((0, 0), (0, LANES - NE)))
    return pl.pallas_call(
        _normmod_route_kernel, grid=(m // tm,),
        in_specs=in_specs + [pl.BlockSpec((d, LANES), lambda i: (0, 0))],
        out_specs=[pl.BlockSpec((tm, d), lambda i: (i, 0)),
                   pl.BlockSpec((tm, LANES), lambda i: (i, 0))],
        out_shape=[jax.ShapeDtypeStruct((m, d), F32),
                   jax.ShapeDtypeStruct((m, LANES), F32)],
        compiler_params=_cparams(("arbitrary",)), name="normmod_route",
    )(*args, wr)


def _mm_kernel(a_ref, w_ref, o_ref):
    o_ref[...] = _dot(a_ref[...], w_ref[...]).astype(o_ref.dtype)


def _matmul(a, w, out_dtype, tm, tn, name):
    m, k = a.shape
    n = w.shape[1]
    return pl.pallas_call(
        _mm_kernel, grid=(n // tn, m // tm),
        in_specs=[pl.BlockSpec((tm, k), lambda j, i: (i, 0)),
                  pl.BlockSpec((k, tn), lambda j, i: (0, j))],
        out_specs=pl.BlockSpec((tm, tn), lambda j, i: (i, j)),
        out_shape=jax.ShapeDtypeStruct((m, n), out_dtype),
        compiler_params=_cparams(("arbitrary", "arbitrary")), name=name,
    )(a, w)


def _mm_resid_kernel(a_ref, w_ref, x_ref, g_ref, o_ref):
    o_ref[...] = x_ref[...] + g_ref[...] * _dot(a_ref[...], w_ref[...])


def _matmul_resid(a, w, x, gate, seq, tm, tn, name):
    m, k = a.shape
    n = w.shape[1]
    tpb = seq // tm
    return pl.pallas_call(
        _mm_resid_kernel, grid=(n // tn, m // tm),
        in_specs=[pl.BlockSpec((tm, k), lambda j, i: (i, 0)),
                  pl.BlockSpec((k, tn), lambda j, i: (0, j)),
                  pl.BlockSpec((tm, tn), lambda j, i: (i, j)),
                  pl.BlockSpec((None, 1, tn), lambda j, i: (i // tpb, 0, j))],
        out_specs=pl.BlockSpec((tm, tn), lambda j, i: (i, j)),
        out_shape=jax.ShapeDtypeStruct((m, n), F32),
        compiler_params=_cparams(("arbitrary", "arbitrary")), name=name,
    )(a, w, x, gate)


def _prep_kernel(x_ref, par_ref, col_ref, rowf_ref, rowg_ref, carry_ref, *, ts):
    @pl.when(pl.program_id(1) == 0)
    def _():
        carry_ref[...] = jnp.zeros_like(carry_ref)

    x = x_ref[...]
    lane = lax.broadcasted_iota(jnp.int32, x.shape, 1)
    is_g = (lane >= L_G) & (lane < L_G + NH)
    is_f = (lane >= L_F) & (lane < L_F + NH)
    a_rate = jnp.exp(par_ref[0:1, :])
    z = x + par_ref[1:2, :]
    t = jnp.log1p(jnp.exp(-jnp.abs(z)))
    sig = _sigmoid(x)
    g = -a_rate * (jnp.maximum(z, 0.0) + t)
    logf = jnp.minimum(z, 0.0) - t
    vals = jnp.where(is_g, g, jnp.where(is_f, logf, 0.0))
    r = lax.broadcasted_iota(jnp.int32, (ts, ts), 0)
    c = lax.broadcasted_iota(jnp.int32, (ts, ts), 1)
    tri_full = jnp.where(c <= r, 1.0, 0.0).astype(BF16)
    tri_blk = jnp.where((c <= r) & ((r // CHUNK) == (c // CHUNK)), 1.0, 0.0).astype(BF16)
    v0, v1, v2 = _split3(vals)
    cs_full = _dot(tri_full, v0) + _dot(tri_full, v1) + _dot(tri_full, v2)
    cs_blk = _dot(tri_blk, v0) + _dot(tri_blk, v1) + _dot(tri_blk, v2)
    cum = cs_full + carry_ref[...]
    carry_ref[...] = cum[ts - 1:ts, :]
    out = jnp.where(lane < NH, sig, jnp.where(is_g, cs_blk, jnp.where(is_f, cum, 0.0)))
    col_ref[...] = out
    out_t = out.T
    rowf_ref[...] = out_t[L_F:L_F + NH, :]
    for ci in range(ts // CHUNK):
        rowg_ref[ci] = out_t[0:32, ci * CHUNK:(ci + 1) * CHUNK]


def _prep(small, par, bsz, seq):
    m = small.shape[0]
    ts = 256
    nt = seq // ts
    return pl.pallas_call(
        functools.partial(_prep_kernel, ts=ts),
        grid=(bsz, nt),
        in_specs=[pl.BlockSpec((ts, LANES), lambda b, j: (b * nt + j, 0)),
                  pl.BlockSpec((8, LANES), lambda b, j: (0, 0))],
        out_specs=[pl.BlockSpec((ts, LANES), lambda b, j: (b * nt + j, 0)),
                   pl.BlockSpec((NH, ts), lambda b, j: (0, b * nt + j)),
                   pl.BlockSpec((ts // CHUNK, 32, CHUNK), lambda b, j: (b * nt + j, 0, 0))],
        out_shape=[jax.ShapeDtypeStruct((m, LANES), F32),
                   jax.ShapeDtypeStruct((NH, m), F32),
                   jax.ShapeDtypeStruct((m // CHUNK, 32, CHUNK), F32)],
        scratch_shapes=[pltpu.VMEM((1, LANES), F32)],
        compiler_params=_cparams(("arbitrary", "arbitrary")), name="prep",
    )(small, par)


HALO = 16


def _bmm(a, b):
    return lax.dot_general(a.astype(BF16), b.astype(BF16), (((2,), (1,)), ((0,), (0,))),
                           preferred_element_type=F32)


def _bmm_nt(a, b):
    return lax.dot_general(a.astype(BF16), b.astype(BF16), (((2,), (2,)), ((0,), (0,))),
                           preferred_element_type=F32)


def _delta_kernel(qkv_ref, halo_ref, z_ref, cw_ref, col_ref, rowg_ref, gain_ref, o_ref, s_ref, *, nc):
    j = pl.program_id(1)

    @pl.when(j == 0)
    def _():
        s_ref[...] = jnp.zeros_like(s_ref)

    ts = nc * CHUNK
    n = NH * nc
    xb = qkv_ref[...]
    halo = halo_ref[...]
    halo = jnp.where(j == 0, jnp.zeros_like(halo), halo)
    xx = jnp.concatenate([halo, xb], axis=0)
    rr = lax.broadcasted_iota(jnp.int32, (3 * ts, HALO + ts), 0)
    cc = lax.broadcasted_iota(jnp.int32, (3 * ts, HALO + ts), 1)
    sel = jnp.where(cc == (rr % ts) + HALO - 1 - rr // ts, 1.0, 0.0).astype(BF16)
    shifted = _dot(sel, xx)
    cw = cw_ref[...]
    acc = xb.astype(F32) * cw[3:4, :]
    for dlt in (1, 2, 3):
        acc = acc + shifted[(dlt - 1) * ts:dlt * ts, :] * cw[3 - dlt:4 - dlt, :]
    act = acc * _sigmoid(acc)

    cv = col_ref[...]
    rg = rowg_ref[...]
    qs, ks, vs, betas, gcs, grs = [], [], [], [], [], []
    for h in range(NH):
        qh = act[:, h * HD:(h + 1) * HD]
        kh = act[:, BW + h * HD:BW + (h + 1) * HD]
        vh = act[:, 2 * BW + h * HD:2 * BW + (h + 1) * HD]
        qh = qh * (lax.rsqrt(jnp.sum(qh * qh, axis=-1, keepdims=True) + EPS) * (HD ** -0.5))
        kh = kh * lax.rsqrt(jnp.sum(kh * kh, axis=-1, keepdims=True) + EPS)
        qs.append(qh.reshape(nc, CHUNK, HD))
        ks.append(kh.reshape(nc, CHUNK, HD))
        vs.append(vh.reshape(nc, CHUNK, HD))
        betas.append(cv[:, L_BETA + h:L_BETA + h + 1].reshape(nc, CHUNK, 1))
        gcs.append(cv[:, L_G + h:L_G + h + 1].reshape(nc, CHUNK, 1))
        grs.append(rg[:, L_G + h:L_G + h + 1, :])
    q = jnp.concatenate(qs, axis=0)
    k = jnp.concatenate(ks, axis=0)
    v = jnp.concatenate(vs, axis=0)
    beta = jnp.concatenate(betas, axis=0)
    gc = jnp.concatenate(gcs, axis=0)
    gr = jnp.concatenate(grs, axis=0)

    ri = lax.broadcasted_iota(jnp.int32, (CHUNK, CHUNK), 0)
    ci = lax.broadcasted_iota(jnp.int32, (CHUNK, CHUNK), 1)
    causal = (ci <= ri)[None]
    strict = (ci < ri)[None]
    decay = jnp.exp(jnp.where(causal, gc - gr, -jnp.inf))
    kb = k * beta
    lower = jnp.where(strict, _bmm_nt(kb, k) * decay, 0.0)
    qk = _bmm_nt(q, k) * decay

    eye = jnp.where(ci == ri, 1.0, 0.0)[None]
    t_inv = eye - lower
    pw = _bmm(lower, lower)
    for it in range(5):
        t_inv = t_inv + _bmm(t_inv, pw)
        if it < 4:
            pw = _bmm(pw, pw)

    eg = jnp.exp(gc)
    rhs = jnp.concatenate([v * beta, kb * eg], axis=-1)
    sol = _bmm(t_inv, rhs)
    u = sol[:, :, :HD]
    w = sol[:, :, HD:]
    q_dec = q * eg
    g_end = gc[:, CHUNK - 1:CHUNK, :]
    k_dec = k * jnp.exp(g_end - gc)
    g_last = jnp.exp(g_end)

    def pick(t, c):
        return t.reshape((NH, nc) + t.shape[1:])[:, c]

    state = s_ref[...]
    outs = []
    for c in range(nc):
        v_new = pick(u, c) - _bmm(pick(w, c), state)
        o_c = _bmm(pick(q_dec, c), state) + _bmm(pick(qk, c), v_new)
        kd_t = jnp.swapaxes(pick(k_dec, c), 1, 2)
        state = state * pick(g_last, c) + _bmm(kd_t, v_new)
        outs.append(o_c)
    s_ref[...] = state

    gain = gain_ref[...]
    zf = z_ref[...].astype(F32)
    cols = []
    for h in range(NH):
        oh = jnp.concatenate([outs[c][h] for c in range(nc)], axis=0)
        oh = oh * lax.rsqrt(jnp.mean(oh * oh, axis=-1, keepdims=True) + EPS) * gain
        zh = zf[:, h * HD:(h + 1) * HD]
        cols.append(oh * (zh * _sigmoid(zh)))
    o_ref[...] = jnp.concatenate(cols, axis=1).astype(BF16)


def _delta(proj, conv_w8, col, rowg, gain, bsz, seq):
    m = proj.shape[0]
    nc = 2
    ts = nc * CHUNK
    nt = seq // ts
    qkv_blk = P_DN // (3 * BW)
    return pl.pallas_call(
        functools.partial(_delta_kernel, nc=nc),
        grid=(bsz, nt),
        in_specs=[pl.BlockSpec((ts, 3 * BW), lambda b, j: (b * nt + j, qkv_blk)),
                  pl.BlockSpec((HALO, 3 * BW), lambda b, j: (jnp.maximum((b * nt + j) * (ts // HALO) - 1, 0), qkv_blk)),
                  pl.BlockSpec((ts, BW), lambda b, j: (b * nt + j, P_DNZ // BW)),
                  pl.BlockSpec((8, 3 * BW), lambda b, j: (0, 0)),
                  pl.BlockSpec((ts, LANES), lambda b, j: (b * nt + j, 0)),
                  pl.BlockSpec((nc, 32, CHUNK), lambda b, j: (b * nt + j, 0, 0)),
                  pl.BlockSpec((1, HD), lambda b, j: (0, 0))],
        out_specs=pl.BlockSpec((ts, BW), lambda b, j: (b * nt + j, 0)),
        out_shape=jax.ShapeDtypeStruct((m, BW), BF16),
        scratch_shapes=[pltpu.VMEM((NH, HD, HD), F32)],
        compiler_params=_cparams(("arbitrary", "arbitrary")), name="delta",
    )(proj, proj, proj, conv_w8, col, rowg, gain.reshape(1, HD))


FOX_T = 512
FOX_HPS = 4


def _fox_kernel(q_ref, k_ref, v_ref, crow_ref, o_ref, vx_ref, m_ref, acc_ref):
    t = FOX_T
    hp = pl.program_id(1)
    qi = pl.program_id(2)

    @pl.when(qi == 0)
    def _():
        lane = lax.broadcasted_iota(jnp.int32, (v_ref.shape[0], HD), 1)
        ones_col = jnp.where(lane == 0, 1.0, 0.0).astype(BF16)
        for e in range(FOX_HPS):
            vx_ref[e, :, :HD] = v_ref[:, e * HD:(e + 1) * HD]
            vx_ref[e, :, HD:] = ones_col

    m_ref[...] = jnp.full_like(m_ref, -jnp.inf)
    acc_ref[...] = jnp.zeros_like(acc_ref)

    def block(ki, masked):
        off = pl.multiple_of(ki * t, t)
        for e in range(FOX_HPS):
            s = lax.dot_general(q_ref[:, e * HD:(e + 1) * HD], k_ref[pl.ds(off, t), e * HD:(e + 1) * HD],
                                (((1,), (1,)), ((), ())), preferred_element_type=F32)
            s = s - crow_ref[pl.ds(FOX_HPS * hp + e, 1), pl.ds(off, t)]
            if masked:
                row = lax.broadcasted_iota(jnp.int32, (t, t), 0)
                col = lax.broadcasted_iota(jnp.int32, (t, t), 1)
                s = jnp.where(col <= row, s, -jnp.inf)
            m_prev = m_ref[e]
            m_new = jnp.maximum(m_prev, jnp.max(s, axis=1, keepdims=True))
            alpha = jnp.exp(m_prev - m_new)
            p = jnp.exp((s - jnp.tile(m_new, (1, t // LANES))).astype(BF16))
            acc_ref[e] = jnp.tile(alpha, (1, 2)) * acc_ref[e] + _dot(p, vx_ref[e, pl.ds(off, t), :])
            m_ref[e] = m_new

    def body(ki, carry):
        block(ki, False)
        return carry

    lax.fori_loop(0, qi, body, 0)
    block(qi, True)
    for e in range(FOX_HPS):
        o_ref[:, e * HD:(e + 1) * HD] = (acc_ref[e, :, :HD] / acc_ref[e, :, HD:HD + 1]).astype(BF16)


def _fox(proj, rowf, bsz, seq):
    m = proj.shape[0]
    t = FOX_T
    nq = seq // t
    w = FOX_HPS * HD
    return pl.pallas_call(
        _fox_kernel,
        grid=(bsz, NH // FOX_HPS, nq),
        in_specs=[pl.BlockSpec((t, w), lambda b, h, qi: (b * nq + qi, P_FQ // w + h)),
                  pl.BlockSpec((seq, w), lambda b, h, qi: (b, P_FK // w + h)),
                  pl.BlockSpec((seq, w), lambda b, h, qi: (b, P_FV // w + h)),
                  pl.BlockSpec((NH, seq), lambda b, h, qi: (0, b))],
        out_specs=pl.BlockSpec((t, w), lambda b, h, qi: (b * nq + qi, h)),
        out_shape=jax.ShapeDtypeStruct((m, BW), BF16),
        scratch_shapes=[pltpu.VMEM((FOX_HPS, seq, 2 * HD), BF16), pltpu.VMEM((FOX_HPS, t, HD), F32),
                        pltpu.VMEM((FOX_HPS, t, 2 * HD), F32)],
        compiler_params=_cparams(("arbitrary", "arbitrary", "arbitrary")), name="fox",
    )(proj, proj, proj, rowf)


def _swap_halves(x):
    w = x.shape[-1]
    lane = lax.broadcasted_iota(jnp.int32, x.shape, x.ndim - 1)
    return jnp.where((lane % SWA_D) < SWA_D // 2, pltpu.roll(x, w - SWA_D // 2, axis=x.ndim - 1),
                     pltpu.roll(x, SWA_D // 2, axis=x.ndim - 1))


def _swa_kernel(sink_ref, q_ref, kc_ref, kp_ref, vc_ref, vp_ref, cc_ref, sc_ref, cp_ref, sp_ref, o_ref, *, nblk):
    i = pl.program_id(0)
    first = (i % nblk) == 0
    cos_c, sin_c = cc_ref[...], sc_ref[...]
    q = q_ref[...].astype(F32)
    q = q * jnp.tile(cos_c, (1, SWA_QH // 2)) + _swap_halves(q) * jnp.tile(sin_c, (1, SWA_QH // 2))
    kc = kc_ref[...].astype(F32)
    kc = kc * cos_c + _swap_halves(kc) * sin_c
    kp = kp_ref[...].astype(F32)
    kp = kp * cp_ref[...] + _swap_halves(kp) * sp_ref[...]
    kk = jnp.concatenate([kp, kc], axis=0).astype(BF16)
    vv = jnp.concatenate([vp_ref[...], vc_ref[...]], axis=0)
    r = lax.broadcasted_iota(jnp.int32, (WIN, 2 * WIN), 0)
    c = lax.broadcasted_iota(jnp.int32, (WIN, 2 * WIN), 1)
    mask = (c > r) & (c <= r + WIN) & ((c >= WIN) | jnp.logical_not(first))
    qb = q.astype(BF16)
    outs = []
    for hq in range(SWA_QH):
        g = hq // SWA_G
        qh = qb[:, hq * SWA_D:(hq + 1) * SWA_D]
        kh = kk[:, g * SWA_D:(g + 1) * SWA_D]
        vh = vv[:, g * SWA_D:(g + 1) * SWA_D]
        s = lax.dot_general(qh, kh, (((1,), (1,)), ((), ())), preferred_element_type=F32) * (SWA_D ** -0.5)
        s = jnp.where(mask, s, -jnp.inf)
        sink = sink_ref[hq]
        mx = jnp.maximum(jnp.max(s, axis=1, keepdims=True), sink)
        p = jnp.exp(s - mx)
        den = jnp.sum(p, axis=1, keepdims=True) + jnp.exp(sink - mx)
        outs.append(_dot(p.astype(BF16), vh) / den)
    o_ref[...] = jnp.concatenate(outs, axis=1).astype(BF16)


def _swa(proj, sinks, cos4, sin4, bsz, seq):
    m = proj.shape[0]
    nblk = seq // WIN
    prev = lambda i: jnp.maximum(i - 1, 0)
    kcol, vcol = P_SK // LANES, P_SV // LANES
    return pl.pallas_call(
        functools.partial(_swa_kernel, nblk=nblk),
        grid=(m // WIN,),
        in_specs=[pl.BlockSpec(memory_space=pltpu.SMEM),
                  pl.BlockSpec((WIN, BW), lambda i: (i, P_SQ // BW)),
                  pl.BlockSpec((WIN, LANES), lambda i: (i, kcol)),
                  pl.BlockSpec((WIN, LANES), lambda i: (prev(i), kcol)),
                  pl.BlockSpec((WIN, LANES), lambda i: (i, vcol)),
                  pl.BlockSpec((WIN, LANES), lambda i: (prev(i), vcol)),
                  pl.BlockSpec((WIN, LANES), lambda i: (i, 0)),
                  pl.BlockSpec((WIN, LANES), lambda i: (i, 0)),
                  pl.BlockSpec((WIN, LANES), lambda i: (prev(i), 0)),
                  pl.BlockSpec((WIN, LANES), lambda i: (prev(i), 0))],
        out_specs=pl.BlockSpec((WIN, BW), lambda i: (i, 0)),
        out_shape=jax.ShapeDtypeStruct((m, BW), BF16),
        compiler_params=_cparams(("arbitrary",)), name="swa",
    )(sinks, proj, proj, proj, proj, proj, cos4, sin4, cos4, sin4)


def _merge_kernel(oa_ref, ob_ref, oc_ref, ga_ref, gb_ref, gc_ref, w_ref, o_ref):
    acc = _sigmoid(ga_ref[...].astype(F32)) * _dot(oa_ref[...], w_ref[0])
    acc = acc + _sigmoid(gb_ref[...].astype(F32)) * _dot(ob_ref[...], w_ref[1])
    acc = acc + _sigmoid(gc_ref[...].astype(F32)) * _dot(oc_ref[...], w_ref[2])
    o_ref[...] = acc.astype(BF16)


def _merge(o_a, o_b, o_c, proj, w_branch):
    m = o_a.shape[0]
    tm, tn = 512, 512
    gblk = lambda b: (lambda j, i: (i, (P_GATE + b * D) // tn + j))
    oblk = pl.BlockSpec((tm, BW), lambda j, i: (i, 0))
    return pl.pallas_call(
        _merge_kernel, grid=(D // tn, m // tm),
        in_specs=[oblk, oblk, oblk,
                  pl.BlockSpec((tm, tn), gblk(0)), pl.BlockSpec((tm, tn), gblk(1)), pl.BlockSpec((tm, tn), gblk(2)),
                  pl.BlockSpec((3, BW, tn), lambda j, i: (0, 0, j))],
        out_specs=pl.BlockSpec((tm, tn), lambda j, i: (i, j)),
        out_shape=jax.ShapeDtypeStruct((m, D), BF16),
        compiler_params=_cparams(("arbitrary", "arbitrary")), name="merge",
    )(o_a, o_b, o_c, proj, proj, proj, w_branch)


def _swiglu_acc(h, wg_ref, wu_ref, wd_ref, acc_ref):
    a = _dot(h, wg_ref[...])
    b = _dot(h, wu_ref[...])
    t = (a * _sigmoid(a) * b).astype(BF16)
    acc_ref[...] += _dot(t, wd_ref[...])


def _ffn_kernel(h_ref, wg_ref, wu_ref, wd_ref, x_ref, g_ref, o_ref, acc_ref, *, nf):
    f = pl.program_id(1)

    @pl.when(f == 0)
    def _():
        acc_ref[...] = jnp.zeros_like(acc_ref)

    _swiglu_acc(h_ref[...], wg_ref, wu_ref, wd_ref, acc_ref)

    @pl.when(f == nf - 1)
    def _():
        o_ref[...] = x_ref[...] + g_ref[...] * acc_ref[...]


def _ffn_dense(h, wg, wu, wd, x, gate, seq):
    m, d = h.shape
    ff = wg.shape[1]
    tm, tf = 512, 512
    nf = ff // tf
    tpb = seq // tm
    return pl.pallas_call(
        functools.partial(_ffn_kernel, nf=nf), grid=(m // tm, nf),
        in_specs=[pl.BlockSpec((tm, d), lambda i, f: (i, 0)),
                  pl.BlockSpec((d, tf), lambda i, f: (0, f)),
                  pl.BlockSpec((d, tf), lambda i, f: (0, f)),
                  pl.BlockSpec((tf, d), lambda i, f: (f, 0)),
                  pl.BlockSpec((tm, d), lambda i, f: (i, 0)),
                  pl.BlockSpec((None, 1, d), lambda i, f: (i // tpb, 0, 0))],
        out_specs=pl.BlockSpec((tm, d), lambda i, f: (i, 0)),
        out_shape=jax.ShapeDtypeStruct((m, d), F32),
        scratch_shapes=[pltpu.VMEM((tm, d), F32)],
        compiler_params=_cparams(("arbitrary", "arbitrary")), name="ffn_dense",
    )(h, wg, wu, wd, x, gate)


def _row_copy(src_hbm, idx_ref, pos, dst, r, sem):
    return pltpu.make_async_copy(src_hbm.at[pl.ds(idx_ref[pos], 1), :], dst.at[pl.ds(r, 1), :], sem)


def _start_row_gather(src_hbm, idx_ref, base, n, dst, sem):
    def body(r, carry):
        _row_copy(src_hbm, idx_ref, base + r, dst, r, sem).start()
        return carry

    lax.fori_loop(0, n, body, 0, unroll=8)


def _wait_row_gather(src_hbm, n, dst, sem):
    pltpu.make_async_copy(src_hbm.at[pl.ds(0, n), :], dst, sem).wait()


def _ffn_group_kernel(te_ref, tv_ref, src_ref, h_hbm, wg_ref, wu_ref, wd_ref, o_ref,
                      buf, hbf_ref, acc_ref, sem, *, nf, tm, n_tiles):
    i = pl.program_id(0)
    f = pl.program_id(1)
    slot = i % 2
    live = tv_ref[i] > 0
    per = -(-tm // nf)
    last = tm - per * (nf - 1)

    @pl.when(f == 0)
    def _():
        @pl.when(i == 0)
        def _():
            _start_row_gather(h_hbm, src_ref, 0, tm, buf.at[0], sem.at[0])

        _wait_row_gather(h_hbm, tm, buf.at[slot], sem.at[slot])
        hbf_ref[...] = buf[slot].astype(BF16)
        acc_ref[...] = jnp.zeros_like(acc_ref)

    def prefetch(first, count):
        for j in range(count):
            r = first + j
            _row_copy(h_hbm, src_ref, (i + 1) * tm + r, buf.at[1 - slot], r, sem.at[1 - slot]).start()

    def step(count, compute):
        c0 = count // 3
        c1 = (count - c0) // 2
        first = f * per
        if not compute:
            prefetch(first, count)
            return
        h = hbf_ref[...]
        a = _dot(h, wg_ref[...])
        prefetch(first, c0)
        b = _dot(h, wu_ref[...])
        prefetch(first + c0, c1)
        t = (a * _sigmoid(a) * b).astype(BF16)
        acc_ref[...] += _dot(t, wd_ref[...])
        prefetch(first + c0 + c1, count - c0 - c1)

    for is_last, count in ((False, per), (True, last)):
        for compute in (True, False):
            cond = (f == nf - 1) if is_last else (f < nf - 1)
            cond = cond & (live if compute else jnp.logical_not(live))
            pl.when(cond)(functools.partial(step, count, compute))

    @pl.when(f == nf - 1)
    def _():
        o_ref[...] = acc_ref[...]

        @pl.when(i == n_tiles - 1)
        def _():
            _wait_row_gather(h_hbm, tm, buf.at[1 - slot], sem.at[1 - slot])


def _ffn_grouped(h, wg, wu, wd, tile_expert, tile_valid, src_rows, tm):
    rows = src_rows.shape[0] - tm
    d, ff = wg.shape[1], wg.shape[2]
    tf = 512
    nf = ff // tf
    n_tiles = rows // tm

    def fsel(i, f, tv):
        return jnp.where(tv[i] > 0, f, nf - 1)

    grid_spec = pltpu.PrefetchScalarGridSpec(
        num_scalar_prefetch=3, grid=(n_tiles, nf),
        in_specs=[pl.BlockSpec(memory_space=pl.ANY),
                  pl.BlockSpec((None, d, tf), lambda i, f, te, tv, sr: (te[i], 0, fsel(i, f, tv))),
                  pl.BlockSpec((None, d, tf), lambda i, f, te, tv, sr: (te[i], 0, fsel(i, f, tv))),
                  pl.BlockSpec((None, tf, d), lambda i, f, te, tv, sr: (te[i], fsel(i, f, tv), 0))],
        out_specs=pl.BlockSpec((tm, d), lambda i, f, te, tv, sr: (i, 0)),
        scratch_shapes=[pltpu.VMEM((2, tm, d), F32),
                        pltpu.VMEM((tm, d), BF16),
                        pltpu.VMEM((tm, d), F32),
                        pltpu.SemaphoreType.DMA((2,))])
    return pl.pallas_call(
        functools.partial(_ffn_group_kernel, nf=nf, tm=tm, n_tiles=n_tiles), grid_spec=grid_spec,
        out_shape=jax.ShapeDtypeStruct((rows, d), F32),
        compiler_params=_cparams(("arbitrary", "arbitrary")), name="ffn_grouped",
    )(tile_expert, tile_valid, src_rows, h, wg, wu, wd)


COMBINE_TM = 256


def _combine_kernel(pos_ref, x_ref, y_hbm, r_ref, g_ref, gain_ref, o_ref, buf, sem, *, tm, nt, m, final):
    i = pl.program_id(0)
    slot = i % 2

    def start(tile, s):
        for k in range(2):
            _start_row_gather(y_hbm, pos_ref, k * m + tile * tm, tm, buf.at[s, k], sem.at[s])

    @pl.when(i == 0)
    def _():
        start(0, 0)

    @pl.when(i + 1 < nt)
    def _():
        start(jnp.minimum(i + 1, nt - 1), 1 - slot)

    for k in range(2):
        _wait_row_gather(y_hbm, tm, buf.at[slot, k], sem.at[slot])
    w0 = r_ref[:, 2:3]
    w1 = r_ref[:, 3:4]
    xn = x_ref[...] + g_ref[...] * (w0 * buf[slot, 0] + w1 * buf[slot, 1])
    if final:
        xn = xn * lax.rsqrt(jnp.mean(xn * xn, axis=-1, keepdims=True) + EPS) * gain_ref[...]
    o_ref[...] = xn


def _combine(x, y, pos_rows, route, gate, seq, final_gain=None):
    m, d = x.shape
    tm = COMBINE_TM
    nt = m // tm
    tpb = seq // tm
    final = final_gain is not None
    gain = (final_gain if final else jnp.ones((d,), F32)).reshape(1, d)
    grid_spec = pltpu.PrefetchScalarGridSpec(
        num_scalar_prefetch=1, grid=(nt,),
        in_specs=[pl.BlockSpec((tm, d), lambda i, p: (i, 0)),
                  pl.BlockSpec(memory_space=pl.ANY),
                  pl.BlockSpec((tm, LANES), lambda i, p: (i, 0)),
                  pl.BlockSpec((None, 1, d), lambda i, p: (i // tpb, 0, 0)),
                  pl.BlockSpec((1, d), lambda i, p: (0, 0))],
        out_specs=pl.BlockSpec((tm, d), lambda i, p: (i, 0)),
        scratch_shapes=[pltpu.VMEM((2, 2, tm, d), F32), pltpu.SemaphoreType.DMA((2,))])
    return pl.pallas_call(
        functools.partial(_combine_kernel, tm=tm, nt=nt, m=m, final=final), grid_spec=grid_spec,
        out_shape=jax.ShapeDtypeStruct((m, d), F32),
        compiler_params=_cparams(("arbitrary",)), name="combine",
    )(pos_rows, x, y, route, gate, gain)


def _final_norm_kernel(x_ref, gain_ref, o_ref):
    x = x_ref[...]
    o_ref[...] = x * lax.rsqrt(jnp.mean(x * x, axis=-1, keepdims=True) + EPS) * gain_ref[...]


def _final_norm(x, gain):
    m, d = x.shape
    tm = 512
    blk = pl.BlockSpec((tm, d), lambda i: (i, 0))
    return pl.pallas_call(
        _final_norm_kernel, grid=(m // tm,),
        in_specs=[blk, pl.BlockSpec((1, d), lambda i: (0, 0))],
        out_specs=blk, out_shape=jax.ShapeDtypeStruct((m, d), F32),
        compiler_params=_cparams(("arbitrary",)), name="final_norm",
    )(x, gain.reshape(1, d))


MOE_TM = 512


def _moe(x, h, route, wg, wu, wd, gate, seq, final_gain=None):
    m, d = x.shape
    tm = MOE_TM
    rows = 2 * m + NE * tm
    n_tiles = rows // tm
    e_flat = route[:, 0:2].astype(jnp.int32).reshape(-1)
    onehot = (e_flat[:, None] == jnp.arange(NE, dtype=jnp.int32)[None, :]).astype(jnp.int32)
    incl = jnp.cumsum(onehot, axis=0)
    rank = jnp.sum((incl - onehot) * onehot, axis=1)
    counts = incl[-1]
    padded = ((counts + tm - 1) // tm) * tm
    ends = jnp.cumsum(padded)
    starts = ends - padded
    pos = jnp.sum(onehot * starts[None, :], axis=1) + rank
    src_token = jnp.zeros((rows + tm,), jnp.int32).at[pos].set(jnp.arange(2 * m, dtype=jnp.int32) // 2)
    tile_start = jnp.arange(n_tiles, dtype=jnp.int32) * tm
    tile_expert = jnp.minimum(jnp.sum((tile_start[:, None] >= ends[None, :]).astype(jnp.int32), axis=1), NE - 1)
    tile_valid = (tile_start < ends[-1]).astype(jnp.int32)

    ys = _ffn_grouped(h, wg, wu, wd, tile_expert, tile_valid, src_token, tm)
    pos_rows = pos.reshape(m, 2).T.reshape(-1)
    return _combine(x, ys, pos_rows, route, gate, seq, final_gain)


def _regroup_w_in(w):
    dn = w[:, 0:4096]
    small = jnp.concatenate([w[:, 4096:4112], w[:, 7184:7192]], axis=1)
    fox = jnp.concatenate([w[:, 4112:5136] * (HD ** -0.5), w[:, 5136:7184]], axis=1)
    swa = w[:, 7192:8472]
    gates = w[:, 8472:14616]
    main = jnp.concatenate([gates, dn, fox, swa], axis=1).astype(BF16)
    small = jnp.pad(small, ((0, 0), (0, LANES - small.shape[1]))).astype(BF16)
    return main, small


def kernel(x, c, positions, w_ada, b_ada, norm_mix, w_in, conv_w, dn_a_log, dn_dt_bias, dn_norm,
           fox_b_forget, swa_sinks, w_branch, w_out, norm_ffn, ffn_w_gate, ffn_w_up, ffn_w_down,
           moe_router, moe_w_gate, moe_w_up, moe_w_down, final_norm):
    bsz, seq, d = x.shape
    depth = w_ada.shape[0]
    m = bsz * seq
    xf = x.reshape(m, d)

    inv_freq = 10000.0 ** (-jnp.arange(0, SWA_D, 2, dtype=F32) / SWA_D)
    ang = positions.astype(F32).reshape(m, 1) * inv_freq[None, :]
    cos, sin = jnp.cos(ang), jnp.sin(ang)
    cos4 = jnp.tile(cos, (1, 4))
    sin4 = jnp.tile(jnp.concatenate([-sin, sin], axis=1), (1, 2))

    c_pad = jnp.pad(c, ((0, 8 - bsz), (0, 0)))
    mod = _ada(c_pad, w_ada, b_ada)[:, :bsz].reshape(depth, bsz, 6, 1, d)

    for layer in range(depth):
        sh1, sc1, g1, sh2, sc2, g2 = (mod[layer, :, t] for t in range(6))
        w_main, w_small = _regroup_w_in(w_in[layer])
        h = _norm_modulate(xf, norm_mix[layer], sc1, sh1, seq)
        proj = _matmul(h, w_main, BF16, 512, 2432, "in_proj")
        small = _matmul(h, w_small, F32, 1024, LANES, "in_proj_small")
        par = jnp.zeros((8, LANES), F32)
        par = par.at[0, L_G:L_G + NH].set(dn_a_log[layer])
        par = par.at[1, L_G:L_G + NH].set(dn_dt_bias[layer])
        par = par.at[1, L_F:L_F + NH].set(fox_b_forget[layer])
        col, rowf, rowg = _prep(small, par, bsz, seq)
        conv_w8 = jnp.pad(conv_w[layer], ((0, 4), (0, 0)))
        o_a = _delta(proj, conv_w8, col, rowg, dn_norm[layer], bsz, seq)
        o_b = _fox(proj, rowf, bsz, seq)
        o_c = _swa(proj, swa_sinks[layer], cos4, sin4, bsz, seq)
        merged = _merge(o_a, o_b, o_c, proj, w_branch[layer].astype(BF16))
        xf = _matmul_resid(merged, w_out[layer].astype(BF16), xf, g1, seq, 512, 1024, "out_proj")
        if layer % 2 == 0:
            i = layer // 2
            h2 = _norm_modulate(xf, norm_ffn[layer], sc2, sh2, seq)
            xf = _ffn_dense(h2, ffn_w_gate[i].astype(BF16), ffn_w_up[i].astype(BF16),
                            ffn_w_down[i].astype(BF16), xf, g2, seq)
        else:
            i = layer // 2
            h2, route = _norm_modulate(xf, norm_ffn[layer], sc2, sh2, seq, w_router=moe_router[i])
            last = layer == depth - 1
            xf = _moe(xf, h2, route, moe_w_gate[i].astype(BF16), moe_w_up[i].astype(BF16),
                      moe_w_down[i].astype(BF16), g2, seq, final_gain=final_norm if last else None)
            if last:
                return xf.reshape(bsz, seq, d)
    return _final_norm(xf, final_norm).reshape(bsz, seq, d)
```

```python
import functools

import jax
import jax.numpy as jnp
from jax import lax
from jax.experimental import pallas as pl
from jax.experimental.pallas import tpu as pltpu

F32 = jnp.float32
BF16 = jnp.bfloat16
EPS = 1e-6

D = 2048
HD = 128
NH = 8
BW = 1024
CHUNK = 64
SWA_D = 64
SWA_QH = 16
SWA_KVH = 2
SWA_G = 8
WIN = 128
D_FF = 7168
NE = 8
LANES = 128

P_GATE = 0
P_DN = 6144
P_DNZ = 9216
P_FQ, P_FK, P_FV = 10240, 11264, 12288
P_SQ, P_SK, P_SV = 13312, 14336, 14464
NP = 14592
L_BETA, L_G, L_F = 0, 8, 16

VMEM_LIMIT = 56 * 1024 * 1024


def _cparams(sem):
    return pltpu.CompilerParams(dimension_semantics=sem, vmem_limit_bytes=VMEM_LIMIT)


def _sigmoid(x):
    return 1.0 / (1.0 + jnp.exp(-x))


def _split3(x):
    hi = x.astype(BF16)
    r = x - hi.astype(F32)
    mid = r.astype(BF16)
    lo = (r - mid.astype(F32)).astype(BF16)
    return hi, mid, lo


def _dot(a, b):
    return jnp.dot(a, b, preferred_element_type=F32)


def _dot_hi(a, b):
    a0, a1, a2 = _split3(a)
    b0, b1, b2 = _split3(b)
    return (_dot(a0, b0) + (_dot(a0, b1) + _dot(a1, b0))
            + (_dot(a0, b2) + _dot(a1, b1) + _dot(a2, b0)))


def _ada_kernel(c_ref, w_ref, b_ref, o_ref):
    c = c_ref[...]
    ca = c * _sigmoid(c)
    a0 = ca.astype(BF16)
    a1 = (ca - a0.astype(F32)).astype(BF16)
    w = w_ref[...]
    w0 = w.astype(BF16)
    w1 = (w - w0.astype(F32)).astype(BF16)
    o_ref[...] = _dot(a0, w0) + (_dot(a0, w1) + _dot(a1, w0)) + b_ref[...]


def _ada(c_pad, w_ada, b_ada):
    depth, d, n = w_ada.shape
    tn = 1536
    return pl.pallas_call(
        _ada_kernel,
        grid=(depth, n // tn),
        in_specs=[pl.BlockSpec((8, d), lambda l, j: (0, 0)),
                  pl.BlockSpec((None, d, tn), lambda l, j: (l, 0, j)),
                  pl.BlockSpec((None, 1, tn), lambda l, j: (l, 0, j))],
        out_specs=pl.BlockSpec((None, 8, tn), lambda l, j: (l, 0, j)),
        out_shape=jax.ShapeDtypeStruct((depth, 8, n), F32),
        compiler_params=_cparams(("arbitrary", "arbitrary")),
        name="ada",
    )(c_pad, w_ada, b_ada.reshape(depth, 1, n))


def _normmod(x_ref, gain_ref, sc_ref, sh_ref):
    x = x_ref[...]
    ms = jnp.mean(x * x, axis=-1, keepdims=True)
    y = x * lax.rsqrt(ms + EPS) * gain_ref[...]
    return y * (1.0 + sc_ref[...]) + sh_ref[...]


def _normmod_kernel(x_ref, gain_ref, sc_ref, sh_ref, h_ref):
    h_ref[...] = _normmod(x_ref, gain_ref, sc_ref, sh_ref).astype(BF16)


def _normmod_route_kernel(x_ref, gain_ref, sc_ref, sh_ref, wr_ref, h_ref, r_ref, cnt_ref, carry_ref):
    @pl.when(pl.program_id(0) == 0)
    def _():
        carry_ref[...] = jnp.zeros_like(carry_ref)

    h = _normmod(x_ref, gain_ref, sc_ref, sh_ref)
    h_ref[...] = h
    logits = _dot_hi(h, wr_ref[...])
    lane = lax.broadcasted_iota(jnp.int32, logits.shape, 1)
    neg = jnp.float32(-jnp.inf)
    l1 = jnp.where(lane < NE, logits, neg)
    m1 = jnp.max(l1, axis=-1, keepdims=True)
    i1 = jnp.min(jnp.where(l1 == m1, lane, LANES), axis=-1, keepdims=True)
    l2 = jnp.where(lane == i1, neg, l1)
    m2 = jnp.max(l2, axis=-1, keepdims=True)
    i2 = jnp.min(jnp.where(l2 == m2, lane, LANES), axis=-1, keepdims=True)
    e = jnp.exp(m2 - m1)
    w1 = 1.0 / (1.0 + e)
    w2 = e / (1.0 + e)
    tm = logits.shape[0]
    picked = jnp.where((lane == i1) | (lane == i2), 1.0, 0.0)
    rr = lax.broadcasted_iota(jnp.int32, (tm, tm), 0)
    cc = lax.broadcasted_iota(jnp.int32, (tm, tm), 1)
    before = _dot(jnp.where(cc < rr, 1.0, 0.0).astype(BF16), picked.astype(BF16)) + carry_ref[...]
    rank1 = jnp.sum(jnp.where(lane == i1, before, 0.0), axis=-1, keepdims=True)
    rank2 = jnp.sum(jnp.where(lane == i2, before, 0.0), axis=-1, keepdims=True)
    carry_ref[...] += jnp.sum(picked, axis=0, keepdims=True)
    cnt_ref[...] = jnp.broadcast_to(carry_ref[...], cnt_ref.shape)
    vals = (i1.astype(F32), i2.astype(F32), w1, w2, rank1, rank2)
    out = jnp.zeros_like(logits)
    for pos, v in enumerate(vals):
        out = jnp.where(lane == pos, v, out)
    r_ref[...] = out


def _norm_modulate(x, gain, sc, sh, seq, w_router=None):
    m, d = x.shape
    tm = 512
    tpb = seq // tm
    in_specs = [pl.BlockSpec((tm, d), lambda i: (i, 0)),
                pl.BlockSpec((1, d), lambda i: (0, 0)),
                pl.BlockSpec((None, 1, d), lambda i: (i // tpb, 0, 0)),
                pl.BlockSpec((None, 1, d), lambda i: (i // tpb, 0, 0))]
    args = [x, gain.reshape(1, d), sc, sh]
    if w_router is None:
        return pl.pallas_call(
            _normmod_kernel, grid=(m // tm,), in_specs=in_specs,
            out_specs=pl.BlockSpec((tm, d), lambda i: (i, 0)),
            out_shape=jax.ShapeDtypeStruct((m, d), BF16),
            compiler_params=_cparams(("arbitrary",)), name="normmod",
        )(*args)
    wr = jnp.pad(w_router, ((0, 0), (0, LANES - NE)))
    return pl.pallas_call(
        _normmod_route_kernel, grid=(m // tm,),
        in_specs=in_specs + [pl.BlockSpec((d, LANES), lambda i: (0, 0))],
        out_specs=[pl.BlockSpec((tm, d), lambda i: (i, 0)),
                   pl.BlockSpec((tm, LANES), lambda i: (i, 0)),
                   pl.BlockSpec((8, LANES), lambda i: (0, 0))],
        out_shape=[jax.ShapeDtypeStruct((m, d), F32),
                   jax.ShapeDtypeStruct((m, LANES), F32),
                   jax.ShapeDtypeStruct((8, LANES), F32)],
        scratch_shapes=[pltpu.VMEM((1, LANES), F32)],
        compiler_params=_cparams(("arbitrary",)), name="normmod_route",
    )(*args, wr)


def _mm_kernel(a_ref, w_ref, o_ref):
    o_ref[...] = _dot(a_ref[...], w_ref[...]).astype(o_ref.dtype)


def _matmul(a, w, out_dtype, tm, tn, name):
    m, k = a.shape
    n = w.shape[1]
    return pl.pallas_call(
        _mm_kernel, grid=(n // tn, m // tm),
        in_specs=[pl.BlockSpec((tm, k), lambda j, i: (i, 0)),
                  pl.BlockSpec((k, tn), lambda j, i: (0, j))],
        out_specs=pl.BlockSpec((tm, tn), lambda j, i: (i, j)),
        out_shape=jax.ShapeDtypeStruct((m, n), out_dtype),
        compiler_params=_cparams(("arbitrary", "arbitrary")), name=name,
    )(a, w)


def _mm_resid_kernel(a_ref, w_ref, x_ref, g_ref, o_ref):
    o_ref[...] = x_ref[...] + g_ref[...] * _dot(a_ref[...], w_ref[...])


def _matmul_resid(a, w, x, gate, seq, tm, tn, name):
    m, k = a.shape
    n = w.shape[1]
    tpb = seq // tm
    return pl.pallas_call(
        _mm_resid_kernel, grid=(n // tn, m // tm),
        in_specs=[pl.BlockSpec((tm, k), lambda j, i: (i, 0)),
                  pl.BlockSpec((k, tn), lambda j, i: (0, j)),
                  pl.BlockSpec((tm, tn), lambda j, i: (i, j)),
                  pl.BlockSpec((None, 1, tn), lambda j, i: (i // tpb, 0, j))],
        out_specs=pl.BlockSpec((tm, tn), lambda j, i: (i, j)),
        out_shape=jax.ShapeDtypeStruct((m, n), F32),
        compiler_params=_cparams(("arbitrary", "arbitrary")), name=name,
    )(a, w, x, gate)


def _prep_kernel(x_ref, par_ref, col_ref, rowf_ref, rowg_ref, carry_ref, *, ts):
    @pl.when(pl.program_id(1) == 0)
    def _():
        carry_ref[...] = jnp.zeros_like(carry_ref)

    x = x_ref[...]
    lane = lax.broadcasted_iota(jnp.int32, x.shape, 1)
    is_g = (lane >= L_G) & (lane < L_G + NH)
    is_f = (lane >= L_F) & (lane < L_F + NH)
    a_rate = jnp.exp(par_ref[0:1, :])
    z = x + par_ref[1:2, :]
    t = jnp.log1p(jnp.exp(-jnp.abs(z)))
    sig = _sigmoid(x)
    g = -a_rate * (jnp.maximum(z, 0.0) + t)
    logf = jnp.minimum(z, 0.0) - t
    vals = jnp.where(is_g, g, jnp.where(is_f, logf, 0.0))
    r = lax.broadcasted_iota(jnp.int32, (ts, ts), 0)
    c = lax.broadcasted_iota(jnp.int32, (ts, ts), 1)
    tri_full = jnp.where(c <= r, 1.0, 0.0).astype(BF16)
    tri_blk = jnp.where((c <= r) & ((r // CHUNK) == (c // CHUNK)), 1.0, 0.0).astype(BF16)
    v0, v1, v2 = _split3(vals)
    cs_full = _dot(tri_full, v0) + _dot(tri_full, v1) + _dot(tri_full, v2)
    cs_blk = _dot(tri_blk, v0) + _dot(tri_blk, v1) + _dot(tri_blk, v2)
    cum = cs_full + carry_ref[...]
    carry_ref[...] = cum[ts - 1:ts, :]
    out = jnp.where(lane < NH, sig, jnp.where(is_g, cs_blk, jnp.where(is_f, cum, 0.0)))
    col_ref[...] = out
    out_t = out.T
    rowf_ref[...] = out_t[L_F:L_F + NH, :]
    for ci in range(ts // CHUNK):
        rowg_ref[ci] = out_t[0:32, ci * CHUNK:(ci + 1) * CHUNK]


def _prep(small, par, bsz, seq):
    m = small.shape[0]
    ts = 256
    nt = seq // ts
    return pl.pallas_call(
        functools.partial(_prep_kernel, ts=ts),
        grid=(bsz, nt),
        in_specs=[pl.BlockSpec((ts, LANES), lambda b, j: (b * nt + j, 0)),
                  pl.BlockSpec((8, LANES), lambda b, j: (0, 0))],
        out_specs=[pl.BlockSpec((ts, LANES), lambda b, j: (b * nt + j, 0)),
                   pl.BlockSpec((NH, ts), lambda b, j: (0, b * nt + j)),
                   pl.BlockSpec((ts // CHUNK, 32, CHUNK), lambda b, j: (b * nt + j, 0, 0))],
        out_shape=[jax.ShapeDtypeStruct((m, LANES), F32),
                   jax.ShapeDtypeStruct((NH, m), F32),
                   jax.ShapeDtypeStruct((m // CHUNK, 32, CHUNK), F32)],
        scratch_shapes=[pltpu.VMEM((1, LANES), F32)],
        compiler_params=_cparams(("arbitrary", "arbitrary")), name="prep",
    )(small, par)


HALO = 16


def _bmm(a, b):
    return lax.dot_general(a.astype(BF16), b.astype(BF16), (((2,), (1,)), ((0,), (0,))),
                           preferred_element_type=F32)


def _bmm_nt(a, b):
    return lax.dot_general(a.astype(BF16), b.astype(BF16), (((2,), (2,)), ((0,), (0,))),
                           preferred_element_type=F32)


def _delta_kernel(qkv_ref, halo_ref, z_ref, cw_ref, col_ref, rowg_ref, gain_ref, o_ref, s_ref, *, nc):
    j = pl.program_id(1)

    @pl.when(j == 0)
    def _():
        s_ref[...] = jnp.zeros_like(s_ref)

    ts = nc * CHUNK
    n = NH * nc
    xb = qkv_ref[...]
    halo = halo_ref[...]
    halo = jnp.where(j == 0, jnp.zeros_like(halo), halo)
    xx = jnp.concatenate([halo, xb], axis=0)
    rr = lax.broadcasted_iota(jnp.int32, (3 * ts, HALO + ts), 0)
    cc = lax.broadcasted_iota(jnp.int32, (3 * ts, HALO + ts), 1)
    sel = jnp.where(cc == (rr % ts) + HALO - 1 - rr // ts, 1.0, 0.0).astype(BF16)
    shifted = _dot(sel, xx)
    cw = cw_ref[...]
    acc = xb.astype(F32) * cw[3:4, :]
    for dlt in (1, 2, 3):
        acc = acc + shifted[(dlt - 1) * ts:dlt * ts, :] * cw[3 - dlt:4 - dlt, :]
    act = acc * _sigmoid(acc)

    cv = col_ref[...]
    rg = rowg_ref[...]
    qs, ks, vs, betas, gcs, grs = [], [], [], [], [], []
    for h in range(NH):
        qh = act[:, h * HD:(h + 1) * HD]
        kh = act[:, BW + h * HD:BW + (h + 1) * HD]
        vh = act[:, 2 * BW + h * HD:2 * BW + (h + 1) * HD]
        qh = qh * (lax.rsqrt(jnp.sum(qh * qh, axis=-1, keepdims=True) + EPS) * (HD ** -0.5))
        kh = kh * lax.rsqrt(jnp.sum(kh * kh, axis=-1, keepdims=True) + EPS)
        qs.append(qh.reshape(nc, CHUNK, HD))
        ks.append(kh.reshape(nc, CHUNK, HD))
        vs.append(vh.reshape(nc, CHUNK, HD))
        betas.append(cv[:, L_BETA + h:L_BETA + h + 1].reshape(nc, CHUNK, 1))
        gcs.append(cv[:, L_G + h:L_G + h + 1].reshape(nc, CHUNK, 1))
        grs.append(rg[:, L_G + h:L_G + h + 1, :])
    q = jnp.concatenate(qs, axis=0)
    k = jnp.concatenate(ks, axis=0)
    v = jnp.concatenate(vs, axis=0)
    beta = jnp.concatenate(betas, axis=0)
    gc = jnp.concatenate(gcs, axis=0)
    gr = jnp.concatenate(grs, axis=0)

    ri = lax.broadcasted_iota(jnp.int32, (CHUNK, CHUNK), 0)
    ci = lax.broadcasted_iota(jnp.int32, (CHUNK, CHUNK), 1)
    causal = (ci <= ri)[None]
    strict = (ci < ri)[None]
    decay = jnp.exp(jnp.where(causal, gc - gr, -jnp.inf))
    kb = k * beta
    both = _bmm_nt(jnp.concatenate([kb, q], axis=1), k)
    lower = jnp.where(strict, both[:, :CHUNK] * decay, 0.0)
    qk = both[:, CHUNK:] * decay

    eye = jnp.where(ci == ri, 1.0, 0.0)[None]
    t_inv = eye - lower
    pw = _bmm(lower, lower)
    for it in range(5):
        t_inv = t_inv + _bmm(t_inv, pw)
        if it < 4:
            pw = _bmm(pw, pw)

    eg = jnp.exp(gc)
    rhs = jnp.concatenate([v * beta, kb * eg], axis=-1)
    sol = _bmm(t_inv, rhs)
    u = sol[:, :, :HD]
    w = sol[:, :, HD:]
    q_dec = q * eg
    g_end = gc[:, CHUNK - 1:CHUNK, :]
    k_dec = k * jnp.exp(g_end - gc)
    g_last = jnp.exp(g_end)

    def pick(t, c):
        return t.reshape((NH, nc) + t.shape[1:])[:, c]

    state = s_ref[...]
    outs = []
    for c in range(nc):
        ws_qs = _bmm(jnp.concatenate([pick(w, c), pick(q_dec, c)], axis=1), state)
        v_new = pick(u, c) - ws_qs[:, :CHUNK]
        o_c = ws_qs[:, CHUNK:] + _bmm(pick(qk, c), v_new)
        kd_t = jnp.swapaxes(pick(k_dec, c), 1, 2)
        state = state * pick(g_last, c) + _bmm(kd_t, v_new)
        outs.append(o_c)
    s_ref[...] = state

    gain = gain_ref[...]
    zf = z_ref[...].astype(F32)
    cols = []
    for h in range(NH):
        oh = jnp.concatenate([outs[c][h] for c in range(nc)], axis=0)
        oh = oh * lax.rsqrt(jnp.mean(oh * oh, axis=-1, keepdims=True) + EPS) * gain
        zh = zf[:, h * HD:(h + 1) * HD]
        cols.append(oh * (zh * _sigmoid(zh)))
    o_ref[...] = jnp.concatenate(cols, axis=1).astype(BF16)


def _delta(proj, conv_w8, col, rowg, gain, bsz, seq):
    m = proj.shape[0]
    nc = 4
    ts = nc * CHUNK
    nt = seq // ts
    qkv_blk = P_DN // (3 * BW)
    return pl.pallas_call(
        functools.partial(_delta_kernel, nc=nc),
        grid=(bsz, nt),
        in_specs=[pl.BlockSpec((ts, 3 * BW), lambda b, j: (b * nt + j, qkv_blk)),
                  pl.BlockSpec((HALO, 3 * BW), lambda b, j: (jnp.maximum((b * nt + j) * (ts // HALO) - 1, 0), qkv_blk)),
                  pl.BlockSpec((ts, BW), lambda b, j: (b * nt + j, P_DNZ // BW)),
                  pl.BlockSpec((8, 3 * BW), lambda b, j: (0, 0)),
                  pl.BlockSpec((ts, LANES), lambda b, j: (b * nt + j, 0)),
                  pl.BlockSpec((nc, 32, CHUNK), lambda b, j: (b * nt + j, 0, 0)),
                  pl.BlockSpec((1, HD), lambda b, j: (0, 0))],
        out_specs=pl.BlockSpec((ts, BW), lambda b, j: (b * nt + j, 0)),
        out_shape=jax.ShapeDtypeStruct((m, BW), BF16),
        scratch_shapes=[pltpu.VMEM((NH, HD, HD), F32)],
        compiler_params=_cparams(("arbitrary", "arbitrary")), name="delta",
    )(proj, proj, proj, conv_w8, col, rowg, gain.reshape(1, HD))


FOX_T = 512
FOX_HPS = 4


def _fox_kernel(q_ref, k_ref, v_ref, crow_ref, o_ref, vx_ref, m_ref, acc_ref):
    t = FOX_T
    hp = pl.program_id(1)
    qi = pl.program_id(2)

    @pl.when(qi == 0)
    def _():
        lane = lax.broadcasted_iota(jnp.int32, (v_ref.shape[0], HD), 1)
        ones_col = jnp.where(lane == 0, 1.0, 0.0).astype(BF16)
        for e in range(FOX_HPS):
            vx_ref[e, :, :HD] = v_ref[:, e * HD:(e + 1) * HD]
            vx_ref[e, :, HD:] = ones_col

    m_ref[...] = jnp.full_like(m_ref, -jnp.inf)
    acc_ref[...] = jnp.zeros_like(acc_ref)

    def block(ki, masked):
        off = pl.multiple_of(ki * t, t)
        for e in range(FOX_HPS):
            s = lax.dot_general(q_ref[:, e * HD:(e + 1) * HD], k_ref[pl.ds(off, t), e * HD:(e + 1) * HD],
                                (((1,), (1,)), ((), ())), preferred_element_type=F32)
            s = s - crow_ref[pl.ds(FOX_HPS * hp + e, 1), pl.ds(off, t)]
            if masked:
                row = lax.broadcasted_iota(jnp.int32, (t, t), 0)
                col = lax.broadcasted_iota(jnp.int32, (t, t), 1)
                s = jnp.where(col <= row, s, -jnp.inf)
            m_prev = m_ref[e]
            m_new = jnp.maximum(m_prev, jnp.max(s, axis=1, keepdims=True))
            alpha = jnp.exp(m_prev - m_new)
            p = jnp.exp((s - jnp.tile(m_new, (1, t // LANES))).astype(BF16))
            acc_ref[e] = jnp.tile(alpha, (1, 2)) * acc_ref[e] + _dot(p, vx_ref[e, pl.ds(off, t), :])
            m_ref[e] = m_new

    def body(ki, carry):
        block(ki, False)
        return carry

    lax.fori_loop(0, qi, body, 0)
    block(qi, True)
    for e in range(FOX_HPS):
        o_ref[:, e * HD:(e + 1) * HD] = (acc_ref[e, :, :HD] / acc_ref[e, :, HD:HD + 1]).astype(BF16)


def _fox(proj, rowf, bsz, seq):
    m = proj.shape[0]
    t = FOX_T
    nq = seq // t
    w = FOX_HPS * HD
    return pl.pallas_call(
        _fox_kernel,
        grid=(bsz, NH // FOX_HPS, nq),
        in_specs=[pl.BlockSpec((t, w), lambda b, h, qi: (b * nq + qi, P_FQ // w + h)),
                  pl.BlockSpec((seq, w), lambda b, h, qi: (b, P_FK // w + h)),
                  pl.BlockSpec((seq, w), lambda b, h, qi: (b, P_FV // w + h)),
                  pl.BlockSpec((NH, seq), lambda b, h, qi: (0, b))],
        out_specs=pl.BlockSpec((t, w), lambda b, h, qi: (b * nq + qi, h)),
        out_shape=jax.ShapeDtypeStruct((m, BW), BF16),
        scratch_shapes=[pltpu.VMEM((FOX_HPS, seq, 2 * HD), BF16), pltpu.VMEM((FOX_HPS, t, HD), F32),
                        pltpu.VMEM((FOX_HPS, t, 2 * HD), F32)],
        compiler_params=_cparams(("arbitrary", "arbitrary", "arbitrary")), name="fox",
    )(proj, proj, proj, rowf)


def _swap_halves(x):
    w = x.shape[-1]
    lane = lax.broadcasted_iota(jnp.int32, x.shape, x.ndim - 1)
    return jnp.where((lane % SWA_D) < SWA_D // 2, pltpu.roll(x, w - SWA_D // 2, axis=x.ndim - 1),
                     pltpu.roll(x, SWA_D // 2, axis=x.ndim - 1))


def _swa_kernel(sink_ref, q_ref, kc_ref, kp_ref, vc_ref, vp_ref, cc_ref, sc_ref, cp_ref, sp_ref, o_ref, *, nblk):
    i = pl.program_id(0)
    first = (i % nblk) == 0
    cos_c, sin_c = cc_ref[...], sc_ref[...]
    q = q_ref[...].astype(F32)
    q = q * jnp.tile(cos_c, (1, SWA_QH // 2)) + _swap_halves(q) * jnp.tile(sin_c, (1, SWA_QH // 2))
    kc = kc_ref[...].astype(F32)
    kc = kc * cos_c + _swap_halves(kc) * sin_c
    kp = kp_ref[...].astype(F32)
    kp = kp * cp_ref[...] + _swap_halves(kp) * sp_ref[...]
    kk = jnp.concatenate([kp, kc], axis=0).astype(BF16)
    vv = jnp.concatenate([vp_ref[...], vc_ref[...]], axis=0)
    r = lax.broadcasted_iota(jnp.int32, (WIN, 2 * WIN), 0)
    c = lax.broadcasted_iota(jnp.int32, (WIN, 2 * WIN), 1)
    mask = (c > r) & (c <= r + WIN) & ((c >= WIN) | jnp.logical_not(first))
    qb = q.astype(BF16)
    outs = []
    for hq in range(SWA_QH):
        g = hq // SWA_G
        qh = qb[:, hq * SWA_D:(hq + 1) * SWA_D]
        kh = kk[:, g * SWA_D:(g + 1) * SWA_D]
        vh = vv[:, g * SWA_D:(g + 1) * SWA_D]
        s = lax.dot_general(qh, kh, (((1,), (1,)), ((), ())), preferred_element_type=F32) * (SWA_D ** -0.5)
        s = jnp.where(mask, s, -jnp.inf)
        sink = sink_ref[hq]
        mx = jnp.maximum(jnp.max(s, axis=1, keepdims=True), sink)
        p = jnp.exp(s - mx)
        den = jnp.sum(p, axis=1, keepdims=True) + jnp.exp(sink - mx)
        outs.append(_dot(p.astype(BF16), vh) / den)
    o_ref[...] = jnp.concatenate(outs, axis=1).astype(BF16)


def _swa(proj, sinks, cos4, sin4, bsz, seq):
    m = proj.shape[0]
    nblk = seq // WIN
    prev = lambda i: jnp.maximum(i - 1, 0)
    kcol, vcol = P_SK // LANES, P_SV // LANES
    return pl.pallas_call(
        functools.partial(_swa_kernel, nblk=nblk),
        grid=(m // WIN,),
        in_specs=[pl.BlockSpec(memory_space=pltpu.SMEM),
                  pl.BlockSpec((WIN, BW), lambda i: (i, P_SQ // BW)),
                  pl.BlockSpec((WIN, LANES), lambda i: (i, kcol)),
                  pl.BlockSpec((WIN, LANES), lambda i: (prev(i), kcol)),
                  pl.BlockSpec((WIN, LANES), lambda i: (i, vcol)),
                  pl.BlockSpec((WIN, LANES), lambda i: (prev(i), vcol)),
                  pl.BlockSpec((WIN, LANES), lambda i: (i, 0)),
                  pl.BlockSpec((WIN, LANES), lambda i: (i, 0)),
                  pl.BlockSpec((WIN, LANES), lambda i: (prev(i), 0)),
                  pl.BlockSpec((WIN, LANES), lambda i: (prev(i), 0))],
        out_specs=pl.BlockSpec((WIN, BW), lambda i: (i, 0)),
        out_shape=jax.ShapeDtypeStruct((m, BW), BF16),
        compiler_params=_cparams(("arbitrary",)), name="swa",
    )(sinks, proj, proj, proj, proj, proj, cos4, sin4, cos4, sin4)


def _merge_kernel(oa_ref, ob_ref, oc_ref, ga_ref, gb_ref, gc_ref, w_ref, o_ref):
    acc = _sigmoid(ga_ref[...].astype(F32)) * _dot(oa_ref[...], w_ref[0])
    acc = acc + _sigmoid(gb_ref[...].astype(F32)) * _dot(ob_ref[...], w_ref[1])
    acc = acc + _sigmoid(gc_ref[...].astype(F32)) * _dot(oc_ref[...], w_ref[2])
    o_ref[...] = acc.astype(BF16)


def _merge(o_a, o_b, o_c, proj, w_branch):
    m = o_a.shape[0]
    tm, tn = 1024, 512
    gblk = lambda b: (lambda j, i: (i, (P_GATE + b * D) // tn + j))
    oblk = pl.BlockSpec((tm, BW), lambda j, i: (i, 0))
    return pl.pallas_call(
        _merge_kernel, grid=(D // tn, m // tm),
        in_specs=[oblk, oblk, oblk,
                  pl.BlockSpec((tm, tn), gblk(0)), pl.BlockSpec((tm, tn), gblk(1)), pl.BlockSpec((tm, tn), gblk(2)),
                  pl.BlockSpec((3, BW, tn), lambda j, i: (0, 0, j))],
        out_specs=pl.BlockSpec((tm, tn), lambda j, i: (i, j)),
        out_shape=jax.ShapeDtypeStruct((m, D), BF16),
        compiler_params=_cparams(("arbitrary", "arbitrary")), name="merge",
    )(o_a, o_b, o_c, proj, proj, proj, w_branch)


def _swiglu_acc(h, wg_ref, wu_ref, wd_ref, acc_ref):
    a = _dot(h, wg_ref[...])
    b = _dot(h, wu_ref[...])
    t = (a * _sigmoid(a) * b).astype(BF16)
    acc_ref[...] += _dot(t, wd_ref[...])


def _ffn_kernel(h_ref, wg_ref, wu_ref, wd_ref, x_ref, g_ref, o_ref, acc_ref, *, nf):
    f = pl.program_id(1)

    @pl.when(f == 0)
    def _():
        acc_ref[...] = jnp.zeros_like(acc_ref)

    _swiglu_acc(h_ref[...], wg_ref, wu_ref, wd_ref, acc_ref)

    @pl.when(f == nf - 1)
    def _():
        o_ref[...] = x_ref[...] + g_ref[...] * acc_ref[...]


def _ffn_dense(h, wg, wu, wd, x, gate, seq):
    m, d = h.shape
    ff = wg.shape[1]
    tm, tf = 512, 1024
    nf = ff // tf
    tpb = seq // tm
    return pl.pallas_call(
        functools.partial(_ffn_kernel, nf=nf), grid=(m // tm, nf),
        in_specs=[pl.BlockSpec((tm, d), lambda i, f: (i, 0)),
                  pl.BlockSpec((d, tf), lambda i, f: (0, f)),
                  pl.BlockSpec((d, tf), lambda i, f: (0, f)),
                  pl.BlockSpec((tf, d), lambda i, f: (f, 0)),
                  pl.BlockSpec((tm, d), lambda i, f: (i, 0)),
                  pl.BlockSpec((None, 1, d), lambda i, f: (i // tpb, 0, 0))],
        out_specs=pl.BlockSpec((tm, d), lambda i, f: (i, 0)),
        out_shape=jax.ShapeDtypeStruct((m, d), F32),
        scratch_shapes=[pltpu.VMEM((tm, d), F32)],
        compiler_params=_cparams(("arbitrary", "arbitrary")), name="ffn_dense",
    )(h, wg, wu, wd, x, gate)


def _row_copy(src_hbm, idx_ref, pos, dst, r, sem):
    return pltpu.make_async_copy(src_hbm.at[pl.ds(idx_ref[pos], 1), :], dst.at[pl.ds(r, 1), :], sem)


def _start_row_gather(src_hbm, idx_ref, base, n, dst, sem):
    def body(r, carry):
        _row_copy(src_hbm, idx_ref, base + r, dst, r, sem).start()
        return carry

    lax.fori_loop(0, n, body, 0, unroll=8)


def _wait_row_gather(src_hbm, n, dst, sem):
    pltpu.make_async_copy(src_hbm.at[pl.ds(0, n), :], dst, sem).wait()


def _ffn_group_kernel(te_ref, tv_ref, src_ref, h_hbm, wg_ref, wu_ref, wd_ref, o_ref,
                      buf, hbf_ref, acc_ref, sem, *, nf, tm, n_tiles):
    i = pl.program_id(0)
    f = pl.program_id(1)
    slot = i % 2
    live = tv_ref[i] > 0
    per = -(-tm // nf)
    last = tm - per * (nf - 1)

    @pl.when(f == 0)
    def _():
        @pl.when(i == 0)
        def _():
            _start_row_gather(h_hbm, src_ref, 0, tm, buf.at[0], sem.at[0])

        _wait_row_gather(h_hbm, tm, buf.at[slot], sem.at[slot])
        hbf_ref[...] = buf[slot].astype(BF16)
        acc_ref[...] = jnp.zeros_like(acc_ref)

    def prefetch(first, count):
        for j in range(count):
            r = first + j
            _row_copy(h_hbm, src_ref, (i + 1) * tm + r, buf.at[1 - slot], r, sem.at[1 - slot]).start()

    def step(count, compute):
        c0 = count // 3
        c1 = (count - c0) // 2
        first = f * per
        if not compute:
            prefetch(first, count)
            return
        h = hbf_ref[...]
        a = _dot(h, wg_ref[...])
        prefetch(first, c0)
        b = _dot(h, wu_ref[...])
        prefetch(first + c0, c1)
        t = (a * _sigmoid(a) * b).astype(BF16)
        acc_ref[...] += _dot(t, wd_ref[...])
        prefetch(first + c0 + c1, count - c0 - c1)

    for is_last, count in ((False, per), (True, last)):
        for compute in (True, False):
            cond = (f == nf - 1) if is_last else (f < nf - 1)
            cond = cond & (live if compute else jnp.logical_not(live))
            pl.when(cond)(functools.partial(step, count, compute))

    @pl.when(f == nf - 1)
    def _():
        o_ref[...] = acc_ref[...]

        @pl.when(i == n_tiles - 1)
        def _():
            _wait_row_gather(h_hbm, tm, buf.at[1 - slot], sem.at[1 - slot])


def _ffn_grouped(h, wg, wu, wd, tile_expert, tile_valid, src_rows, tm):
    rows = src_rows.shape[0] - tm
    d, ff = wg.shape[1], wg.shape[2]
    tf = 1024
    nf = ff // tf
    n_tiles = rows // tm

    def fsel(i, f, tv):
        return jnp.where(tv[i] > 0, f, nf - 1)

    grid_spec = pltpu.PrefetchScalarGridSpec(
        num_scalar_prefetch=3, grid=(n_tiles, nf),
        in_specs=[pl.BlockSpec(memory_space=pl.ANY),
                  pl.BlockSpec((None, d, tf), lambda i, f, te, tv, sr: (te[i], 0, fsel(i, f, tv))),
                  pl.BlockSpec((None, d, tf), lambda i, f, te, tv, sr: (te[i], 0, fsel(i, f, tv))),
                  pl.BlockSpec((None, tf, d), lambda i, f, te, tv, sr: (te[i], fsel(i, f, tv), 0))],
        out_specs=pl.BlockSpec((tm, d), lambda i, f, te, tv, sr: (i, 0)),
        scratch_shapes=[pltpu.VMEM((2, tm, d), F32),
                        pltpu.VMEM((tm, d), BF16),
                        pltpu.VMEM((tm, d), F32),
                        pltpu.SemaphoreType.DMA((2,))])
    return pl.pallas_call(
        functools.partial(_ffn_group_kernel, nf=nf, tm=tm, n_tiles=n_tiles), grid_spec=grid_spec,
        out_shape=jax.ShapeDtypeStruct((rows, d), F32),
        compiler_params=_cparams(("arbitrary", "arbitrary")), name="ffn_grouped",
    )(tile_expert, tile_valid, src_rows, h, wg, wu, wd)


COMBINE_TM = 256


def _combine_kernel(pos_ref, x_ref, y_hbm, r_ref, g_ref, gain_ref, o_ref, buf, sem, *, tm, nt, m, final):
    i = pl.program_id(0)
    slot = i % 2

    def start(tile, s):
        for k in range(2):
            _start_row_gather(y_hbm, pos_ref, k * m + tile * tm, tm, buf.at[s, k], sem.at[s])

    @pl.when(i == 0)
    def _():
        start(0, 0)

    @pl.when(i + 1 < nt)
    def _():
        start(jnp.minimum(i + 1, nt - 1), 1 - slot)

    for k in range(2):
        _wait_row_gather(y_hbm, tm, buf.at[slot, k], sem.at[slot])
    w0 = r_ref[:, 2:3]
    w1 = r_ref[:, 3:4]
    xn = x_ref[...] + g_ref[...] * (w0 * buf[slot, 0] + w1 * buf[slot, 1])
    if final:
        xn = xn * lax.rsqrt(jnp.mean(xn * xn, axis=-1, keepdims=True) + EPS) * gain_ref[...]
    o_ref[...] = xn


def _combine(x, y, pos_rows, route, gate, seq, final_gain=None):
    m, d = x.shape
    tm = COMBINE_TM
    nt = m // tm
    tpb = seq // tm
    final = final_gain is not None
    gain = (final_gain if final else jnp.ones((d,), F32)).reshape(1, d)
    grid_spec = pltpu.PrefetchScalarGridSpec(
        num_scalar_prefetch=1, grid=(nt,),
        in_specs=[pl.BlockSpec((tm, d), lambda i, p: (i, 0)),
                  pl.BlockSpec(memory_space=pl.ANY),
                  pl.BlockSpec((tm, LANES), lambda i, p: (i, 0)),
                  pl.BlockSpec((None, 1, d), lambda i, p: (i // tpb, 0, 0)),
                  pl.BlockSpec((1, d), lambda i, p: (0, 0))],
        out_specs=pl.BlockSpec((tm, d), lambda i, p: (i, 0)),
        scratch_shapes=[pltpu.VMEM((2, 2, tm, d), F32), pltpu.SemaphoreType.DMA((2,))])
    return pl.pallas_call(
        functools.partial(_combine_kernel, tm=tm, nt=nt, m=m, final=final), grid_spec=grid_spec,
        out_shape=jax.ShapeDtypeStruct((m, d), F32),
        compiler_params=_cparams(("arbitrary",)), name="combine",
    )(pos_rows, x, y, route, gate, gain)


def _final_norm_kernel(x_ref, gain_ref, o_ref):
    x = x_ref[...]
    o_ref[...] = x * lax.rsqrt(jnp.mean(x * x, axis=-1, keepdims=True) + EPS) * gain_ref[...]


def _final_norm(x, gain):
    m, d = x.shape
    tm = 512
    blk = pl.BlockSpec((tm, d), lambda i: (i, 0))
    return pl.pallas_call(
        _final_norm_kernel, grid=(m // tm,),
        in_specs=[blk, pl.BlockSpec((1, d), lambda i: (0, 0))],
        out_specs=blk, out_shape=jax.ShapeDtypeStruct((m, d), F32),
        compiler_params=_cparams(("arbitrary",)), name="final_norm",
    )(x, gain.reshape(1, d))


MOE_TM = 512


def _moe(x, h, route, counts, wg, wu, wd, gate, seq, final_gain=None):
    m, d = x.shape
    tm = MOE_TM
    rows = 2 * m + NE * tm
    n_tiles = rows // tm
    e_flat = route[:, 0:2].astype(jnp.int32).reshape(-1)
    rank = route[:, 4:6].astype(jnp.int32).reshape(-1)
    counts = counts[0, :NE].astype(jnp.int32)
    padded = ((counts + tm - 1) // tm) * tm
    ends = jnp.cumsum(padded)
    starts = ends - padded
    pos = jnp.take(starts, e_flat) + rank
    src_token = jnp.zeros((rows + tm,), jnp.int32).at[pos].set(jnp.arange(2 * m, dtype=jnp.int32) // 2)
    tile_start = jnp.arange(n_tiles, dtype=jnp.int32) * tm
    tile_expert = jnp.minimum(jnp.sum((tile_start[:, None] >= ends[None, :]).astype(jnp.int32), axis=1), NE - 1)
    tile_valid = (tile_start < ends[-1]).astype(jnp.int32)

    ys = _ffn_grouped(h, wg, wu, wd, tile_expert, tile_valid, src_token, tm)
    pos_rows = pos.reshape(m, 2).T.reshape(-1)
    return _combine(x, ys, pos_rows, route, gate, seq, final_gain)


def _regroup_w_in(w):
    dn = w[:, 0:4096]
    small = jnp.concatenate([w[:, 4096:4112], w[:, 7184:7192]], axis=1)
    fox = jnp.concatenate([w[:, 4112:5136] * (HD ** -0.5), w[:, 5136:7184]], axis=1)
    swa = w[:, 7192:8472]
    gates = w[:, 8472:14616]
    main = jnp.concatenate([gates, dn, fox, swa], axis=1).astype(BF16)
    small = jnp.pad(small, ((0, 0), (0, LANES - small.shape[1]))).astype(BF16)
    return main, small


def kernel(x, c, positions, w_ada, b_ada, norm_mix, w_in, conv_w, dn_a_log, dn_dt_bias, dn_norm,
           fox_b_forget, swa_sinks, w_branch, w_out, norm_ffn, ffn_w_gate, ffn_w_up, ffn_w_down,
           moe_router, moe_w_gate, moe_w_up, moe_w_down, final_norm):
    bsz, seq, d = x.shape
    depth = w_ada.shape[0]
    m = bsz * seq
    xf = x.reshape(m, d)

    inv_freq = 10000.0 ** (-jnp.arange(0, SWA_D, 2, dtype=F32) / SWA_D)
    ang = positions.astype(F32).reshape(m, 1) * inv_freq[None, :]
    cos, sin = jnp.cos(ang), jnp.sin(ang)
    cos4 = jnp.tile(cos, (1, 4))
    sin4 = jnp.tile(jnp.concatenate([-sin, sin], axis=1), (1, 2))

    c_pad = jnp.pad(c, ((0, 8 - bsz), (0, 0)))
    mod = _ada(c_pad, w_ada, b_ada)[:, :bsz].reshape(depth, bsz, 6, 1, d)

    for layer in range(depth):
        sh1, sc1, g1, sh2, sc2, g2 = (mod[layer, :, t] for t in range(6))
        w_main, w_small = _regroup_w_in(w_in[layer])
        h = _norm_modulate(xf, norm_mix[layer], sc1, sh1, seq)
        proj = _matmul(h, w_main, BF16, 1024, 2432, "in_proj")
        small = _matmul(h, w_small, F32, 1024, LANES, "in_proj_small")
        par = jnp.zeros((8, LANES), F32)
        par = par.at[0, L_G:L_G + NH].set(dn_a_log[layer])
        par = par.at[1, L_G:L_G + NH].set(dn_dt_bias[layer])
        par = par.at[1, L_F:L_F + NH].set(fox_b_forget[layer])
        col, rowf, rowg = _prep(small, par, bsz, seq)
        conv_w8 = jnp.pad(conv_w[layer], ((0, 4), (0, 0)))
        o_a = _delta(proj, conv_w8, col, rowg, dn_norm[layer], bsz, seq)
        o_b = _fox(proj, rowf, bsz, seq)
        o_c = _swa(proj, swa_sinks[layer], cos4, sin4, bsz, seq)
        merged = _merge(o_a, o_b, o_c, proj, w_branch[layer].astype(BF16))
        xf = _matmul_resid(merged, w_out[layer].astype(BF16), xf, g1, seq, 512, 2048, "out_proj")
        if layer % 2 == 0:
            i = layer // 2
            h2 = _norm_modulate(xf, norm_ffn[layer], sc2, sh2, seq)
            xf = _ffn_dense(h2, ffn_w_gate[i].astype(BF16), ffn_w_up[i].astype(BF16),
                            ffn_w_down[i].astype(BF16), xf, g2, seq)
        else:
            i = layer // 2
            h2, route, counts = _norm_modulate(xf, norm_ffn[layer], sc2, sh2, seq, w_router=moe_router[i])
            last = layer == depth - 1
            xf = _moe(xf, h2, route, counts, moe_w_gate[i].astype(BF16), moe_w_up[i].astype(BF16),
                      moe_w_down[i].astype(BF16), g2, seq, final_gain=final_norm if last else None)
            if last:
                return xf.reshape(bsz, seq, d)
    return _final_norm(xf, final_norm).reshape(bsz, seq, d)
```

```python
import functools

import jax
import jax.numpy as jnp
from jax import lax
from jax.experimental import pallas as pl
from jax.experimental.pallas import tpu as pltpu

F32 = jnp.float32
BF16 = jnp.bfloat16
EPS = 1e-6

D = 2048
HD = 128
NH = 8
BW = 1024
CHUNK = 64
SWA_D = 64
SWA_QH = 16
SWA_KVH = 2
SWA_G = 8
WIN = 128
D_FF = 7168
NE = 8
LANES = 128

P_GATE = 0
P_DN = 6144
P_DNZ = 9216
P_FQ, P_FK, P_FV = 10240, 11264, 12288
P_SQ, P_SK, P_SV = 13312, 14336, 14464
NP = 14592
L_BETA, L_G, L_F = 0, 8, 16

VMEM_LIMIT = 56 * 1024 * 1024


def _cparams(sem):
    return pltpu.CompilerParams(dimension_semantics=sem, vmem_limit_bytes=VMEM_LIMIT)


def _sigmoid(x):
    return 1.0 / (1.0 + jnp.exp(-x))


def _split3(x):
    hi = x.astype(BF16)
    r = x - hi.astype(F32)
    mid = r.astype(BF16)
    lo = (r - mid.astype(F32)).astype(BF16)
    return hi, mid, lo


def _dot(a, b):
    return jnp.dot(a, b, preferred_element_type=F32)


def _dot_hi(a, b):
    a0, a1, a2 = _split3(a)
    b0, b1, b2 = _split3(b)
    return (_dot(a0, b0) + (_dot(a0, b1) + _dot(a1, b0))
            + (_dot(a0, b2) + _dot(a1, b1) + _dot(a2, b0)))


def _ada_kernel(c_ref, w_ref, b_ref, o_ref):
    c = c_ref[...]
    ca = c * _sigmoid(c)
    a0 = ca.astype(BF16)
    a1 = (ca - a0.astype(F32)).astype(BF16)
    w = w_ref[...]
    w0 = w.astype(BF16)
    w1 = (w - w0.astype(F32)).astype(BF16)
    o_ref[...] = _dot(a0, w0) + (_dot(a0, w1) + _dot(a1, w0)) + b_ref[...]


def _ada(c_pad, w_ada, b_ada):
    depth, d, n = w_ada.shape
    tn = 1536
    return pl.pallas_call(
        _ada_kernel,
        grid=(depth, n // tn),
        in_specs=[pl.BlockSpec((8, d), lambda l, j: (0, 0)),
                  pl.BlockSpec((None, d, tn), lambda l, j: (l, 0, j)),
                  pl.BlockSpec((None, 1, tn), lambda l, j: (l, 0, j))],
        out_specs=pl.BlockSpec((None, 8, tn), lambda l, j: (l, 0, j)),
        out_shape=jax.ShapeDtypeStruct((depth, 8, n), F32),
        compiler_params=_cparams(("arbitrary", "arbitrary")),
        name="ada",
    )(c_pad, w_ada, b_ada.reshape(depth, 1, n))


def _normmod(x_ref, gain_ref, sc_ref, sh_ref):
    x = x_ref[...]
    ms = jnp.mean(x * x, axis=-1, keepdims=True)
    y = x * lax.rsqrt(ms + EPS) * gain_ref[...]
    return y * (1.0 + sc_ref[...]) + sh_ref[...]


def _normmod_kernel(x_ref, gain_ref, sc_ref, sh_ref, h_ref):
    h_ref[...] = _normmod(x_ref, gain_ref, sc_ref, sh_ref).astype(BF16)


def _normmod_route_kernel(x_ref, gain_ref, sc_ref, sh_ref, wr_ref, h_ref, r_ref, cnt_ref, carry_ref):
    @pl.when(pl.program_id(0) == 0)
    def _():
        carry_ref[...] = jnp.zeros_like(carry_ref)

    h = _normmod(x_ref, gain_ref, sc_ref, sh_ref)
    h_ref[...] = h
    logits = _dot_hi(h, wr_ref[...])
    lane = lax.broadcasted_iota(jnp.int32, logits.shape, 1)
    neg = jnp.float32(-jnp.inf)
    l1 = jnp.where(lane < NE, logits, neg)
    m1 = jnp.max(l1, axis=-1, keepdims=True)
    i1 = jnp.min(jnp.where(l1 == m1, lane, LANES), axis=-1, keepdims=True)
    l2 = jnp.where(lane == i1, neg, l1)
    m2 = jnp.max(l2, axis=-1, keepdims=True)
    i2 = jnp.min(jnp.where(l2 == m2, lane, LANES), axis=-1, keepdims=True)
    e = jnp.exp(m2 - m1)
    w1 = 1.0 / (1.0 + e)
    w2 = e / (1.0 + e)
    tm = logits.shape[0]
    picked = jnp.where((lane == i1) | (lane == i2), 1.0, 0.0)
    rr = lax.broadcasted_iota(jnp.int32, (tm, tm), 0)
    cc = lax.broadcasted_iota(jnp.int32, (tm, tm), 1)
    before = _dot(jnp.where(cc < rr, 1.0, 0.0).astype(BF16), picked.astype(BF16)) + carry_ref[...]
    rank1 = jnp.sum(jnp.where(lane == i1, before, 0.0), axis=-1, keepdims=True)
    rank2 = jnp.sum(jnp.where(lane == i2, before, 0.0), axis=-1, keepdims=True)
    carry_ref[...] += jnp.sum(picked, axis=0, keepdims=True)
    cnt_ref[...] = jnp.broadcast_to(carry_ref[...], cnt_ref.shape)
    vals = (i1.astype(F32), i2.astype(F32), w1, w2, rank1, rank2)
    out = jnp.zeros_like(logits)
    for pos, v in enumerate(vals):
        out = jnp.where(lane == pos, v, out)
    r_ref[...] = out


def _norm_modulate(x, gain, sc, sh, seq, w_router=None):
    m, d = x.shape
    tm = 512
    tpb = seq // tm
    in_specs = [pl.BlockSpec((tm, d), lambda i: (i, 0)),
                pl.BlockSpec((1, d), lambda i: (0, 0)),
                pl.BlockSpec((None, 1, d), lambda i: (i // tpb, 0, 0)),
                pl.BlockSpec((None, 1, d), lambda i: (i // tpb, 0, 0))]
    args = [x, gain.reshape(1, d), sc, sh]
    if w_router is None:
        return pl.pallas_call(
            _normmod_kernel, grid=(m // tm,), in_specs=in_specs,
            out_specs=pl.BlockSpec((tm, d), lambda i: (i, 0)),
            out_shape=jax.ShapeDtypeStruct((m, d), BF16),
            compiler_params=_cparams(("arbitrary",)), name="normmod",
        )(*args)
    wr = jnp.pad(w_router, ((0, 0), (0, LANES - NE)))
    return pl.pallas_call(
        _normmod_route_kernel, grid=(m // tm,),
        in_specs=in_specs + [pl.BlockSpec((d, LANES), lambda i: (0, 0))],
        out_specs=[pl.BlockSpec((tm, d), lambda i: (i, 0)),
                   pl.BlockSpec((tm, LANES), lambda i: (i, 0)),
                   pl.BlockSpec((8, LANES), lambda i: (0, 0))],
        out_shape=[jax.ShapeDtypeStruct((m, d), F32),
                   jax.ShapeDtypeStruct((m, LANES), F32),
                   jax.ShapeDtypeStruct((8, LANES), F32)],
        scratch_shapes=[pltpu.VMEM((1, LANES), F32)],
        compiler_params=_cparams(("arbitrary",)), name="normmod_route",
    )(*args, wr)


def _mm_kernel(a_ref, w_ref, o_ref):
    o_ref[...] = _dot(a_ref[...], w_ref[...]).astype(o_ref.dtype)


def _matmul(a, w, out_dtype, tm, tn, name):
    m, k = a.shape
    n = w.shape[1]
    return pl.pallas_call(
        _mm_kernel, grid=(n // tn, m // tm),
        in_specs=[pl.BlockSpec((tm, k), lambda j, i: (i, 0)),
                  pl.BlockSpec((k, tn), lambda j, i: (0, j))],
        out_specs=pl.BlockSpec((tm, tn), lambda j, i: (i, j)),
        out_shape=jax.ShapeDtypeStruct((m, n), out_dtype),
        compiler_params=_cparams(("arbitrary", "arbitrary")), name=name,
    )(a, w)


def _mm_resid_kernel(a_ref, w_ref, x_ref, g_ref, o_ref):
    o_ref[...] = x_ref[...] + g_ref[...] * _dot(a_ref[...], w_ref[...])


def _matmul_resid(a, w, x, gate, seq, tm, tn, name):
    m, k = a.shape
    n = w.shape[1]
    tpb = seq // tm
    return pl.pallas_call(
        _mm_resid_kernel, grid=(n // tn, m // tm),
        in_specs=[pl.BlockSpec((tm, k), lambda j, i: (i, 0)),
                  pl.BlockSpec((k, tn), lambda j, i: (0, j)),
                  pl.BlockSpec((tm, tn), lambda j, i: (i, j)),
                  pl.BlockSpec((None, 1, tn), lambda j, i: (i // tpb, 0, j))],
        out_specs=pl.BlockSpec((tm, tn), lambda j, i: (i, j)),
        out_shape=jax.ShapeDtypeStruct((m, n), F32),
        compiler_params=_cparams(("arbitrary", "arbitrary")), name=name,
    )(a, w, x, gate)


def _prep_kernel(x_ref, par_ref, col_ref, rowf_ref, rowg_ref, carry_ref, *, ts):
    @pl.when(pl.program_id(1) == 0)
    def _():
        carry_ref[...] = jnp.zeros_like(carry_ref)

    x = x_ref[...]
    lane = lax.broadcasted_iota(jnp.int32, x.shape, 1)
    is_g = (lane >= L_G) & (lane < L_G + NH)
    is_f = (lane >= L_F) & (lane < L_F + NH)
    a_rate = jnp.exp(par_ref[0:1, :])
    z = x + par_ref[1:2, :]
    t = jnp.log1p(jnp.exp(-jnp.abs(z)))
    sig = _sigmoid(x)
    g = -a_rate * (jnp.maximum(z, 0.0) + t)
    logf = jnp.minimum(z, 0.0) - t
    vals = jnp.where(is_g, g, jnp.where(is_f, logf, 0.0))
    r = lax.broadcasted_iota(jnp.int32, (ts, ts), 0)
    c = lax.broadcasted_iota(jnp.int32, (ts, ts), 1)
    tri_full = jnp.where(c <= r, 1.0, 0.0).astype(BF16)
    tri_blk = jnp.where((c <= r) & ((r // CHUNK) == (c // CHUNK)), 1.0, 0.0).astype(BF16)
    v0, v1, v2 = _split3(vals)
    cs_full = _dot(tri_full, v0) + _dot(tri_full, v1) + _dot(tri_full, v2)
    cs_blk = _dot(tri_blk, v0) + _dot(tri_blk, v1) + _dot(tri_blk, v2)
    cum = cs_full + carry_ref[...]
    carry_ref[...] = cum[ts - 1:ts, :]
    out = jnp.where(lane < NH, sig, jnp.where(is_g, cs_blk, jnp.where(is_f, cum, 0.0)))
    col_ref[...] = out
    out_t = out.T
    rowf_ref[...] = out_t[L_F:L_F + NH, :]
    for ci in range(ts // CHUNK):
        rowg_ref[ci] = out_t[0:32, ci * CHUNK:(ci + 1) * CHUNK]


def _prep(small, par, bsz, seq):
    m = small.shape[0]
    ts = 256
    nt = seq // ts
    return pl.pallas_call(
        functools.partial(_prep_kernel, ts=ts),
        grid=(bsz, nt),
        in_specs=[pl.BlockSpec((ts, LANES), lambda b, j: (b * nt + j, 0)),
                  pl.BlockSpec((8, LANES), lambda b, j: (0, 0))],
        out_specs=[pl.BlockSpec((ts, LANES), lambda b, j: (b * nt + j, 0)),
                   pl.BlockSpec((NH, ts), lambda b, j: (0, b * nt + j)),
                   pl.BlockSpec((ts // CHUNK, 32, CHUNK), lambda b, j: (b * nt + j, 0, 0))],
        out_shape=[jax.ShapeDtypeStruct((m, LANES), F32),
                   jax.ShapeDtypeStruct((NH, m), F32),
                   jax.ShapeDtypeStruct((m // CHUNK, 32, CHUNK), F32)],
        scratch_shapes=[pltpu.VMEM((1, LANES), F32)],
        compiler_params=_cparams(("arbitrary", "arbitrary")), name="prep",
    )(small, par)


HALO = 16


def _bmm(a, b):
    return lax.dot_general(a.astype(BF16), b.astype(BF16), (((2,), (1,)), ((0,), (0,))),
                           preferred_element_type=F32)


def _bmm_nt(a, b):
    return lax.dot_general(a.astype(BF16), b.astype(BF16), (((2,), (2,)), ((0,), (0,))),
                           preferred_element_type=F32)


def _delta_kernel(qkv_ref, halo_ref, z_ref, cw_ref, col_ref, rowg_ref, gain_ref, o_ref, s_ref, *, nc):
    j = pl.program_id(1)

    @pl.when(j == 0)
    def _():
        s_ref[...] = jnp.zeros_like(s_ref)

    ts = nc * CHUNK
    n = NH * nc
    xb = qkv_ref[...]
    halo = halo_ref[...]
    halo = jnp.where(j == 0, jnp.zeros_like(halo), halo)
    xx = jnp.concatenate([halo, xb], axis=0)
    rr = lax.broadcasted_iota(jnp.int32, (3 * ts, HALO + ts), 0)
    cc = lax.broadcasted_iota(jnp.int32, (3 * ts, HALO + ts), 1)
    sel = jnp.where(cc == (rr % ts) + HALO - 1 - rr // ts, 1.0, 0.0).astype(BF16)
    shifted = _dot(sel, xx)
    cw = cw_ref[...]
    acc = xb.astype(F32) * cw[3:4, :]
    for dlt in (1, 2, 3):
        acc = acc + shifted[(dlt - 1) * ts:dlt * ts, :] * cw[3 - dlt:4 - dlt, :]
    act = acc * _sigmoid(acc)

    cv = col_ref[...]
    rg = rowg_ref[...]
    qs, ks, vs, betas, gcs, grs = [], [], [], [], [], []
    for h in range(NH):
        qh = act[:, h * HD:(h + 1) * HD]
        kh = act[:, BW + h * HD:BW + (h + 1) * HD]
        vh = act[:, 2 * BW + h * HD:2 * BW + (h + 1) * HD]
        qh = qh * (lax.rsqrt(jnp.sum(qh * qh, axis=-1, keepdims=True) + EPS) * (HD ** -0.5))
        kh = kh * lax.rsqrt(jnp.sum(kh * kh, axis=-1, keepdims=True) + EPS)
        qs.append(qh.reshape(nc, CHUNK, HD))
        ks.append(kh.reshape(nc, CHUNK, HD))
        vs.append(vh.reshape(nc, CHUNK, HD))
        betas.append(cv[:, L_BETA + h:L_BETA + h + 1].reshape(nc, CHUNK, 1))
        gcs.append(cv[:, L_G + h:L_G + h + 1].reshape(nc, CHUNK, 1))
        grs.append(rg[:, L_G + h:L_G + h + 1, :])
    q = jnp.concatenate(qs, axis=0)
    k = jnp.concatenate(ks, axis=0)
    v = jnp.concatenate(vs, axis=0)
    beta = jnp.concatenate(betas, axis=0)
    gc = jnp.concatenate(gcs, axis=0)
    gr = jnp.concatenate(grs, axis=0)

    ri = lax.broadcasted_iota(jnp.int32, (CHUNK, CHUNK), 0)
    ci = lax.broadcasted_iota(jnp.int32, (CHUNK, CHUNK), 1)
    causal = (ci <= ri)[None]
    strict = (ci < ri)[None]
    decay = jnp.exp(jnp.where(causal, gc - gr, -jnp.inf))
    kb = k * beta
    both = _bmm_nt(jnp.concatenate([kb, q], axis=1), k)
    lower = jnp.where(strict, both[:, :CHUNK] * decay, 0.0)
    qk = both[:, CHUNK:] * decay

    eye = jnp.where(ci == ri, 1.0, 0.0)[None]
    t_inv = eye - lower
    pw = _bmm(lower, lower)
    for it in range(5):
        t_inv = t_inv + _bmm(t_inv, pw)
        if it < 4:
            pw = _bmm(pw, pw)

    eg = jnp.exp(gc)
    rhs = jnp.concatenate([v * beta, kb * eg], axis=-1)
    sol = _bmm(t_inv, rhs)
    u = sol[:, :, :HD]
    w = sol[:, :, HD:]
    q_dec = q * eg
    g_end = gc[:, CHUNK - 1:CHUNK, :]
    k_dec = k * jnp.exp(g_end - gc)
    g_last = jnp.exp(g_end)

    def pick(t, c):
        return t.reshape((NH, nc) + t.shape[1:])[:, c]

    state = s_ref[...]
    outs = []
    for c in range(nc):
        ws_qs = _bmm(jnp.concatenate([pick(w, c), pick(q_dec, c)], axis=1), state)
        v_new = pick(u, c) - ws_qs[:, :CHUNK]
        o_c = ws_qs[:, CHUNK:] + _bmm(pick(qk, c), v_new)
        kd_t = jnp.swapaxes(pick(k_dec, c), 1, 2)
        state = state * pick(g_last, c) + _bmm(kd_t, v_new)
        outs.append(o_c)
    s_ref[...] = state

    gain = gain_ref[...]
    zf = z_ref[...].astype(F32)
    cols = []
    for h in range(NH):
        oh = jnp.concatenate([outs[c][h] for c in range(nc)], axis=0)
        oh = oh * lax.rsqrt(jnp.mean(oh * oh, axis=-1, keepdims=True) + EPS) * gain
        zh = zf[:, h * HD:(h + 1) * HD]
        cols.append(oh * (zh * _sigmoid(zh)))
    o_ref[...] = jnp.concatenate(cols, axis=1).astype(BF16)


def _delta(proj, conv_w8, col, rowg, gain, bsz, seq):
    m = proj.shape[0]
    nc = 4
    ts = nc * CHUNK
    nt = seq // ts
    qkv_blk = P_DN // (3 * BW)
    return pl.pallas_call(
        functools.partial(_delta_kernel, nc=nc),
        grid=(bsz, nt),
        in_specs=[pl.BlockSpec((ts, 3 * BW), lambda b, j: (b * nt + j, qkv_blk)),
                  pl.BlockSpec((HALO, 3 * BW), lambda b, j: (jnp.maximum((b * nt + j) * (ts // HALO) - 1, 0), qkv_blk)),
                  pl.BlockSpec((ts, BW), lambda b, j: (b * nt + j, P_DNZ // BW)),
                  pl.BlockSpec((8, 3 * BW), lambda b, j: (0, 0)),
                  pl.BlockSpec((ts, LANES), lambda b, j: (b * nt + j, 0)),
                  pl.BlockSpec((nc, 32, CHUNK), lambda b, j: (b * nt + j, 0, 0)),
                  pl.BlockSpec((1, HD), lambda b, j: (0, 0))],
        out_specs=pl.BlockSpec((ts, BW), lambda b, j: (b * nt + j, 0)),
        out_shape=jax.ShapeDtypeStruct((m, BW), BF16),
        scratch_shapes=[pltpu.VMEM((NH, HD, HD), F32)],
        compiler_params=_cparams(("arbitrary", "arbitrary")), name="delta",
    )(proj, proj, proj, conv_w8, col, rowg, gain.reshape(1, HD))


FOX_T = 512
FOX_HPS = 4


def _fox_kernel(q_ref, k_ref, v_ref, crow_ref, o_ref, vx_ref, m_ref, acc_ref):
    t = FOX_T
    hp = pl.program_id(1)
    qi = pl.program_id(2)

    @pl.when(qi == 0)
    def _():
        lane = lax.broadcasted_iota(jnp.int32, (v_ref.shape[0], HD), 1)
        ones_col = jnp.where(lane == 0, 1.0, 0.0).astype(BF16)
        for e in range(FOX_HPS):
            vx_ref[e, :, :HD] = v_ref[:, e * HD:(e + 1) * HD]
            vx_ref[e, :, HD:] = ones_col

    m_ref[...] = jnp.full_like(m_ref, -jnp.inf)
    acc_ref[...] = jnp.zeros_like(acc_ref)

    def block(ki, masked):
        off = pl.multiple_of(ki * t, t)
        for e in range(FOX_HPS):
            s = lax.dot_general(q_ref[:, e * HD:(e + 1) * HD], k_ref[pl.ds(off, t), e * HD:(e + 1) * HD],
                                (((1,), (1,)), ((), ())), preferred_element_type=F32)
            s = s - crow_ref[pl.ds(FOX_HPS * hp + e, 1), pl.ds(off, t)]
            if masked:
                row = lax.broadcasted_iota(jnp.int32, (t, t), 0)
                col = lax.broadcasted_iota(jnp.int32, (t, t), 1)
                s = jnp.where(col <= row, s, -jnp.inf)
            m_prev = m_ref[e]
            m_new = jnp.maximum(m_prev, jnp.max(s, axis=1, keepdims=True))
            alpha = jnp.exp(m_prev - m_new)
            p = jnp.exp((s - jnp.tile(m_new, (1, t // LANES))).astype(BF16))
            acc_ref[e] = jnp.tile(alpha, (1, 2)) * acc_ref[e] + _dot(p, vx_ref[e, pl.ds(off, t), :])
            m_ref[e] = m_new

    def body(ki, carry):
        block(ki, False)
        return carry

    lax.fori_loop(0, qi, body, 0)
    block(qi, True)
    for e in range(FOX_HPS):
        o_ref[:, e * HD:(e + 1) * HD] = (acc_ref[e, :, :HD] / acc_ref[e, :, HD:HD + 1]).astype(BF16)


def _fox(proj, rowf, bsz, seq):
    m = proj.shape[0]
    t = FOX_T
    nq = seq // t
    w = FOX_HPS * HD
    return pl.pallas_call(
        _fox_kernel,
        grid=(bsz, NH // FOX_HPS, nq),
        in_specs=[pl.BlockSpec((t, w), lambda b, h, qi: (b * nq + qi, P_FQ // w + h)),
                  pl.BlockSpec((seq, w), lambda b, h, qi: (b, P_FK // w + h)),
                  pl.BlockSpec((seq, w), lambda b, h, qi: (b, P_FV // w + h)),
                  pl.BlockSpec((NH, seq), lambda b, h, qi: (0, b))],
        out_specs=pl.BlockSpec((t, w), lambda b, h, qi: (b * nq + qi, h)),
        out_shape=jax.ShapeDtypeStruct((m, BW), BF16),
        scratch_shapes=[pltpu.VMEM((FOX_HPS, seq, 2 * HD), BF16), pltpu.VMEM((FOX_HPS, t, HD), F32),
                        pltpu.VMEM((FOX_HPS, t, 2 * HD), F32)],
        compiler_params=_cparams(("arbitrary", "arbitrary", "arbitrary")), name="fox",
    )(proj, proj, proj, rowf)


def _swap_halves(x):
    w = x.shape[-1]
    lane = lax.broadcasted_iota(jnp.int32, x.shape, x.ndim - 1)
    return jnp.where((lane % SWA_D) < SWA_D // 2, pltpu.roll(x, w - SWA_D // 2, axis=x.ndim - 1),
                     pltpu.roll(x, SWA_D // 2, axis=x.ndim - 1))


def _swa_kernel(sink_ref, q_ref, kc_ref, kp_ref, vc_ref, vp_ref, cc_ref, sc_ref, cp_ref, sp_ref, o_ref, *, nblk):
    i = pl.program_id(0)
    first = (i % nblk) == 0
    cos_c, sin_c = cc_ref[...], sc_ref[...]
    q = q_ref[...].astype(F32)
    q = q * jnp.tile(cos_c, (1, SWA_QH // 2)) + _swap_halves(q) * jnp.tile(sin_c, (1, SWA_QH // 2))
    kc = kc_ref[...].astype(F32)
    kc = kc * cos_c + _swap_halves(kc) * sin_c
    kp = kp_ref[...].astype(F32)
    kp = kp * cp_ref[...] + _swap_halves(kp) * sp_ref[...]
    kk = jnp.concatenate([kp, kc], axis=0).astype(BF16)
    vv = jnp.concatenate([vp_ref[...], vc_ref[...]], axis=0)
    r = lax.broadcasted_iota(jnp.int32, (WIN, 2 * WIN), 0)
    c = lax.broadcasted_iota(jnp.int32, (WIN, 2 * WIN), 1)
    mask = (c > r) & (c <= r + WIN) & ((c >= WIN) | jnp.logical_not(first))
    qb = q.astype(BF16)
    outs = []
    for hq in range(SWA_QH):
        g = hq // SWA_G
        qh = qb[:, hq * SWA_D:(hq + 1) * SWA_D]
        kh = kk[:, g * SWA_D:(g + 1) * SWA_D]
        vh = vv[:, g * SWA_D:(g + 1) * SWA_D]
        s = lax.dot_general(qh, kh, (((1,), (1,)), ((), ())), preferred_element_type=F32)
        s = jnp.where(mask, s, -jnp.inf)
        sink = sink_ref[hq]
        mx = jnp.maximum(jnp.max(s, axis=1, keepdims=True), sink)
        p = jnp.exp(s - mx)
        den = jnp.sum(p, axis=1, keepdims=True) + jnp.exp(sink - mx)
        outs.append(_dot(p.astype(BF16), vh) / den)
    o_ref[...] = jnp.concatenate(outs, axis=1).astype(BF16)


def _swa(proj, sinks, cos4, sin4, bsz, seq):
    m = proj.shape[0]
    nblk = seq // WIN
    prev = lambda i: jnp.maximum(i - 1, 0)
    kcol, vcol = P_SK // LANES, P_SV // LANES
    return pl.pallas_call(
        functools.partial(_swa_kernel, nblk=nblk),
        grid=(m // WIN,),
        in_specs=[pl.BlockSpec(memory_space=pltpu.SMEM),
                  pl.BlockSpec((WIN, BW), lambda i: (i, P_SQ // BW)),
                  pl.BlockSpec((WIN, LANES), lambda i: (i, kcol)),
                  pl.BlockSpec((WIN, LANES), lambda i: (prev(i), kcol)),
                  pl.BlockSpec((WIN, LANES), lambda i: (i, vcol)),
                  pl.BlockSpec((WIN, LANES), lambda i: (prev(i), vcol)),
                  pl.BlockSpec((WIN, LANES), lambda i: (i, 0)),
                  pl.BlockSpec((WIN, LANES), lambda i: (i, 0)),
                  pl.BlockSpec((WIN, LANES), lambda i: (prev(i), 0)),
                  pl.BlockSpec((WIN, LANES), lambda i: (prev(i), 0))],
        out_specs=pl.BlockSpec((WIN, BW), lambda i: (i, 0)),
        out_shape=jax.ShapeDtypeStruct((m, BW), BF16),
        compiler_params=_cparams(("arbitrary",)), name="swa",
    )(sinks, proj, proj, proj, proj, proj, cos4, sin4, cos4, sin4)


def _merge_kernel(oa_ref, ob_ref, oc_ref, ga_ref, gb_ref, gc_ref, w_ref, o_ref):
    acc = _sigmoid(ga_ref[...].astype(F32)) * _dot(oa_ref[...], w_ref[0])
    acc = acc + _sigmoid(gb_ref[...].astype(F32)) * _dot(ob_ref[...], w_ref[1])
    acc = acc + _sigmoid(gc_ref[...].astype(F32)) * _dot(oc_ref[...], w_ref[2])
    o_ref[...] = acc.astype(BF16)


def _merge(o_a, o_b, o_c, proj, w_branch):
    m = o_a.shape[0]
    tm, tn = 1024, 512
    gblk = lambda b: (lambda j, i: (i, (P_GATE + b * D) // tn + j))
    oblk = pl.BlockSpec((tm, BW), lambda j, i: (i, 0))
    return pl.pallas_call(
        _merge_kernel, grid=(D // tn, m // tm),
        in_specs=[oblk, oblk, oblk,
                  pl.BlockSpec((tm, tn), gblk(0)), pl.BlockSpec((tm, tn), gblk(1)), pl.BlockSpec((tm, tn), gblk(2)),
                  pl.BlockSpec((3, BW, tn), lambda j, i: (0, 0, j))],
        out_specs=pl.BlockSpec((tm, tn), lambda j, i: (i, j)),
        out_shape=jax.ShapeDtypeStruct((m, D), BF16),
        compiler_params=_cparams(("arbitrary", "arbitrary")), name="merge",
    )(o_a, o_b, o_c, proj, proj, proj, w_branch)


def _swiglu_acc(h, wg_ref, wu_ref, wd_ref, acc_ref):
    a = _dot(h, wg_ref[...])
    b = _dot(h, wu_ref[...])
    t = (a * _sigmoid(a) * b).astype(BF16)
    acc_ref[...] += _dot(t, wd_ref[...])


def _ffn_kernel(h_ref, wg_ref, wu_ref, wd_ref, x_ref, g_ref, o_ref, acc_ref, *, nf):
    f = pl.program_id(1)

    @pl.when(f == 0)
    def _():
        acc_ref[...] = jnp.zeros_like(acc_ref)

    _swiglu_acc(h_ref[...], wg_ref, wu_ref, wd_ref, acc_ref)

    @pl.when(f == nf - 1)
    def _():
        o_ref[...] = x_ref[...] + g_ref[...] * acc_ref[...]


def _ffn_dense(h, wg, wu, wd, x, gate, seq):
    m, d = h.shape
    ff = wg.shape[1]
    tm, tf = 512, 1024
    nf = ff // tf
    tpb = seq // tm
    return pl.pallas_call(
        functools.partial(_ffn_kernel, nf=nf), grid=(m // tm, nf),
        in_specs=[pl.BlockSpec((tm, d), lambda i, f: (i, 0)),
                  pl.BlockSpec((d, tf), lambda i, f: (0, f)),
                  pl.BlockSpec((d, tf), lambda i, f: (0, f)),
                  pl.BlockSpec((tf, d), lambda i, f: (f, 0)),
                  pl.BlockSpec((tm, d), lambda i, f: (i, 0)),
                  pl.BlockSpec((None, 1, d), lambda i, f: (i // tpb, 0, 0))],
        out_specs=pl.BlockSpec((tm, d), lambda i, f: (i, 0)),
        out_shape=jax.ShapeDtypeStruct((m, d), F32),
        scratch_shapes=[pltpu.VMEM((tm, d), F32)],
        compiler_params=_cparams(("arbitrary", "arbitrary")), name="ffn_dense",
    )(h, wg, wu, wd, x, gate)


def _row_copy(src_hbm, idx_ref, pos, dst, r, sem):
    return pltpu.make_async_copy(src_hbm.at[pl.ds(idx_ref[pos], 1), :], dst.at[pl.ds(r, 1), :], sem)


def _start_row_gather(src_hbm, idx_ref, base, n, dst, sem):
    def body(r, carry):
        _row_copy(src_hbm, idx_ref, base + r, dst, r, sem).start()
        return carry

    lax.fori_loop(0, n, body, 0, unroll=8)


def _wait_row_gather(src_hbm, n, dst, sem):
    pltpu.make_async_copy(src_hbm.at[pl.ds(0, n), :], dst, sem).wait()


def _ffn_group_kernel(te_ref, tv_ref, src_ref, h_hbm, wg_ref, wu_ref, wd_ref, o_ref,
                      buf, hbf_ref, acc_ref, sem, *, nf, tm, n_tiles):
    i = pl.program_id(0)
    f = pl.program_id(1)
    slot = i % 2
    live = tv_ref[i] > 0
    per = -(-tm // nf)
    last = tm - per * (nf - 1)

    @pl.when(f == 0)
    def _():
        @pl.when(i == 0)
        def _():
            _start_row_gather(h_hbm, src_ref, 0, tm, buf.at[0], sem.at[0])

        _wait_row_gather(h_hbm, tm, buf.at[slot], sem.at[slot])
        hbf_ref[...] = buf[slot].astype(BF16)
        acc_ref[...] = jnp.zeros_like(acc_ref)

    def prefetch(first, count):
        for j in range(count):
            r = first + j
            _row_copy(h_hbm, src_ref, (i + 1) * tm + r, buf.at[1 - slot], r, sem.at[1 - slot]).start()

    def step(count, compute):
        c0 = count // 3
        c1 = (count - c0) // 2
        first = f * per
        if not compute:
            prefetch(first, count)
            return
        h = hbf_ref[...]
        a = _dot(h, wg_ref[...])
        prefetch(first, c0)
        b = _dot(h, wu_ref[...])
        prefetch(first + c0, c1)
        t = (a * _sigmoid(a) * b).astype(BF16)
        acc_ref[...] += _dot(t, wd_ref[...])
        prefetch(first + c0 + c1, count - c0 - c1)

    for is_last, count in ((False, per), (True, last)):
        for compute in (True, False):
            cond = (f == nf - 1) if is_last else (f < nf - 1)
            cond = cond & (live if compute else jnp.logical_not(live))
            pl.when(cond)(functools.partial(step, count, compute))

    @pl.when(f == nf - 1)
    def _():
        o_ref[...] = acc_ref[...]

        @pl.when(i == n_tiles - 1)
        def _():
            _wait_row_gather(h_hbm, tm, buf.at[1 - slot], sem.at[1 - slot])


def _ffn_grouped(h, wg, wu, wd, tile_expert, tile_valid, src_rows, tm):
    rows = src_rows.shape[0] - tm
    d, ff = wg.shape[1], wg.shape[2]
    tf = 1024
    nf = ff // tf
    n_tiles = rows // tm

    def fsel(i, f, tv):
        return jnp.where(tv[i] > 0, f, nf - 1)

    grid_spec = pltpu.PrefetchScalarGridSpec(
        num_scalar_prefetch=3, grid=(n_tiles, nf),
        in_specs=[pl.BlockSpec(memory_space=pl.ANY),
                  pl.BlockSpec((None, d, tf), lambda i, f, te, tv, sr: (te[i], 0, fsel(i, f, tv))),
                  pl.BlockSpec((None, d, tf), lambda i, f, te, tv, sr: (te[i], 0, fsel(i, f, tv))),
                  pl.BlockSpec((None, tf, d), lambda i, f, te, tv, sr: (te[i], fsel(i, f, tv), 0))],
        out_specs=pl.BlockSpec((tm, d), lambda i, f, te, tv, sr: (i, 0)),
        scratch_shapes=[pltpu.VMEM((2, tm, d), F32),
                        pltpu.VMEM((tm, d), BF16),
                        pltpu.VMEM((tm, d), F32),
                        pltpu.SemaphoreType.DMA((2,))])
    return pl.pallas_call(
        functools.partial(_ffn_group_kernel, nf=nf, tm=tm, n_tiles=n_tiles), grid_spec=grid_spec,
        out_shape=jax.ShapeDtypeStruct((rows, d), F32),
        compiler_params=_cparams(("arbitrary", "arbitrary")), name="ffn_grouped",
    )(tile_expert, tile_valid, src_rows, h, wg, wu, wd)


COMBINE_TM = 256


def _combine_kernel(pos_ref, x_ref, y_hbm, r_ref, g_ref, gain_ref, o_ref, buf, sem, *, tm, nt, m, final):
    i = pl.program_id(0)
    slot = i % 2

    def start(tile, s):
        for k in range(2):
            _start_row_gather(y_hbm, pos_ref, k * m + tile * tm, tm, buf.at[s, k], sem.at[s])

    @pl.when(i == 0)
    def _():
        start(0, 0)

    @pl.when(i + 1 < nt)
    def _():
        start(jnp.minimum(i + 1, nt - 1), 1 - slot)

    for k in range(2):
        _wait_row_gather(y_hbm, tm, buf.at[slot, k], sem.at[slot])
    w0 = r_ref[:, 2:3]
    w1 = r_ref[:, 3:4]
    xn = x_ref[...] + g_ref[...] * (w0 * buf[slot, 0] + w1 * buf[slot, 1])
    if final:
        xn = xn * lax.rsqrt(jnp.mean(xn * xn, axis=-1, keepdims=True) + EPS) * gain_ref[...]
    o_ref[...] = xn


def _combine(x, y, pos_rows, route, gate, seq, final_gain=None):
    m, d = x.shape
    tm = COMBINE_TM
    nt = m // tm
    tpb = seq // tm
    final = final_gain is not None
    gain = (final_gain if final else jnp.ones((d,), F32)).reshape(1, d)
    grid_spec = pltpu.PrefetchScalarGridSpec(
        num_scalar_prefetch=1, grid=(nt,),
        in_specs=[pl.BlockSpec((tm, d), lambda i, p: (i, 0)),
                  pl.BlockSpec(memory_space=pl.ANY),
                  pl.BlockSpec((tm, LANES), lambda i, p: (i, 0)),
                  pl.BlockSpec((None, 1, d), lambda i, p: (i // tpb, 0, 0)),
                  pl.BlockSpec((1, d), lambda i, p: (0, 0))],
        out_specs=pl.BlockSpec((tm, d), lambda i, p: (i, 0)),
        scratch_shapes=[pltpu.VMEM((2, 2, tm, d), F32), pltpu.SemaphoreType.DMA((2,))])
    return pl.pallas_call(
        functools.partial(_combine_kernel, tm=tm, nt=nt, m=m, final=final), grid_spec=grid_spec,
        out_shape=jax.ShapeDtypeStruct((m, d), F32),
        compiler_params=_cparams(("arbitrary",)), name="combine",
    )(pos_rows, x, y, route, gate, gain)


def _final_norm_kernel(x_ref, gain_ref, o_ref):
    x = x_ref[...]
    o_ref[...] = x * lax.rsqrt(jnp.mean(x * x, axis=-1, keepdims=True) + EPS) * gain_ref[...]


def _final_norm(x, gain):
    m, d = x.shape
    tm = 512
    blk = pl.BlockSpec((tm, d), lambda i: (i, 0))
    return pl.pallas_call(
        _final_norm_kernel, grid=(m // tm,),
        in_specs=[blk, pl.BlockSpec((1, d), lambda i: (0, 0))],
        out_specs=blk, out_shape=jax.ShapeDtypeStruct((m, d), F32),
        compiler_params=_cparams(("arbitrary",)), name="final_norm",
    )(x, gain.reshape(1, d))


MOE_TM = 512


def _moe(x, h, route, counts, wg, wu, wd, gate, seq, final_gain=None):
    m, d = x.shape
    tm = MOE_TM
    rows = 2 * m + NE * tm
    n_tiles = rows // tm
    e_flat = route[:, 0:2].astype(jnp.int32).reshape(-1)
    rank = route[:, 4:6].astype(jnp.int32).reshape(-1)
    counts = counts[0, :NE].astype(jnp.int32)
    padded = ((counts + tm - 1) // tm) * tm
    ends = jnp.cumsum(padded)
    starts = ends - padded
    pos = jnp.take(starts, e_flat) + rank
    src_token = jnp.zeros((rows + tm,), jnp.int32).at[pos].set(jnp.arange(2 * m, dtype=jnp.int32) // 2)
    tile_start = jnp.arange(n_tiles, dtype=jnp.int32) * tm
    tile_expert = jnp.minimum(jnp.sum((tile_start[:, None] >= ends[None, :]).astype(jnp.int32), axis=1), NE - 1)
    tile_valid = (tile_start < ends[-1]).astype(jnp.int32)

    ys = _ffn_grouped(h, wg, wu, wd, tile_expert, tile_valid, src_token, tm)
    pos_rows = pos.reshape(m, 2).T.reshape(-1)
    return _combine(x, ys, pos_rows, route, gate, seq, final_gain)


REGROUP_TN = 256
REGROUP_SEGS = ((P_GATE, 8472, 6144), (P_DN, 0, 4096), (P_FQ, 4112, 3072), (P_SQ, 7192, 1280))


def _regroup_kernel(blk_ref, shift_ref, scale_ref, a_ref, b_ref, c_ref, o_ref):
    j = pl.program_id(0)
    win = jnp.concatenate([a_ref[...], b_ref[...], c_ref[...]], axis=1)
    width = REGROUP_TN + LANES
    win = pltpu.roll(win, width - shift_ref[j], axis=1)
    o_ref[...] = (win[:, :REGROUP_TN] * scale_ref[j]).astype(BF16)


def _regroup_w_in(w_in, layer):
    d = w_in.shape[1]
    tn = REGROUP_TN
    blk, shift, scale = [], [], []
    for dst, src, width in REGROUP_SEGS:
        for t in range(width // tn):
            s = src + t * tn
            blk.append(s // LANES)
            shift.append(s % LANES)
            o = dst + t * tn
            scale.append(HD ** -0.5 if P_FQ <= o < P_FK else SWA_D ** -0.5 if P_SQ <= o < P_SK else 1.0)
    n_tiles = len(blk)
    assert n_tiles * tn == NP
    grid_spec = pltpu.PrefetchScalarGridSpec(
        num_scalar_prefetch=3, grid=(n_tiles,),
        in_specs=[pl.BlockSpec((None, d, LANES), lambda j, bk, sh, sc, k=k: (layer, 0, bk[j] + k)) for k in range(3)],
        out_specs=pl.BlockSpec((d, tn), lambda j, bk, sh, sc: (0, j)))
    return pl.pallas_call(
        _regroup_kernel, grid_spec=grid_spec,
        out_shape=jax.ShapeDtypeStruct((d, NP), BF16),
        compiler_params=_cparams(("arbitrary",)), name="regroup_w_in",
    )(jnp.asarray(blk, jnp.int32), jnp.asarray(shift, jnp.int32), jnp.asarray(scale, F32), w_in, w_in, w_in)


def _small_w_in(w):
    small = jnp.concatenate([w[:, 4096:4112], w[:, 7184:7192]], axis=1)
    return jnp.pad(small, ((0, 0), (0, LANES - small.shape[1]))).astype(BF16)


def kernel(x, c, positions, w_ada, b_ada, norm_mix, w_in, conv_w, dn_a_log, dn_dt_bias, dn_norm,
           fox_b_forget, swa_sinks, w_branch, w_out, norm_ffn, ffn_w_gate, ffn_w_up, ffn_w_down,
           moe_router, moe_w_gate, moe_w_up, moe_w_down, final_norm):
    bsz, seq, d = x.shape
    depth = w_ada.shape[0]
    m = bsz * seq
    xf = x.reshape(m, d)

    inv_freq = 10000.0 ** (-jnp.arange(0, SWA_D, 2, dtype=F32) / SWA_D)
    ang = positions.astype(F32).reshape(m, 1) * inv_freq[None, :]
    cos, sin = jnp.cos(ang), jnp.sin(ang)
    cos4 = jnp.tile(cos, (1, 4))
    sin4 = jnp.tile(jnp.concatenate([-sin, sin], axis=1), (1, 2))

    c_pad = jnp.pad(c, ((0, 8 - bsz), (0, 0)))
    mod = _ada(c_pad, w_ada, b_ada)[:, :bsz].reshape(depth, bsz, 6, 1, d)

    for layer in range(depth):
        sh1, sc1, g1, sh2, sc2, g2 = (mod[layer, :, t] for t in range(6))
        w_main = _regroup_w_in(w_in, layer)
        w_small = _small_w_in(w_in[layer])
        h = _norm_modulate(xf, norm_mix[layer], sc1, sh1, seq)
        proj = _matmul(h, w_main, BF16, 1024, 2432, "in_proj")
        small = _matmul(h, w_small, F32, 1024, LANES, "in_proj_small")
        par = jnp.zeros((8, LANES), F32)
        par = par.at[0, L_G:L_G + NH].set(dn_a_log[layer])
        par = par.at[1, L_G:L_G + NH].set(dn_dt_bias[layer])
        par = par.at[1, L_F:L_F + NH].set(fox_b_forget[layer])
        col, rowf, rowg = _prep(small, par, bsz, seq)
        conv_w8 = jnp.pad(conv_w[layer], ((0, 4), (0, 0)))
        o_a = _delta(proj, conv_w8, col, rowg, dn_norm[layer], bsz, seq)
        o_b = _fox(proj, rowf, bsz, seq)
        o_c = _swa(proj, swa_sinks[layer], cos4, sin4, bsz, seq)
        merged = _merge(o_a, o_b, o_c, proj, w_branch[layer].astype(BF16))
        xf = _matmul_resid(merged, w_out[layer].astype(BF16), xf, g1, seq, 512, 2048, "out_proj")
        if layer % 2 == 0:
            i = layer // 2
            h2 = _norm_modulate(xf, norm_ffn[layer], sc2, sh2, seq)
            xf = _ffn_dense(h2, ffn_w_gate[i].astype(BF16), ffn_w_up[i].astype(BF16),
                            ffn_w_down[i].astype(BF16), xf, g2, seq)
        else:
            i = layer // 2
            h2, route, counts = _norm_modulate(xf, norm_ffn[layer], sc2, sh2, seq, w_router=moe_router[i])
            last = layer == depth - 1
            xf = _moe(xf, h2, route, counts, moe_w_gate[i].astype(BF16), moe_w_up[i].astype(BF16),
                      moe_w_down[i].astype(BF16), g2, seq, final_gain=final_norm if last else None)
            if last:
                return xf.reshape(bsz, seq, d)
    return _final_norm(xf, final_norm).reshape(bsz, seq, d)
```

```python
import functools

import jax
import jax.numpy as jnp
from jax import lax
from jax.experimental import pallas as pl
from jax.experimental.pallas import tpu as pltpu

F32 = jnp.float32
BF16 = jnp.bfloat16
EPS = 1e-6

D = 2048
HD = 128
NH = 8
BW = 1024
CHUNK = 64
SWA_D = 64
SWA_QH = 16
SWA_KVH = 2
SWA_G = 8
WIN = 128
D_FF = 7168
NE = 8
LANES = 128

P_GATE = 0
P_DN = 6144
P_DNZ = 9216
P_FQ, P_FK, P_FV = 10240, 11264, 12288
P_SQ, P_SK, P_SV = 13312, 14336, 14464
NP = 14592
L_BETA, L_G, L_F = 0, 8, 16

VMEM_LIMIT = 56 * 1024 * 1024


def _cparams(sem):
    return pltpu.CompilerParams(dimension_semantics=sem, vmem_limit_bytes=VMEM_LIMIT)


def _sigmoid(x):
    return 1.0 / (1.0 + jnp.exp(-x))


def _split3(x):
    hi = x.astype(BF16)
    r = x - hi.astype(F32)
    mid = r.astype(BF16)
    lo = (r - mid.astype(F32)).astype(BF16)
    return hi, mid, lo


def _dot(a, b):
    return jnp.dot(a, b, preferred_element_type=F32)


def _dot_hi(a, b):
    a0, a1, a2 = _split3(a)
    b0, b1, b2 = _split3(b)
    return (_dot(a0, b0) + (_dot(a0, b1) + _dot(a1, b0))
            + (_dot(a0, b2) + _dot(a1, b1) + _dot(a2, b0)))


def _ada_kernel(c_ref, w_ref, b_ref, o_ref):
    c = c_ref[...]
    ca = c * _sigmoid(c)
    a0 = ca.astype(BF16)
    a1 = (ca - a0.astype(F32)).astype(BF16)
    w = w_ref[...]
    w0 = w.astype(BF16)
    w1 = (w - w0.astype(F32)).astype(BF16)
    o_ref[...] = _dot(a0, w0) + (_dot(a0, w1) + _dot(a1, w0)) + b_ref[...]


def _ada(c_pad, w_ada, b_ada):
    depth, d, n = w_ada.shape
    tn = 1536
    return pl.pallas_call(
        _ada_kernel,
        grid=(depth, n // tn),
        in_specs=[pl.BlockSpec((8, d), lambda l, j: (0, 0)),
                  pl.BlockSpec((None, d, tn), lambda l, j: (l, 0, j)),
                  pl.BlockSpec((None, 1, tn), lambda l, j: (l, 0, j))],
        out_specs=pl.BlockSpec((None, 8, tn), lambda l, j: (l, 0, j)),
        out_shape=jax.ShapeDtypeStruct((depth, 8, n), F32),
        compiler_params=_cparams(("arbitrary", "arbitrary")),
        name="ada",
    )(c_pad, w_ada, b_ada.reshape(depth, 1, n))


def _normmod(x_ref, gain_ref, sc_ref, sh_ref):
    x = x_ref[...]
    ms = jnp.mean(x * x, axis=-1, keepdims=True)
    y = x * lax.rsqrt(ms + EPS) * gain_ref[...]
    return y * (1.0 + sc_ref[...]) + sh_ref[...]


def _normmod_kernel(x_ref, gain_ref, sc_ref, sh_ref, h_ref):
    h_ref[...] = _normmod(x_ref, gain_ref, sc_ref, sh_ref).astype(BF16)


def _normmod_small_kernel(x_ref, gain_ref, sc_ref, sh_ref, ws_ref, h_ref, s_ref):
    h = _normmod(x_ref, gain_ref, sc_ref, sh_ref).astype(BF16)
    h_ref[...] = h
    s_ref[...] = _dot(h, ws_ref[...])


def _normmod_route_kernel(x_ref, gain_ref, sc_ref, sh_ref, wr_ref, h_ref, r_ref, cnt_ref, carry_ref):
    @pl.when(pl.program_id(0) == 0)
    def _():
        carry_ref[...] = jnp.zeros_like(carry_ref)

    h = _normmod(x_ref, gain_ref, sc_ref, sh_ref)
    h_ref[...] = h
    logits = _dot_hi(h, wr_ref[...])
    lane = lax.broadcasted_iota(jnp.int32, logits.shape, 1)
    neg = jnp.float32(-jnp.inf)
    l1 = jnp.where(lane < NE, logits, neg)
    m1 = jnp.max(l1, axis=-1, keepdims=True)
    i1 = jnp.min(jnp.where(l1 == m1, lane, LANES), axis=-1, keepdims=True)
    l2 = jnp.where(lane == i1, neg, l1)
    m2 = jnp.max(l2, axis=-1, keepdims=True)
    i2 = jnp.min(jnp.where(l2 == m2, lane, LANES), axis=-1, keepdims=True)
    e = jnp.exp(m2 - m1)
    w1 = 1.0 / (1.0 + e)
    w2 = e / (1.0 + e)
    tm = logits.shape[0]
    picked = jnp.where((lane == i1) | (lane == i2), 1.0, 0.0)
    rr = lax.broadcasted_iota(jnp.int32, (tm, tm), 0)
    cc = lax.broadcasted_iota(jnp.int32, (tm, tm), 1)
    before = _dot(jnp.where(cc < rr, 1.0, 0.0).astype(BF16), picked.astype(BF16)) + carry_ref[...]
    rank1 = jnp.sum(jnp.where(lane == i1, before, 0.0), axis=-1, keepdims=True)
    rank2 = jnp.sum(jnp.where(lane == i2, before, 0.0), axis=-1, keepdims=True)
    carry_ref[...] += jnp.sum(picked, axis=0, keepdims=True)
    cnt_ref[...] = jnp.broadcast_to(carry_ref[...], cnt_ref.shape)
    vals = (i1.astype(F32), i2.astype(F32), w1, w2, rank1, rank2)
    out = jnp.zeros_like(logits)
    for pos, v in enumerate(vals):
        out = jnp.where(lane == pos, v, out)
    r_ref[...] = out


def _norm_modulate(x, gain, sc, sh, seq, w_router=None, w_small=None):
    m, d = x.shape
    tm = 512
    tpb = seq // tm
    in_specs = [pl.BlockSpec((tm, d), lambda i: (i, 0)),
                pl.BlockSpec((1, d), lambda i: (0, 0)),
                pl.BlockSpec((None, 1, d), lambda i: (i // tpb, 0, 0)),
                pl.BlockSpec((None, 1, d), lambda i: (i // tpb, 0, 0))]
    args = [x, gain.reshape(1, d), sc, sh]
    if w_small is not None:
        return pl.pallas_call(
            _normmod_small_kernel, grid=(m // tm,),
            in_specs=in_specs + [pl.BlockSpec((d, LANES), lambda i: (0, 0))],
            out_specs=[pl.BlockSpec((tm, d), lambda i: (i, 0)), pl.BlockSpec((tm, LANES), lambda i: (i, 0))],
            out_shape=[jax.ShapeDtypeStruct((m, d), BF16), jax.ShapeDtypeStruct((m, LANES), F32)],
            compiler_params=_cparams(("arbitrary",)), name="normmod_small",
        )(*args, w_small)
    if w_router is None:
        return pl.pallas_call(
            _normmod_kernel, grid=(m // tm,), in_specs=in_specs,
            out_specs=pl.BlockSpec((tm, d), lambda i: (i, 0)),
            out_shape=jax.ShapeDtypeStruct((m, d), BF16),
            compiler_params=_cparams(("arbitrary",)), name="normmod",
        )(*args)
    wr = jnp.pad(w_router, ((0, 0), (0, LANES - NE)))
    return pl.pallas_call(
        _normmod_route_kernel, grid=(m // tm,),
        in_specs=in_specs + [pl.BlockSpec((d, LANES), lambda i: (0, 0))],
        out_specs=[pl.BlockSpec((tm, d), lambda i: (i, 0)),
                   pl.BlockSpec((tm, LANES), lambda i: (i, 0)),
                   pl.BlockSpec((8, LANES), lambda i: (0, 0))],
        out_shape=[jax.ShapeDtypeStruct((m, d), F32),
                   jax.ShapeDtypeStruct((m, LANES), F32),
                   jax.ShapeDtypeStruct((8, LANES), F32)],
        scratch_shapes=[pltpu.VMEM((1, LANES), F32)],
        compiler_params=_cparams(("arbitrary",)), name="normmod_route",
    )(*args, wr)


def _mm_kernel(a_ref, w_ref, o_ref):
    o_ref[...] = _dot(a_ref[...], w_ref[...]).astype(o_ref.dtype)


def _matmul(a, w, out_dtype, tm, tn, name):
    m, k = a.shape
    n = w.shape[1]
    return pl.pallas_call(
        _mm_kernel, grid=(n // tn, m // tm),
        in_specs=[pl.BlockSpec((tm, k), lambda j, i: (i, 0)),
                  pl.BlockSpec((k, tn), lambda j, i: (0, j))],
        out_specs=pl.BlockSpec((tm, tn), lambda j, i: (i, j)),
        out_shape=jax.ShapeDtypeStruct((m, n), out_dtype),
        compiler_params=_cparams(("arbitrary", "arbitrary")), name=name,
    )(a, w)


def _mm_resid_kernel(a_ref, w_ref, x_ref, g_ref, o_ref):
    o_ref[...] = x_ref[...] + g_ref[...] * _dot(a_ref[...], w_ref[...])


def _matmul_resid(a, w, x, gate, seq, tm, tn, name):
    m, k = a.shape
    n = w.shape[1]
    tpb = seq // tm
    return pl.pallas_call(
        _mm_resid_kernel, grid=(n // tn, m // tm),
        in_specs=[pl.BlockSpec((tm, k), lambda j, i: (i, 0)),
                  pl.BlockSpec((k, tn), lambda j, i: (0, j)),
                  pl.BlockSpec((tm, tn), lambda j, i: (i, j)),
                  pl.BlockSpec((None, 1, tn), lambda j, i: (i // tpb, 0, j))],
        out_specs=pl.BlockSpec((tm, tn), lambda j, i: (i, j)),
        out_shape=jax.ShapeDtypeStruct((m, n), F32),
        compiler_params=_cparams(("arbitrary", "arbitrary")), name=name,
    )(a, w, x, gate)


def _prep_kernel(x_ref, par_ref, col_ref, rowf_ref, rowg_ref, carry_ref, *, ts):
    @pl.when(pl.program_id(1) == 0)
    def _():
        carry_ref[...] = jnp.zeros_like(carry_ref)

    x = x_ref[...]
    lane = lax.broadcasted_iota(jnp.int32, x.shape, 1)
    is_g = (lane >= L_G) & (lane < L_G + NH)
    is_f = (lane >= L_F) & (lane < L_F + NH)
    a_rate = jnp.exp(par_ref[0:1, :])
    z = x + par_ref[1:2, :]
    t = jnp.log1p(jnp.exp(-jnp.abs(z)))
    sig = _sigmoid(x)
    g = -a_rate * (jnp.maximum(z, 0.0) + t)
    logf = jnp.minimum(z, 0.0) - t
    vals = jnp.where(is_g, g, jnp.where(is_f, logf, 0.0))
    r = lax.broadcasted_iota(jnp.int32, (ts, ts), 0)
    c = lax.broadcasted_iota(jnp.int32, (ts, ts), 1)
    tri_full = jnp.where(c <= r, 1.0, 0.0).astype(BF16)
    tri_blk = jnp.where((c <= r) & ((r // CHUNK) == (c // CHUNK)), 1.0, 0.0).astype(BF16)
    v0, v1, v2 = _split3(vals)
    cs_full = _dot(tri_full, v0) + _dot(tri_full, v1) + _dot(tri_full, v2)
    cs_blk = _dot(tri_blk, v0) + _dot(tri_blk, v1) + _dot(tri_blk, v2)
    cum = cs_full + carry_ref[...]
    carry_ref[...] = cum[ts - 1:ts, :]
    out = jnp.where(lane < NH, sig, jnp.where(is_g, cs_blk, jnp.where(is_f, cum, 0.0)))
    col_ref[...] = out
    out_t = out.T
    rowf_ref[...] = out_t[L_F:L_F + NH, :]
    for ci in range(ts // CHUNK):
        rowg_ref[ci] = out_t[0:32, ci * CHUNK:(ci + 1) * CHUNK]


def _prep(small, par, bsz, seq):
    m = small.shape[0]
    ts = 256
    nt = seq // ts
    return pl.pallas_call(
        functools.partial(_prep_kernel, ts=ts),
        grid=(bsz, nt),
        in_specs=[pl.BlockSpec((ts, LANES), lambda b, j: (b * nt + j, 0)),
                  pl.BlockSpec((8, LANES), lambda b, j: (0, 0))],
        out_specs=[pl.BlockSpec((ts, LANES), lambda b, j: (b * nt + j, 0)),
                   pl.BlockSpec((NH, ts), lambda b, j: (0, b * nt + j)),
                   pl.BlockSpec((ts // CHUNK, 32, CHUNK), lambda b, j: (b * nt + j, 0, 0))],
        out_shape=[jax.ShapeDtypeStruct((m, LANES), F32),
                   jax.ShapeDtypeStruct((NH, m), F32),
                   jax.ShapeDtypeStruct((m // CHUNK, 32, CHUNK), F32)],
        scratch_shapes=[pltpu.VMEM((1, LANES), F32)],
        compiler_params=_cparams(("arbitrary", "arbitrary")), name="prep",
    )(small, par)


HALO = 16


def _bmm(a, b):
    return lax.dot_general(a.astype(BF16), b.astype(BF16), (((2,), (1,)), ((0,), (0,))),
                           preferred_element_type=F32)


def _bmm_nt(a, b):
    return lax.dot_general(a.astype(BF16), b.astype(BF16), (((2,), (2,)), ((0,), (0,))),
                           preferred_element_type=F32)


def _delta_kernel(qkv_ref, halo_ref, z_ref, cw_ref, col_ref, rowg_ref, gain_ref, o_ref, s_ref, *, nc):
    j = pl.program_id(1)

    @pl.when(j == 0)
    def _():
        s_ref[...] = jnp.zeros_like(s_ref)

    ts = nc * CHUNK
    n = NH * nc
    xb = qkv_ref[...]
    halo = halo_ref[...]
    halo = jnp.where(j == 0, jnp.zeros_like(halo), halo)
    xx = jnp.concatenate([halo, xb], axis=0)
    rr = lax.broadcasted_iota(jnp.int32, (3 * ts, HALO + ts), 0)
    cc = lax.broadcasted_iota(jnp.int32, (3 * ts, HALO + ts), 1)
    sel = jnp.where(cc == (rr % ts) + HALO - 1 - rr // ts, 1.0, 0.0).astype(BF16)
    shifted = _dot(sel, xx)
    cw = cw_ref[...]
    acc = xb.astype(F32) * cw[3:4, :]
    for dlt in (1, 2, 3):
        acc = acc + shifted[(dlt - 1) * ts:dlt * ts, :] * cw[3 - dlt:4 - dlt, :]
    act = acc * _sigmoid(acc)

    cv = col_ref[...]
    rg = rowg_ref[...]
    qs, ks, vs, betas, gcs, grs = [], [], [], [], [], []
    for h in range(NH):
        qh = act[:, h * HD:(h + 1) * HD]
        kh = act[:, BW + h * HD:BW + (h + 1) * HD]
        vh = act[:, 2 * BW + h * HD:2 * BW + (h + 1) * HD]
        qh = qh * (lax.rsqrt(jnp.sum(qh * qh, axis=-1, keepdims=True) + EPS) * (HD ** -0.5))
        kh = kh * lax.rsqrt(jnp.sum(kh * kh, axis=-1, keepdims=True) + EPS)
        qs.append(qh.reshape(nc, CHUNK, HD))
        ks.append(kh.reshape(nc, CHUNK, HD))
        vs.append(vh.reshape(nc, CHUNK, HD))
        betas.append(cv[:, L_BETA + h:L_BETA + h + 1].reshape(nc, CHUNK, 1))
        gcs.append(cv[:, L_G + h:L_G + h + 1].reshape(nc, CHUNK, 1))
        grs.append(rg[:, L_G + h:L_G + h + 1, :])
    q = jnp.concatenate(qs, axis=0)
    k = jnp.concatenate(ks, axis=0)
    v = jnp.concatenate(vs, axis=0)
    beta = jnp.concatenate(betas, axis=0)
    gc = jnp.concatenate(gcs, axis=0)
    gr = jnp.concatenate(grs, axis=0)

    ri = lax.broadcasted_iota(jnp.int32, (CHUNK, CHUNK), 0)
    ci = lax.broadcasted_iota(jnp.int32, (CHUNK, CHUNK), 1)
    causal = (ci <= ri)[None]
    strict = (ci < ri)[None]
    decay = jnp.exp(jnp.where(causal, gc - gr, -jnp.inf))
    kb = k * beta
    both = _bmm_nt(jnp.concatenate([kb, q], axis=1), k)
    lower = jnp.where(strict, both[:, :CHUNK] * decay, 0.0)
    qk = both[:, CHUNK:] * decay

    eye = jnp.where(ci == ri, 1.0, 0.0)[None]
    t_inv = eye - lower
    pw = _bmm(lower, lower)
    for it in range(5):
        t_inv = t_inv + _bmm(t_inv, pw)
        if it < 4:
            pw = _bmm(pw, pw)

    eg = jnp.exp(gc)
    rhs = jnp.concatenate([v * beta, kb * eg], axis=-1)
    sol = _bmm(t_inv, rhs)
    u = sol[:, :, :HD]
    w = sol[:, :, HD:]
    q_dec = q * eg
    g_end = gc[:, CHUNK - 1:CHUNK, :]
    k_dec = k * jnp.exp(g_end - gc)
    g_last = jnp.exp(g_end)

    def pick(t, c):
        return t.reshape((NH, nc) + t.shape[1:])[:, c]

    state = s_ref[...]
    outs = []
    for c in range(nc):
        ws_qs = _bmm(jnp.concatenate([pick(w, c), pick(q_dec, c)], axis=1), state)
        v_new = pick(u, c) - ws_qs[:, :CHUNK]
        o_c = ws_qs[:, CHUNK:] + _bmm(pick(qk, c), v_new)
        kd_t = jnp.swapaxes(pick(k_dec, c), 1, 2)
        state = state * pick(g_last, c) + _bmm(kd_t, v_new)
        outs.append(o_c)
    s_ref[...] = state

    gain = gain_ref[...]
    zf = z_ref[...].astype(F32)
    cols = []
    for h in range(NH):
        oh = jnp.concatenate([outs[c][h] for c in range(nc)], axis=0)
        oh = oh * lax.rsqrt(jnp.mean(oh * oh, axis=-1, keepdims=True) + EPS) * gain
        zh = zf[:, h * HD:(h + 1) * HD]
        cols.append(oh * (zh * _sigmoid(zh)))
    o_ref[...] = jnp.concatenate(cols, axis=1).astype(BF16)


def _delta(proj, conv_w8, col, rowg, gain, bsz, seq):
    m = proj.shape[0]
    nc = 4
    ts = nc * CHUNK
    nt = seq // ts
    qkv_blk = P_DN // (3 * BW)
    return pl.pallas_call(
        functools.partial(_delta_kernel, nc=nc),
        grid=(bsz, nt),
        in_specs=[pl.BlockSpec((ts, 3 * BW), lambda b, j: (b * nt + j, qkv_blk)),
                  pl.BlockSpec((HALO, 3 * BW), lambda b, j: (jnp.maximum((b * nt + j) * (ts // HALO) - 1, 0), qkv_blk)),
                  pl.BlockSpec((ts, BW), lambda b, j: (b * nt + j, P_DNZ // BW)),
                  pl.BlockSpec((8, 3 * BW), lambda b, j: (0, 0)),
                  pl.BlockSpec((ts, LANES), lambda b, j: (b * nt + j, 0)),
                  pl.BlockSpec((nc, 32, CHUNK), lambda b, j: (b * nt + j, 0, 0)),
                  pl.BlockSpec((1, HD), lambda b, j: (0, 0))],
        out_specs=pl.BlockSpec((ts, BW), lambda b, j: (b * nt + j, 0)),
        out_shape=jax.ShapeDtypeStruct((m, BW), BF16),
        scratch_shapes=[pltpu.VMEM((NH, HD, HD), F32)],
        compiler_params=_cparams(("arbitrary", "arbitrary")), name="delta",
    )(proj, proj, proj, conv_w8, col, rowg, gain.reshape(1, HD))


FOX_T = 512
FOX_HPS = 4


def _fox_kernel(q_ref, k_ref, v_ref, crow_ref, o_ref, vx_ref, m_ref, acc_ref):
    t = FOX_T
    hp = pl.program_id(1)
    qi = pl.program_id(2)

    @pl.when(qi == 0)
    def _():
        lane = lax.broadcasted_iota(jnp.int32, (v_ref.shape[0], HD), 1)
        ones_col = jnp.where(lane == 0, 1.0, 0.0).astype(BF16)
        for e in range(FOX_HPS):
            vx_ref[e, :, :HD] = v_ref[:, e * HD:(e + 1) * HD]
            vx_ref[e, :, HD:] = ones_col

    m_ref[...] = jnp.full_like(m_ref, -jnp.inf)
    acc_ref[...] = jnp.zeros_like(acc_ref)

    def block(ki, masked):
        off = pl.multiple_of(ki * t, t)
        for e in range(FOX_HPS):
            s = lax.dot_general(q_ref[:, e * HD:(e + 1) * HD], k_ref[pl.ds(off, t), e * HD:(e + 1) * HD],
                                (((1,), (1,)), ((), ())), preferred_element_type=F32)
            s = s - crow_ref[pl.ds(FOX_HPS * hp + e, 1), pl.ds(off, t)]
            if masked:
                row = lax.broadcasted_iota(jnp.int32, (t, t), 0)
                col = lax.broadcasted_iota(jnp.int32, (t, t), 1)
                s = jnp.where(col <= row, s, -jnp.inf)
            m_prev = m_ref[e]
            m_new = jnp.maximum(m_prev, jnp.max(s, axis=1, keepdims=True))
            alpha = jnp.exp(m_prev - m_new)
            p = jnp.exp((s - jnp.tile(m_new, (1, t // LANES))).astype(BF16))
            acc_ref[e] = jnp.tile(alpha, (1, 2)) * acc_ref[e] + _dot(p, vx_ref[e, pl.ds(off, t), :])
            m_ref[e] = m_new

    def body(ki, carry):
        block(ki, False)
        return carry

    lax.fori_loop(0, qi, body, 0)
    block(qi, True)
    for e in range(FOX_HPS):
        o_ref[:, e * HD:(e + 1) * HD] = (acc_ref[e, :, :HD] / acc_ref[e, :, HD:HD + 1]).astype(BF16)


def _fox(proj, rowf, bsz, seq):
    m = proj.shape[0]
    t = FOX_T
    nq = seq // t
    w = FOX_HPS * HD
    return pl.pallas_call(
        _fox_kernel,
        grid=(bsz, NH // FOX_HPS, nq),
        in_specs=[pl.BlockSpec((t, w), lambda b, h, qi: (b * nq + qi, P_FQ // w + h)),
                  pl.BlockSpec((seq, w), lambda b, h, qi: (b, P_FK // w + h)),
                  pl.BlockSpec((seq, w), lambda b, h, qi: (b, P_FV // w + h)),
                  pl.BlockSpec((NH, seq), lambda b, h, qi: (0, b))],
        out_specs=pl.BlockSpec((t, w), lambda b, h, qi: (b * nq + qi, h)),
        out_shape=jax.ShapeDtypeStruct((m, BW), BF16),
        scratch_shapes=[pltpu.VMEM((FOX_HPS, seq, 2 * HD), BF16), pltpu.VMEM((FOX_HPS, t, HD), F32),
                        pltpu.VMEM((FOX_HPS, t, 2 * HD), F32)],
        compiler_params=_cparams(("arbitrary", "arbitrary", "arbitrary")), name="fox",
    )(proj, proj, proj, rowf)


def _swap_halves(x):
    w = x.shape[-1]
    lane = lax.broadcasted_iota(jnp.int32, x.shape, x.ndim - 1)
    return jnp.where((lane % SWA_D) < SWA_D // 2, pltpu.roll(x, w - SWA_D // 2, axis=x.ndim - 1),
                     pltpu.roll(x, SWA_D // 2, axis=x.ndim - 1))


def _swa_kernel(sink_ref, q_ref, kc_ref, kp_ref, vc_ref, vp_ref, cc_ref, sc_ref, cp_ref, sp_ref, o_ref, *, nblk):
    i = pl.program_id(0)
    first = (i % nblk) == 0
    cos_c, sin_c = cc_ref[...], sc_ref[...]
    q = q_ref[...].astype(F32)
    q = q * jnp.tile(cos_c, (1, SWA_QH // 2)) + _swap_halves(q) * jnp.tile(sin_c, (1, SWA_QH // 2))
    kc = kc_ref[...].astype(F32)
    kc = kc * cos_c + _swap_halves(kc) * sin_c
    kp = kp_ref[...].astype(F32)
    kp = kp * cp_ref[...] + _swap_halves(kp) * sp_ref[...]
    kk = jnp.concatenate([kp, kc], axis=0).astype(BF16)
    vv = jnp.concatenate([vp_ref[...], vc_ref[...]], axis=0)
    r = lax.broadcasted_iota(jnp.int32, (WIN, 2 * WIN), 0)
    c = lax.broadcasted_iota(jnp.int32, (WIN, 2 * WIN), 1)
    mask = (c > r) & (c <= r + WIN) & ((c >= WIN) | jnp.logical_not(first))
    qb = q.astype(BF16)
    outs = []
    for hq in range(SWA_QH):
        g = hq // SWA_G
        qh = qb[:, hq * SWA_D:(hq + 1) * SWA_D]
        kh = kk[:, g * SWA_D:(g + 1) * SWA_D]
        vh = vv[:, g * SWA_D:(g + 1) * SWA_D]
        s = lax.dot_general(qh, kh, (((1,), (1,)), ((), ())), preferred_element_type=F32)
        s = jnp.where(mask, s, -jnp.inf)
        sink = sink_ref[hq]
        mx = jnp.maximum(jnp.max(s, axis=1, keepdims=True), sink)
        p = jnp.exp(s - mx)
        den = jnp.sum(p, axis=1, keepdims=True) + jnp.exp(sink - mx)
        outs.append(_dot(p.astype(BF16), vh) / den)
    o_ref[...] = jnp.concatenate(outs, axis=1).astype(BF16)


def _swa(proj, sinks, cos4, sin4, bsz, seq):
    m = proj.shape[0]
    nblk = seq // WIN
    prev = lambda i: jnp.maximum(i - 1, 0)
    kcol, vcol = P_SK // LANES, P_SV // LANES
    return pl.pallas_call(
        functools.partial(_swa_kernel, nblk=nblk),
        grid=(m // WIN,),
        in_specs=[pl.BlockSpec(memory_space=pltpu.SMEM),
                  pl.BlockSpec((WIN, BW), lambda i: (i, P_SQ // BW)),
                  pl.BlockSpec((WIN, LANES), lambda i: (i, kcol)),
                  pl.BlockSpec((WIN, LANES), lambda i: (prev(i), kcol)),
                  pl.BlockSpec((WIN, LANES), lambda i: (i, vcol)),
                  pl.BlockSpec((WIN, LANES), lambda i: (prev(i), vcol)),
                  pl.BlockSpec((WIN, LANES), lambda i: (i, 0)),
                  pl.BlockSpec((WIN, LANES), lambda i: (i, 0)),
                  pl.BlockSpec((WIN, LANES), lambda i: (prev(i), 0)),
                  pl.BlockSpec((WIN, LANES), lambda i: (prev(i), 0))],
        out_specs=pl.BlockSpec((WIN, BW), lambda i: (i, 0)),
        out_shape=jax.ShapeDtypeStruct((m, BW), BF16),
        compiler_params=_cparams(("arbitrary",)), name="swa",
    )(sinks, proj, proj, proj, proj, proj, cos4, sin4, cos4, sin4)


def _merge_kernel(oa_ref, ob_ref, oc_ref, ga_ref, gb_ref, gc_ref, w_ref, o_ref):
    acc = _sigmoid(ga_ref[...].astype(F32)) * _dot(oa_ref[...], w_ref[0])
    acc = acc + _sigmoid(gb_ref[...].astype(F32)) * _dot(ob_ref[...], w_ref[1])
    acc = acc + _sigmoid(gc_ref[...].astype(F32)) * _dot(oc_ref[...], w_ref[2])
    o_ref[...] = acc.astype(BF16)


def _merge(o_a, o_b, o_c, proj, w_branch):
    m = o_a.shape[0]
    tm, tn = 1024, 512
    gblk = lambda b: (lambda j, i: (i, (P_GATE + b * D) // tn + j))
    oblk = pl.BlockSpec((tm, BW), lambda j, i: (i, 0))
    return pl.pallas_call(
        _merge_kernel, grid=(D // tn, m // tm),
        in_specs=[oblk, oblk, oblk,
                  pl.BlockSpec((tm, tn), gblk(0)), pl.BlockSpec((tm, tn), gblk(1)), pl.BlockSpec((tm, tn), gblk(2)),
                  pl.BlockSpec((3, BW, tn), lambda j, i: (0, 0, j))],
        out_specs=pl.BlockSpec((tm, tn), lambda j, i: (i, j)),
        out_shape=jax.ShapeDtypeStruct((m, D), BF16),
        compiler_params=_cparams(("arbitrary", "arbitrary")), name="merge",
    )(o_a, o_b, o_c, proj, proj, proj, w_branch)


def _swiglu_acc(h, wg_ref, wu_ref, wd_ref, acc_ref):
    a = _dot(h, wg_ref[...])
    b = _dot(h, wu_ref[...])
    t = (a * _sigmoid(a) * b).astype(BF16)
    acc_ref[...] += _dot(t, wd_ref[...])


def _ffn_kernel(h_ref, wg_ref, wu_ref, wd_ref, x_ref, g_ref, o_ref, acc_ref, *, nf):
    f = pl.program_id(1)

    @pl.when(f == 0)
    def _():
        acc_ref[...] = jnp.zeros_like(acc_ref)

    _swiglu_acc(h_ref[...], wg_ref, wu_ref, wd_ref, acc_ref)

    @pl.when(f == nf - 1)
    def _():
        o_ref[...] = x_ref[...] + g_ref[...] * acc_ref[...]


def _ffn_dense(h, wg, wu, wd, x, gate, seq):
    m, d = h.shape
    ff = wg.shape[1]
    tm, tf = 512, 1024
    nf = ff // tf
    tpb = seq // tm
    return pl.pallas_call(
        functools.partial(_ffn_kernel, nf=nf), grid=(m // tm, nf),
        in_specs=[pl.BlockSpec((tm, d), lambda i, f: (i, 0)),
                  pl.BlockSpec((d, tf), lambda i, f: (0, f)),
                  pl.BlockSpec((d, tf), lambda i, f: (0, f)),
                  pl.BlockSpec((tf, d), lambda i, f: (f, 0)),
                  pl.BlockSpec((tm, d), lambda i, f: (i, 0)),
                  pl.BlockSpec((None, 1, d), lambda i, f: (i // tpb, 0, 0))],
        out_specs=pl.BlockSpec((tm, d), lambda i, f: (i, 0)),
        out_shape=jax.ShapeDtypeStruct((m, d), F32),
        scratch_shapes=[pltpu.VMEM((tm, d), F32)],
        compiler_params=_cparams(("arbitrary", "arbitrary")), name="ffn_dense",
    )(h, wg, wu, wd, x, gate)


def _row_copy(src_hbm, idx_ref, pos, dst, r, sem):
    return pltpu.make_async_copy(src_hbm.at[pl.ds(idx_ref[pos], 1), :], dst.at[pl.ds(r, 1), :], sem)


def _start_row_gather(src_hbm, idx_ref, base, n, dst, sem):
    def body(r, carry):
        _row_copy(src_hbm, idx_ref, base + r, dst, r, sem).start()
        return carry

    lax.fori_loop(0, n, body, 0, unroll=8)


def _wait_row_gather(src_hbm, n, dst, sem):
    pltpu.make_async_copy(src_hbm.at[pl.ds(0, n), :], dst, sem).wait()


def _ffn_group_kernel(te_ref, tv_ref, src_ref, h_hbm, wg_ref, wu_ref, wd_ref, o_ref,
                      buf, hbf_ref, acc_ref, sem, *, nf, tm, n_tiles):
    i = pl.program_id(0)
    f = pl.program_id(1)
    slot = i % 2
    live = tv_ref[i] > 0
    per = -(-tm // nf)
    last = tm - per * (nf - 1)

    @pl.when(f == 0)
    def _():
        @pl.when(i == 0)
        def _():
            _start_row_gather(h_hbm, src_ref, 0, tm, buf.at[0], sem.at[0])

        _wait_row_gather(h_hbm, tm, buf.at[slot], sem.at[slot])
        hbf_ref[...] = buf[slot].astype(BF16)
        acc_ref[...] = jnp.zeros_like(acc_ref)

    def prefetch(first, count):
        for j in range(count):
            r = first + j
            _row_copy(h_hbm, src_ref, (i + 1) * tm + r, buf.at[1 - slot], r, sem.at[1 - slot]).start()

    def step(count, compute):
        c0 = count // 3
        c1 = (count - c0) // 2
        first = f * per
        if not compute:
            prefetch(first, count)
            return
        h = hbf_ref[...]
        a = _dot(h, wg_ref[...])
        prefetch(first, c0)
        b = _dot(h, wu_ref[...])
        prefetch(first + c0, c1)
        t = (a * _sigmoid(a) * b).astype(BF16)
        acc_ref[...] += _dot(t, wd_ref[...])
        prefetch(first + c0 + c1, count - c0 - c1)

    for is_last, count in ((False, per), (True, last)):
        for compute in (True, False):
            cond = (f == nf - 1) if is_last else (f < nf - 1)
            cond = cond & (live if compute else jnp.logical_not(live))
            pl.when(cond)(functools.partial(step, count, compute))

    @pl.when(f == nf - 1)
    def _():
        o_ref[...] = acc_ref[...]

        @pl.when(i == n_tiles - 1)
        def _():
            _wait_row_gather(h_hbm, tm, buf.at[1 - slot], sem.at[1 - slot])


def _ffn_grouped(h, wg, wu, wd, tile_expert, tile_valid, src_rows, tm):
    rows = src_rows.shape[0] - tm
    d, ff = wg.shape[1], wg.shape[2]
    tf = 1024
    nf = ff // tf
    n_tiles = rows // tm

    def fsel(i, f, tv):
        return jnp.where(tv[i] > 0, f, nf - 1)

    grid_spec = pltpu.PrefetchScalarGridSpec(
        num_scalar_prefetch=3, grid=(n_tiles, nf),
        in_specs=[pl.BlockSpec(memory_space=pl.ANY),
                  pl.BlockSpec((None, d, tf), lambda i, f, te, tv, sr: (te[i], 0, fsel(i, f, tv))),
                  pl.BlockSpec((None, d, tf), lambda i, f, te, tv, sr: (te[i], 0, fsel(i, f, tv))),
                  pl.BlockSpec((None, tf, d), lambda i, f, te, tv, sr: (te[i], fsel(i, f, tv), 0))],
        out_specs=pl.BlockSpec((tm, d), lambda i, f, te, tv, sr: (i, 0)),
        scratch_shapes=[pltpu.VMEM((2, tm, d), F32),
                        pltpu.VMEM((tm, d), BF16),
                        pltpu.VMEM((tm, d), F32),
                        pltpu.SemaphoreType.DMA((2,))])
    return pl.pallas_call(
        functools.partial(_ffn_group_kernel, nf=nf, tm=tm, n_tiles=n_tiles), grid_spec=grid_spec,
        out_shape=jax.ShapeDtypeStruct((rows, d), F32),
        compiler_params=_cparams(("arbitrary", "arbitrary")), name="ffn_grouped",
    )(tile_expert, tile_valid, src_rows, h, wg, wu, wd)


COMBINE_TM = 256


def _combine_kernel(pos_ref, x_ref, y_hbm, r_ref, g_ref, gain_ref, o_ref, buf, sem, *, tm, nt, m, final):
    i = pl.program_id(0)
    slot = i % 2

    def start(tile, s):
        for k in range(2):
            _start_row_gather(y_hbm, pos_ref, k * m + tile * tm, tm, buf.at[s, k], sem.at[s])

    @pl.when(i == 0)
    def _():
        start(0, 0)

    @pl.when(i + 1 < nt)
    def _():
        start(jnp.minimum(i + 1, nt - 1), 1 - slot)

    for k in range(2):
        _wait_row_gather(y_hbm, tm, buf.at[slot, k], sem.at[slot])
    w0 = r_ref[:, 2:3]
    w1 = r_ref[:, 3:4]
    xn = x_ref[...] + g_ref[...] * (w0 * buf[slot, 0] + w1 * buf[slot, 1])
    if final:
        xn = xn * lax.rsqrt(jnp.mean(xn * xn, axis=-1, keepdims=True) + EPS) * gain_ref[...]
    o_ref[...] = xn


def _combine(x, y, pos_rows, route, gate, seq, final_gain=None):
    m, d = x.shape
    tm = COMBINE_TM
    nt = m // tm
    tpb = seq // tm
    final = final_gain is not None
    gain = (final_gain if final else jnp.ones((d,), F32)).reshape(1, d)
    grid_spec = pltpu.PrefetchScalarGridSpec(
        num_scalar_prefetch=1, grid=(nt,),
        in_specs=[pl.BlockSpec((tm, d), lambda i, p: (i, 0)),
                  pl.BlockSpec(memory_space=pl.ANY),
                  pl.BlockSpec((tm, LANES), lambda i, p: (i, 0)),
                  pl.BlockSpec((None, 1, d), lambda i, p: (i // tpb, 0, 0)),
                  pl.BlockSpec((1, d), lambda i, p: (0, 0))],
        out_specs=pl.BlockSpec((tm, d), lambda i, p: (i, 0)),
        scratch_shapes=[pltpu.VMEM((2, 2, tm, d), F32), pltpu.SemaphoreType.DMA((2,))])
    return pl.pallas_call(
        functools.partial(_combine_kernel, tm=tm, nt=nt, m=m, final=final), grid_spec=grid_spec,
        out_shape=jax.ShapeDtypeStruct((m, d), F32),
        compiler_params=_cparams(("arbitrary",)), name="combine",
    )(pos_rows, x, y, route, gate, gain)


def _final_norm_kernel(x_ref, gain_ref, o_ref):
    x = x_ref[...]
    o_ref[...] = x * lax.rsqrt(jnp.mean(x * x, axis=-1, keepdims=True) + EPS) * gain_ref[...]


def _final_norm(x, gain):
    m, d = x.shape
    tm = 512
    blk = pl.BlockSpec((tm, d), lambda i: (i, 0))
    return pl.pallas_call(
        _final_norm_kernel, grid=(m // tm,),
        in_specs=[blk, pl.BlockSpec((1, d), lambda i: (0, 0))],
        out_specs=blk, out_shape=jax.ShapeDtypeStruct((m, d), F32),
        compiler_params=_cparams(("arbitrary",)), name="final_norm",
    )(x, gain.reshape(1, d))


MOE_TM = 512


def _moe(x, h, route, counts, wg, wu, wd, gate, seq, final_gain=None):
    m, d = x.shape
    tm = MOE_TM
    rows = 2 * m + NE * tm
    n_tiles = rows // tm
    e_flat = route[:, 0:2].astype(jnp.int32).reshape(-1)
    rank = route[:, 4:6].astype(jnp.int32).reshape(-1)
    counts = counts[0, :NE].astype(jnp.int32)
    padded = ((counts + tm - 1) // tm) * tm
    ends = jnp.cumsum(padded)
    starts = ends - padded
    pos = jnp.take(starts, e_flat) + rank
    src_token = jnp.zeros((rows + tm,), jnp.int32).at[pos].set(jnp.arange(2 * m, dtype=jnp.int32) // 2)
    tile_start = jnp.arange(n_tiles, dtype=jnp.int32) * tm
    tile_expert = jnp.minimum(jnp.sum((tile_start[:, None] >= ends[None, :]).astype(jnp.int32), axis=1), NE - 1)
    tile_valid = (tile_start < ends[-1]).astype(jnp.int32)

    ys = _ffn_grouped(h, wg, wu, wd, tile_expert, tile_valid, src_token, tm)
    pos_rows = pos.reshape(m, 2).T.reshape(-1)
    return _combine(x, ys, pos_rows, route, gate, seq, final_gain)


REGROUP_TN = 256
REGROUP_SEGS = ((P_GATE, 8472, 6144), (P_DN, 0, 4096), (P_FQ, 4112, 3072), (P_SQ, 7192, 1280))


def _regroup_kernel(blk_ref, shift_ref, scale_ref, a_ref, b_ref, c_ref, o_ref):
    j = pl.program_id(0)
    win = jnp.concatenate([a_ref[...], b_ref[...], c_ref[...]], axis=1)
    width = REGROUP_TN + LANES
    win = pltpu.roll(win, width - shift_ref[j], axis=1)
    o_ref[...] = (win[:, :REGROUP_TN] * scale_ref[j]).astype(BF16)


def _regroup_w_in(w_in, layer):
    d = w_in.shape[1]
    tn = REGROUP_TN
    blk, shift, scale = [], [], []
    for dst, src, width in REGROUP_SEGS:
        for t in range(width // tn):
            s = src + t * tn
            blk.append(s // LANES)
            shift.append(s % LANES)
            o = dst + t * tn
            scale.append(HD ** -0.5 if P_FQ <= o < P_FK else SWA_D ** -0.5 if P_SQ <= o < P_SK else 1.0)
    n_tiles = len(blk)
    assert n_tiles * tn == NP
    grid_spec = pltpu.PrefetchScalarGridSpec(
        num_scalar_prefetch=3, grid=(n_tiles,),
        in_specs=[pl.BlockSpec((None, d, LANES), lambda j, bk, sh, sc, k=k: (layer, 0, bk[j] + k)) for k in range(3)],
        out_specs=pl.BlockSpec((d, tn), lambda j, bk, sh, sc: (0, j)))
    return pl.pallas_call(
        _regroup_kernel, grid_spec=grid_spec,
        out_shape=jax.ShapeDtypeStruct((d, NP), BF16),
        compiler_params=_cparams(("arbitrary",)), name="regroup_w_in",
    )(jnp.asarray(blk, jnp.int32), jnp.asarray(shift, jnp.int32), jnp.asarray(scale, F32), w_in, w_in, w_in)


def _small_w_kernel(a_ref, b_ref, o_ref):
    lane = lax.broadcasted_iota(jnp.int32, a_ref.shape, 1)
    o_ref[...] = jnp.where(lane < L_F, a_ref[...], jnp.where(lane < L_F + NH, b_ref[...], 0.0)).astype(BF16)


def _small_w_in(w_in, layer):
    d = w_in.shape[1]
    return pl.pallas_call(
        _small_w_kernel, grid=(1,),
        in_specs=[pl.BlockSpec((None, d, LANES), lambda i: (layer, 0, 4096 // LANES)),
                  pl.BlockSpec((None, d, LANES), lambda i: (layer, 0, 7184 // LANES))],
        out_specs=pl.BlockSpec((d, LANES), lambda i: (0, 0)),
        out_shape=jax.ShapeDtypeStruct((d, LANES), BF16),
        compiler_params=_cparams(("arbitrary",)), name="small_w_in",
    )(w_in, w_in)


def kernel(x, c, positions, w_ada, b_ada, norm_mix, w_in, conv_w, dn_a_log, dn_dt_bias, dn_norm,
           fox_b_forget, swa_sinks, w_branch, w_out, norm_ffn, ffn_w_gate, ffn_w_up, ffn_w_down,
           moe_router, moe_w_gate, moe_w_up, moe_w_down, final_norm):
    bsz, seq, d = x.shape
    depth = w_ada.shape[0]
    m = bsz * seq
    xf = x.reshape(m, d)

    inv_freq = 10000.0 ** (-jnp.arange(0, SWA_D, 2, dtype=F32) / SWA_D)
    ang = positions.astype(F32).reshape(m, 1) * inv_freq[None, :]
    cos, sin = jnp.cos(ang), jnp.sin(ang)
    cos4 = jnp.tile(cos, (1, 4))
    sin4 = jnp.tile(jnp.concatenate([-sin, sin], axis=1), (1, 2))

    c_pad = jnp.pad(c, ((0, 8 - bsz), (0, 0)))
    mod = _ada(c_pad, w_ada, b_ada)[:, :bsz].reshape(depth, bsz, 6, 1, d)

    for layer in range(depth):
        sh1, sc1, g1, sh2, sc2, g2 = (mod[layer, :, t] for t in range(6))
        w_main = _regroup_w_in(w_in, layer)
        w_small = _small_w_in(w_in, layer)
        h, small = _norm_modulate(xf, norm_mix[layer], sc1, sh1, seq, w_small=w_small)
        proj = _matmul(h, w_main, BF16, 1024, 2432, "in_proj")
        par = jnp.zeros((8, LANES), F32)
        par = par.at[0, L_G:L_G + NH].set(dn_a_log[layer])
        par = par.at[1, L_G:L_G + NH].set(dn_dt_bias[layer])
        par = par.at[1, L_F:L_F + NH].set(fox_b_forget[layer])
        col, rowf, rowg = _prep(small, par, bsz, seq)
        conv_w8 = jnp.pad(conv_w[layer], ((0, 4), (0, 0)))
        o_a = _delta(proj, conv_w8, col, rowg, dn_norm[layer], bsz, seq)
        o_b = _fox(proj, rowf, bsz, seq)
        o_c = _swa(proj, swa_sinks[layer], cos4, sin4, bsz, seq)
        merged = _merge(o_a, o_b, o_c, proj, w_branch[layer].astype(BF16))
        xf = _matmul_resid(merged, w_out[layer].astype(BF16), xf, g1, seq, 512, 2048, "out_proj")
        if layer % 2 == 0:
            i = layer // 2
            h2 = _norm_modulate(xf, norm_ffn[layer], sc2, sh2, seq)
            xf = _ffn_dense(h2, ffn_w_gate[i].astype(BF16), ffn_w_up[i].astype(BF16),
                            ffn_w_down[i].astype(BF16), xf, g2, seq)
        else:
            i = layer // 2
            h2, route, counts = _norm_modulate(xf, norm_ffn[layer], sc2, sh2, seq, w_router=moe_router[i])
            last = layer == depth - 1
            xf = _moe(xf, h2, route, counts, moe_w_gate[i].astype(BF16), moe_w_up[i].astype(BF16),
                      moe_w_down[i].astype(BF16), g2, seq, final_gain=final_norm if last else None)
            if last:
                return xf.reshape(bsz, seq, d)
    return _final_norm(xf, final_norm).reshape(bsz, seq, d)
```

```python
import functools

import jax
import jax.numpy as jnp
from jax import lax
from jax.experimental import pallas as pl
from jax.experimental.pallas import tpu as pltpu

F32 = jnp.float32
BF16 = jnp.bfloat16
EPS = 1e-6

D = 2048
HD = 128
NH = 8
BW = 1024
CHUNK = 64
SWA_D = 64
SWA_QH = 16
SWA_KVH = 2
SWA_G = 8
WIN = 128
D_FF = 7168
NE = 8
LANES = 128

P_GATE = 0
P_DN = 6144
P_DNZ = 9216
P_FQ, P_FK, P_FV = 10240, 11264, 12288
P_SQ, P_SK, P_SV = 13312, 14336, 14464
NP = 14592
L_BETA, L_G, L_F = 0, 8, 16

VMEM_LIMIT = 56 * 1024 * 1024


def _cparams(sem):
    return pltpu.CompilerParams(dimension_semantics=sem, vmem_limit_bytes=VMEM_LIMIT)


def _sigmoid(x):
    return 1.0 / (1.0 + jnp.exp(-x))


def _split3(x):
    hi = x.astype(BF16)
    r = x - hi.astype(F32)
    mid = r.astype(BF16)
    lo = (r - mid.astype(F32)).astype(BF16)
    return hi, mid, lo


def _dot(a, b):
    return jnp.dot(a, b, preferred_element_type=F32)


def _dot_hi(a, b):
    a0, a1, a2 = _split3(a)
    b0, b1, b2 = _split3(b)
    return (_dot(a0, b0) + (_dot(a0, b1) + _dot(a1, b0))
            + (_dot(a0, b2) + _dot(a1, b1) + _dot(a2, b0)))


def _ada_kernel(c_ref, w_ref, b_ref, o_ref):
    c = c_ref[...]
    ca = c * _sigmoid(c)
    a0 = ca.astype(BF16)
    a1 = (ca - a0.astype(F32)).astype(BF16)
    w = w_ref[...]
    w0 = w.astype(BF16)
    w1 = (w - w0.astype(F32)).astype(BF16)
    o_ref[...] = _dot(a0, w0) + (_dot(a0, w1) + _dot(a1, w0)) + b_ref[...]


def _ada(c_pad, w_ada, b_ada):
    depth, d, n = w_ada.shape
    tn = 1536
    return pl.pallas_call(
        _ada_kernel,
        grid=(depth, n // tn),
        in_specs=[pl.BlockSpec((8, d), lambda l, j: (0, 0)),
                  pl.BlockSpec((None, d, tn), lambda l, j: (l, 0, j)),
                  pl.BlockSpec((None, 1, tn), lambda l, j: (l, 0, j))],
        out_specs=pl.BlockSpec((None, 8, tn), lambda l, j: (l, 0, j)),
        out_shape=jax.ShapeDtypeStruct((depth, 8, n), F32),
        compiler_params=_cparams(("arbitrary", "arbitrary")),
        name="ada",
    )(c_pad, w_ada, b_ada.reshape(depth, 1, n))


def _normmod(x_ref, gain_ref, sc_ref, sh_ref):
    x = x_ref[...]
    ms = jnp.mean(x * x, axis=-1, keepdims=True)
    y = x * lax.rsqrt(ms + EPS) * gain_ref[...]
    return y * (1.0 + sc_ref[...]) + sh_ref[...]


def _normmod_kernel(x_ref, gain_ref, sc_ref, sh_ref, h_ref):
    h_ref[...] = _normmod(x_ref, gain_ref, sc_ref, sh_ref).astype(BF16)


def _normmod_small_kernel(x_ref, gain_ref, sc_ref, sh_ref, ws_ref, h_ref, s_ref):
    h = _normmod(x_ref, gain_ref, sc_ref, sh_ref).astype(BF16)
    h_ref[...] = h
    s_ref[...] = _dot(h, ws_ref[...])


def _normmod_route_kernel(x_ref, gain_ref, sc_ref, sh_ref, wr_ref, h_ref, r_ref, cnt_ref, carry_ref):
    @pl.when(pl.program_id(0) == 0)
    def _():
        carry_ref[...] = jnp.zeros_like(carry_ref)

    h = _normmod(x_ref, gain_ref, sc_ref, sh_ref)
    h_ref[...] = h
    logits = _dot_hi(h, wr_ref[...])
    lane = lax.broadcasted_iota(jnp.int32, logits.shape, 1)
    neg = jnp.float32(-jnp.inf)
    l1 = jnp.where(lane < NE, logits, neg)
    m1 = jnp.max(l1, axis=-1, keepdims=True)
    i1 = jnp.min(jnp.where(l1 == m1, lane, LANES), axis=-1, keepdims=True)
    l2 = jnp.where(lane == i1, neg, l1)
    m2 = jnp.max(l2, axis=-1, keepdims=True)
    i2 = jnp.min(jnp.where(l2 == m2, lane, LANES), axis=-1, keepdims=True)
    e = jnp.exp(m2 - m1)
    w1 = 1.0 / (1.0 + e)
    w2 = e / (1.0 + e)
    tm = logits.shape[0]
    picked = jnp.where((lane == i1) | (lane == i2), 1.0, 0.0)
    rr = lax.broadcasted_iota(jnp.int32, (tm, tm), 0)
    cc = lax.broadcasted_iota(jnp.int32, (tm, tm), 1)
    before = _dot(jnp.where(cc < rr, 1.0, 0.0).astype(BF16), picked.astype(BF16)) + carry_ref[...]
    rank1 = jnp.sum(jnp.where(lane == i1, before, 0.0), axis=-1, keepdims=True)
    rank2 = jnp.sum(jnp.where(lane == i2, before, 0.0), axis=-1, keepdims=True)
    carry_ref[...] += jnp.sum(picked, axis=0, keepdims=True)
    cnt_ref[...] = jnp.broadcast_to(carry_ref[...], cnt_ref.shape)
    vals = (i1.astype(F32), i2.astype(F32), w1, w2, rank1, rank2)
    out = jnp.zeros_like(logits)
    for pos, v in enumerate(vals):
        out = jnp.where(lane == pos, v, out)
    r_ref[...] = out


def _norm_modulate(x, gain, sc, sh, seq, w_router=None, w_small=None):
    m, d = x.shape
    tm = 512
    tpb = seq // tm
    in_specs = [pl.BlockSpec((tm, d), lambda i: (i, 0)),
                pl.BlockSpec((1, d), lambda i: (0, 0)),
                pl.BlockSpec((None, 1, d), lambda i: (i // tpb, 0, 0)),
                pl.BlockSpec((None, 1, d), lambda i: (i // tpb, 0, 0))]
    args = [x, gain.reshape(1, d), sc, sh]
    if w_small is not None:
        return pl.pallas_call(
            _normmod_small_kernel, grid=(m // tm,),
            in_specs=in_specs + [pl.BlockSpec((d, LANES), lambda i: (0, 0))],
            out_specs=[pl.BlockSpec((tm, d), lambda i: (i, 0)), pl.BlockSpec((tm, LANES), lambda i: (i, 0))],
            out_shape=[jax.ShapeDtypeStruct((m, d), BF16), jax.ShapeDtypeStruct((m, LANES), F32)],
            compiler_params=_cparams(("arbitrary",)), name="normmod_small",
        )(*args, w_small)
    if w_router is None:
        return pl.pallas_call(
            _normmod_kernel, grid=(m // tm,), in_specs=in_specs,
            out_specs=pl.BlockSpec((tm, d), lambda i: (i, 0)),
            out_shape=jax.ShapeDtypeStruct((m, d), BF16),
            compiler_params=_cparams(("arbitrary",)), name="normmod",
        )(*args)
    wr = jnp.pad(w_router, ((0, 0), (0, LANES - NE)))
    return pl.pallas_call(
        _normmod_route_kernel, grid=(m // tm,),
        in_specs=in_specs + [pl.BlockSpec((d, LANES), lambda i: (0, 0))],
        out_specs=[pl.BlockSpec((tm, d), lambda i: (i, 0)),
                   pl.BlockSpec((tm, LANES), lambda i: (i, 0)),
                   pl.BlockSpec((8, LANES), lambda i: (0, 0))],
        out_shape=[jax.ShapeDtypeStruct((m, d), F32),
                   jax.ShapeDtypeStruct((m, LANES), F32),
                   jax.ShapeDtypeStruct((8, LANES), F32)],
        scratch_shapes=[pltpu.VMEM((1, LANES), F32)],
        compiler_params=_cparams(("arbitrary",)), name="normmod_route",
    )(*args, wr)


def _mm_kernel(a_ref, w_ref, o_ref):
    o_ref[...] = _dot(a_ref[...], w_ref[...]).astype(o_ref.dtype)


def _matmul(a, w, out_dtype, tm, tn, name):
    m, k = a.shape
    n = w.shape[1]
    return pl.pallas_call(
        _mm_kernel, grid=(n // tn, m // tm),
        in_specs=[pl.BlockSpec((tm, k), lambda j, i: (i, 0)),
                  pl.BlockSpec((k, tn), lambda j, i: (0, j))],
        out_specs=pl.BlockSpec((tm, tn), lambda j, i: (i, j)),
        out_shape=jax.ShapeDtypeStruct((m, n), out_dtype),
        compiler_params=_cparams(("arbitrary", "arbitrary")), name=name,
    )(a, w)


def _mm_resid_kernel(a_ref, w_ref, x_ref, g_ref, o_ref):
    o_ref[...] = x_ref[...] + g_ref[...] * _dot(a_ref[...], w_ref[...])


def _matmul_resid(a, w, x, gate, seq, tm, tn, name):
    m, k = a.shape
    n = w.shape[1]
    tpb = seq // tm
    return pl.pallas_call(
        _mm_resid_kernel, grid=(n // tn, m // tm),
        in_specs=[pl.BlockSpec((tm, k), lambda j, i: (i, 0)),
                  pl.BlockSpec((k, tn), lambda j, i: (0, j)),
                  pl.BlockSpec((tm, tn), lambda j, i: (i, j)),
                  pl.BlockSpec((None, 1, tn), lambda j, i: (i // tpb, 0, j))],
        out_specs=pl.BlockSpec((tm, tn), lambda j, i: (i, j)),
        out_shape=jax.ShapeDtypeStruct((m, n), F32),
        compiler_params=_cparams(("arbitrary", "arbitrary")), name=name,
    )(a, w, x, gate)


def _prep_kernel(x_ref, par_ref, col_ref, rowf_ref, rowg_ref, carry_ref, *, ts):
    @pl.when(pl.program_id(1) == 0)
    def _():
        carry_ref[...] = jnp.zeros_like(carry_ref)

    x = x_ref[...]
    lane = lax.broadcasted_iota(jnp.int32, x.shape, 1)
    is_g = (lane >= L_G) & (lane < L_G + NH)
    is_f = (lane >= L_F) & (lane < L_F + NH)
    a_rate = jnp.exp(par_ref[0:1, :])
    z = x + par_ref[1:2, :]
    t = jnp.log1p(jnp.exp(-jnp.abs(z)))
    sig = _sigmoid(x)
    g = -a_rate * (jnp.maximum(z, 0.0) + t)
    logf = jnp.minimum(z, 0.0) - t
    vals = jnp.where(is_g, g, jnp.where(is_f, logf, 0.0))
    r = lax.broadcasted_iota(jnp.int32, (ts, ts), 0)
    c = lax.broadcasted_iota(jnp.int32, (ts, ts), 1)
    tri_full = jnp.where(c <= r, 1.0, 0.0).astype(BF16)
    tri_blk = jnp.where((c <= r) & ((r // CHUNK) == (c // CHUNK)), 1.0, 0.0).astype(BF16)
    v0, v1, v2 = _split3(vals)
    cs_full = _dot(tri_full, v0) + _dot(tri_full, v1) + _dot(tri_full, v2)
    cs_blk = _dot(tri_blk, v0) + _dot(tri_blk, v1) + _dot(tri_blk, v2)
    cum = cs_full + carry_ref[...]
    carry_ref[...] = cum[ts - 1:ts, :]
    out = jnp.where(lane < NH, sig, jnp.where(is_g, cs_blk, jnp.where(is_f, cum, 0.0)))
    col_ref[...] = out
    out_t = out.T
    rowf_ref[...] = out_t[L_F:L_F + NH, :]
    for ci in range(ts // CHUNK):
        rowg_ref[ci] = out_t[0:32, ci * CHUNK:(ci + 1) * CHUNK]


def _prep(small, par, bsz, seq):
    m = small.shape[0]
    ts = 256
    nt = seq // ts
    return pl.pallas_call(
        functools.partial(_prep_kernel, ts=ts),
        grid=(bsz, nt),
        in_specs=[pl.BlockSpec((ts, LANES), lambda b, j: (b * nt + j, 0)),
                  pl.BlockSpec((8, LANES), lambda b, j: (0, 0))],
        out_specs=[pl.BlockSpec((ts, LANES), lambda b, j: (b * nt + j, 0)),
                   pl.BlockSpec((NH, ts), lambda b, j: (0, b * nt + j)),
                   pl.BlockSpec((ts // CHUNK, 32, CHUNK), lambda b, j: (b * nt + j, 0, 0))],
        out_shape=[jax.ShapeDtypeStruct((m, LANES), F32),
                   jax.ShapeDtypeStruct((NH, m), F32),
                   jax.ShapeDtypeStruct((m // CHUNK, 32, CHUNK), F32)],
        scratch_shapes=[pltpu.VMEM((1, LANES), F32)],
        compiler_params=_cparams(("arbitrary", "arbitrary")), name="prep",
    )(small, par)


HALO = 16


def _bmm(a, b):
    return lax.dot_general(a.astype(BF16), b.astype(BF16), (((2,), (1,)), ((0,), (0,))),
                           preferred_element_type=F32)


def _bmm_nt(a, b):
    return lax.dot_general(a.astype(BF16), b.astype(BF16), (((2,), (2,)), ((0,), (0,))),
                           preferred_element_type=F32)


def _delta_kernel(qkv_ref, halo_ref, z_ref, cw_ref, col_ref, rowg_ref, gain_ref, o_ref, s_ref, *, nc):
    j = pl.program_id(1)

    @pl.when(j == 0)
    def _():
        s_ref[...] = jnp.zeros_like(s_ref)

    ts = nc * CHUNK
    n = NH * nc
    xb = qkv_ref[...]
    halo = halo_ref[...]
    halo = jnp.where(j == 0, jnp.zeros_like(halo), halo)
    xx = jnp.concatenate([halo, xb], axis=0)
    rr = lax.broadcasted_iota(jnp.int32, (3 * ts, HALO + ts), 0)
    cc = lax.broadcasted_iota(jnp.int32, (3 * ts, HALO + ts), 1)
    sel = jnp.where(cc == (rr % ts) + HALO - 1 - rr // ts, 1.0, 0.0).astype(BF16)
    shifted = _dot(sel, xx)
    cw = cw_ref[...]
    acc = xb.astype(F32) * cw[3:4, :]
    for dlt in (1, 2, 3):
        acc = acc + shifted[(dlt - 1) * ts:dlt * ts, :] * cw[3 - dlt:4 - dlt, :]
    act = acc * _sigmoid(acc)

    cv = col_ref[...]
    rg = rowg_ref[...]
    qs, ks, vs, betas, gcs, grs = [], [], [], [], [], []
    for h in range(NH):
        qh = act[:, h * HD:(h + 1) * HD]
        kh = act[:, BW + h * HD:BW + (h + 1) * HD]
        vh = act[:, 2 * BW + h * HD:2 * BW + (h + 1) * HD]
        qh = qh * (lax.rsqrt(jnp.sum(qh * qh, axis=-1, keepdims=True) + EPS) * (HD ** -0.5))
        kh = kh * lax.rsqrt(jnp.sum(kh * kh, axis=-1, keepdims=True) + EPS)
        qs.append(qh.reshape(nc, CHUNK, HD))
        ks.append(kh.reshape(nc, CHUNK, HD))
        vs.append(vh.reshape(nc, CHUNK, HD))
        betas.append(cv[:, L_BETA + h:L_BETA + h + 1].reshape(nc, CHUNK, 1))
        gcs.append(cv[:, L_G + h:L_G + h + 1].reshape(nc, CHUNK, 1))
        grs.append(rg[:, L_G + h:L_G + h + 1, :])
    q = jnp.concatenate(qs, axis=0)
    k = jnp.concatenate(ks, axis=0)
    v = jnp.concatenate(vs, axis=0)
    beta = jnp.concatenate(betas, axis=0)
    gc = jnp.concatenate(gcs, axis=0)
    gr = jnp.concatenate(grs, axis=0)

    ri = lax.broadcasted_iota(jnp.int32, (CHUNK, CHUNK), 0)
    ci = lax.broadcasted_iota(jnp.int32, (CHUNK, CHUNK), 1)
    causal = (ci <= ri)[None]
    strict = (ci < ri)[None]
    decay = jnp.exp(jnp.where(causal, gc - gr, -jnp.inf))
    kb = k * beta
    both = _bmm_nt(jnp.concatenate([kb, q], axis=1), k)
    lower = jnp.where(strict, both[:, :CHUNK] * decay, 0.0)
    qk = both[:, CHUNK:] * decay

    eye = jnp.where(ci == ri, 1.0, 0.0)[None]
    t_inv = eye - lower
    pw = _bmm(lower, lower)
    for it in range(5):
        t_inv = t_inv + _bmm(t_inv, pw)
        if it < 4:
            pw = _bmm(pw, pw)

    eg = jnp.exp(gc)
    rhs = jnp.concatenate([v * beta, kb * eg], axis=-1)
    sol = _bmm(t_inv, rhs)
    u = sol[:, :, :HD]
    w = sol[:, :, HD:]
    q_dec = q * eg
    g_end = gc[:, CHUNK - 1:CHUNK, :]
    k_dec = k * jnp.exp(g_end - gc)
    g_last = jnp.exp(g_end)

    def pick(t, c):
        return t.reshape((NH, nc) + t.shape[1:])[:, c]

    state = s_ref[...]
    outs = []
    for c in range(nc):
        ws_qs = _bmm(jnp.concatenate([pick(w, c), pick(q_dec, c)], axis=1), state)
        v_new = pick(u, c) - ws_qs[:, :CHUNK]
        o_c = ws_qs[:, CHUNK:] + _bmm(pick(qk, c), v_new)
        kd_t = jnp.swapaxes(pick(k_dec, c), 1, 2)
        state = state * pick(g_last, c) + _bmm(kd_t, v_new)
        outs.append(o_c)
    s_ref[...] = state

    gain = gain_ref[...]
    zf = z_ref[...].astype(F32)
    cols = []
    for h in range(NH):
        oh = jnp.concatenate([outs[c][h] for c in range(nc)], axis=0)
        oh = oh * lax.rsqrt(jnp.mean(oh * oh, axis=-1, keepdims=True) + EPS) * gain
        zh = zf[:, h * HD:(h + 1) * HD]
        cols.append(oh * (zh * _sigmoid(zh)))
    o_ref[...] = jnp.concatenate(cols, axis=1).astype(BF16)


def _delta(proj, conv_w8, col, rowg, gain, bsz, seq):
    m = proj.shape[0]
    nc = 4
    ts = nc * CHUNK
    nt = seq // ts
    qkv_blk = P_DN // (3 * BW)
    return pl.pallas_call(
        functools.partial(_delta_kernel, nc=nc),
        grid=(bsz, nt),
        in_specs=[pl.BlockSpec((ts, 3 * BW), lambda b, j: (b * nt + j, qkv_blk)),
                  pl.BlockSpec((HALO, 3 * BW), lambda b, j: (jnp.maximum((b * nt + j) * (ts // HALO) - 1, 0), qkv_blk)),
                  pl.BlockSpec((ts, BW), lambda b, j: (b * nt + j, P_DNZ // BW)),
                  pl.BlockSpec((8, 3 * BW), lambda b, j: (0, 0)),
                  pl.BlockSpec((ts, LANES), lambda b, j: (b * nt + j, 0)),
                  pl.BlockSpec((nc, 32, CHUNK), lambda b, j: (b * nt + j, 0, 0)),
                  pl.BlockSpec((1, HD), lambda b, j: (0, 0))],
        out_specs=pl.BlockSpec((ts, BW), lambda b, j: (b * nt + j, 0)),
        out_shape=jax.ShapeDtypeStruct((m, BW), BF16),
        scratch_shapes=[pltpu.VMEM((NH, HD, HD), F32)],
        compiler_params=_cparams(("arbitrary", "arbitrary")), name="delta",
    )(proj, proj, proj, conv_w8, col, rowg, gain.reshape(1, HD))


FOX_T = 512
FOX_HPS = 4


def _fox_kernel(q_ref, k_ref, v_ref, crow_ref, o_ref, vx_ref, m_ref, acc_ref):
    t = FOX_T
    hp = pl.program_id(1)
    qi = pl.program_id(2)

    @pl.when(qi == 0)
    def _():
        lane = lax.broadcasted_iota(jnp.int32, (v_ref.shape[0], HD), 1)
        ones_col = jnp.where(lane == 0, 1.0, 0.0).astype(BF16)
        for e in range(FOX_HPS):
            vx_ref[e, :, :HD] = v_ref[:, e * HD:(e + 1) * HD]
            vx_ref[e, :, HD:] = ones_col

    m_ref[...] = jnp.full_like(m_ref, -jnp.inf)
    acc_ref[...] = jnp.zeros_like(acc_ref)

    def block(ki, masked):
        off = pl.multiple_of(ki * t, t)
        for e in range(FOX_HPS):
            s = lax.dot_general(q_ref[:, e * HD:(e + 1) * HD], k_ref[pl.ds(off, t), e * HD:(e + 1) * HD],
                                (((1,), (1,)), ((), ())), preferred_element_type=F32)
            s = s - crow_ref[pl.ds(FOX_HPS * hp + e, 1), pl.ds(off, t)]
            if masked:
                row = lax.broadcasted_iota(jnp.int32, (t, t), 0)
                col = lax.broadcasted_iota(jnp.int32, (t, t), 1)
                s = jnp.where(col <= row, s, -jnp.inf)
            m_prev = m_ref[e]
            m_new = jnp.maximum(m_prev, jnp.max(s, axis=1, keepdims=True))
            alpha = jnp.exp(m_prev - m_new)
            p = jnp.exp((s - jnp.tile(m_new, (1, t // LANES))).astype(BF16))
            acc_ref[e] = jnp.tile(alpha, (1, 2)) * acc_ref[e] + _dot(p, vx_ref[e, pl.ds(off, t), :])
            m_ref[e] = m_new

    def body(ki, carry):
        block(ki, False)
        return carry

    lax.fori_loop(0, qi, body, 0)
    block(qi, True)
    for e in range(FOX_HPS):
        o_ref[:, e * HD:(e + 1) * HD] = (acc_ref[e, :, :HD] / acc_ref[e, :, HD:HD + 1]).astype(BF16)


def _fox(proj, rowf, bsz, seq):
    m = proj.shape[0]
    t = FOX_T
    nq = seq // t
    w = FOX_HPS * HD
    return pl.pallas_call(
        _fox_kernel,
        grid=(bsz, NH // FOX_HPS, nq),
        in_specs=[pl.BlockSpec((t, w), lambda b, h, qi: (b * nq + qi, P_FQ // w + h)),
                  pl.BlockSpec((seq, w), lambda b, h, qi: (b, P_FK // w + h)),
                  pl.BlockSpec((seq, w), lambda b, h, qi: (b, P_FV // w + h)),
                  pl.BlockSpec((NH, seq), lambda b, h, qi: (0, b))],
        out_specs=pl.BlockSpec((t, w), lambda b, h, qi: (b * nq + qi, h)),
        out_shape=jax.ShapeDtypeStruct((m, BW), BF16),
        scratch_shapes=[pltpu.VMEM((FOX_HPS, seq, 2 * HD), BF16), pltpu.VMEM((FOX_HPS, t, HD), F32),
                        pltpu.VMEM((FOX_HPS, t, 2 * HD), F32)],
        compiler_params=_cparams(("arbitrary", "arbitrary", "arbitrary")), name="fox",
    )(proj, proj, proj, rowf)


def _swap_halves(x):
    w = x.shape[-1]
    lane = lax.broadcasted_iota(jnp.int32, x.shape, x.ndim - 1)
    return jnp.where((lane % SWA_D) < SWA_D // 2, pltpu.roll(x, w - SWA_D // 2, axis=x.ndim - 1),
                     pltpu.roll(x, SWA_D // 2, axis=x.ndim - 1))


def _swa_kernel(sink_ref, q_ref, kc_ref, kp_ref, vc_ref, vp_ref, cc_ref, sc_ref, cp_ref, sp_ref, o_ref, *, nblk):
    i = pl.program_id(0)
    first = (i % nblk) == 0
    cos_c, sin_c = cc_ref[...], sc_ref[...]
    q = q_ref[...].astype(F32)
    q = q * jnp.tile(cos_c, (1, SWA_QH // 2)) + _swap_halves(q) * jnp.tile(sin_c, (1, SWA_QH // 2))
    kc = kc_ref[...].astype(F32)
    kc = kc * cos_c + _swap_halves(kc) * sin_c
    kp = kp_ref[...].astype(F32)
    kp = kp * cp_ref[...] + _swap_halves(kp) * sp_ref[...]
    kk = jnp.concatenate([kp, kc], axis=0).astype(BF16)
    vv = jnp.concatenate([vp_ref[...], vc_ref[...]], axis=0)
    r = lax.broadcasted_iota(jnp.int32, (WIN, 2 * WIN), 0)
    c = lax.broadcasted_iota(jnp.int32, (WIN, 2 * WIN), 1)
    mask = (c > r) & (c <= r + WIN) & ((c >= WIN) | jnp.logical_not(first))
    qb = q.astype(BF16)
    outs = []
    for hq in range(SWA_QH):
        g = hq // SWA_G
        qh = qb[:, hq * SWA_D:(hq + 1) * SWA_D]
        kh = kk[:, g * SWA_D:(g + 1) * SWA_D]
        vh = vv[:, g * SWA_D:(g + 1) * SWA_D]
        s = lax.dot_general(qh, kh, (((1,), (1,)), ((), ())), preferred_element_type=F32)
        s = jnp.where(mask, s, -jnp.inf)
        sink = sink_ref[hq]
        mx = jnp.maximum(jnp.max(s, axis=1, keepdims=True), sink)
        p = jnp.exp(s - mx)
        den = jnp.sum(p, axis=1, keepdims=True) + jnp.exp(sink - mx)
        outs.append(_dot(p.astype(BF16), vh) / den)
    o_ref[...] = jnp.concatenate(outs, axis=1).astype(BF16)


def _swa(proj, sinks, cos4, sin4, bsz, seq):
    m = proj.shape[0]
    nblk = seq // WIN
    prev = lambda i: jnp.maximum(i - 1, 0)
    kcol, vcol = P_SK // LANES, P_SV // LANES
    return pl.pallas_call(
        functools.partial(_swa_kernel, nblk=nblk),
        grid=(m // WIN,),
        in_specs=[pl.BlockSpec(memory_space=pltpu.SMEM),
                  pl.BlockSpec((WIN, BW), lambda i: (i, P_SQ // BW)),
                  pl.BlockSpec((WIN, LANES), lambda i: (i, kcol)),
                  pl.BlockSpec((WIN, LANES), lambda i: (prev(i), kcol)),
                  pl.BlockSpec((WIN, LANES), lambda i: (i, vcol)),
                  pl.BlockSpec((WIN, LANES), lambda i: (prev(i), vcol)),
                  pl.BlockSpec((WIN, LANES), lambda i: (i, 0)),
                  pl.BlockSpec((WIN, LANES), lambda i: (i, 0)),
                  pl.BlockSpec((WIN, LANES), lambda i: (prev(i), 0)),
                  pl.BlockSpec((WIN, LANES), lambda i: (prev(i), 0))],
        out_specs=pl.BlockSpec((WIN, BW), lambda i: (i, 0)),
        out_shape=jax.ShapeDtypeStruct((m, BW), BF16),
        compiler_params=_cparams(("arbitrary",)), name="swa",
    )(sinks, proj, proj, proj, proj, proj, cos4, sin4, cos4, sin4)


def _merge_kernel(oa_ref, ob_ref, oc_ref, ga_ref, gb_ref, gc_ref, w_ref, o_ref):
    acc = _sigmoid(ga_ref[...].astype(F32)) * _dot(oa_ref[...], w_ref[0])
    acc = acc + _sigmoid(gb_ref[...].astype(F32)) * _dot(ob_ref[...], w_ref[1])
    acc = acc + _sigmoid(gc_ref[...].astype(F32)) * _dot(oc_ref[...], w_ref[2])
    o_ref[...] = acc.astype(BF16)


def _merge(o_a, o_b, o_c, proj, w_branch):
    m = o_a.shape[0]
    tm, tn = 1024, 512
    gblk = lambda b: (lambda j, i: (i, (P_GATE + b * D) // tn + j))
    oblk = pl.BlockSpec((tm, BW), lambda j, i: (i, 0))
    return pl.pallas_call(
        _merge_kernel, grid=(D // tn, m // tm),
        in_specs=[oblk, oblk, oblk,
                  pl.BlockSpec((tm, tn), gblk(0)), pl.BlockSpec((tm, tn), gblk(1)), pl.BlockSpec((tm, tn), gblk(2)),
                  pl.BlockSpec((3, BW, tn), lambda j, i: (0, 0, j))],
        out_specs=pl.BlockSpec((tm, tn), lambda j, i: (i, j)),
        out_shape=jax.ShapeDtypeStruct((m, D), BF16),
        compiler_params=_cparams(("arbitrary", "arbitrary")), name="merge",
    )(o_a, o_b, o_c, proj, proj, proj, w_branch)


def _swiglu_acc(h, wg_ref, wu_ref, wd_ref, acc_ref):
    a = _dot(h, wg_ref[...])
    b = _dot(h, wu_ref[...])
    t = (a * _sigmoid(a) * b).astype(BF16)
    acc_ref[...] += _dot(t, wd_ref[...])


def _ffn_kernel(h_ref, wg_ref, wu_ref, wd_ref, x_ref, g_ref, o_ref, acc_ref, *, nf):
    f = pl.program_id(1)

    @pl.when(f == 0)
    def _():
        acc_ref[...] = jnp.zeros_like(acc_ref)

    _swiglu_acc(h_ref[...], wg_ref, wu_ref, wd_ref, acc_ref)

    @pl.when(f == nf - 1)
    def _():
        o_ref[...] = x_ref[...] + g_ref[...] * acc_ref[...]


def _ffn_dense(h, wg, wu, wd, x, gate, seq):
    m, d = h.shape
    ff = wg.shape[1]
    tm, tf = 512, 1024
    nf = ff // tf
    tpb = seq // tm
    return pl.pallas_call(
        functools.partial(_ffn_kernel, nf=nf), grid=(m // tm, nf),
        in_specs=[pl.BlockSpec((tm, d), lambda i, f: (i, 0)),
                  pl.BlockSpec((d, tf), lambda i, f: (0, f)),
                  pl.BlockSpec((d, tf), lambda i, f: (0, f)),
                  pl.BlockSpec((tf, d), lambda i, f: (f, 0)),
                  pl.BlockSpec((tm, d), lambda i, f: (i, 0)),
                  pl.BlockSpec((None, 1, d), lambda i, f: (i // tpb, 0, 0))],
        out_specs=pl.BlockSpec((tm, d), lambda i, f: (i, 0)),
        out_shape=jax.ShapeDtypeStruct((m, d), F32),
        scratch_shapes=[pltpu.VMEM((tm, d), F32)],
        compiler_params=_cparams(("arbitrary", "arbitrary")), name="ffn_dense",
    )(h, wg, wu, wd, x, gate)


def _row_copy(src_hbm, idx_ref, pos, dst, r, sem):
    return pltpu.make_async_copy(src_hbm.at[pl.ds(idx_ref[pos], 1), :], dst.at[pl.ds(r, 1), :], sem)


def _start_row_gather(src_hbm, idx_ref, base, n, dst, sem):
    def body(r, carry):
        _row_copy(src_hbm, idx_ref, base + r, dst, r, sem).start()
        return carry

    lax.fori_loop(0, n, body, 0, unroll=8)


def _wait_row_gather(src_hbm, n, dst, sem):
    pltpu.make_async_copy(src_hbm.at[pl.ds(0, n), :], dst, sem).wait()


def _ffn_group_kernel(te_ref, tv_ref, src_ref, h_hbm, wg_ref, wu_ref, wd_ref, o_ref,
                      buf, hbf_ref, acc_ref, sem, *, nf, tm, n_tiles):
    i = pl.program_id(0)
    f = pl.program_id(1)
    slot = i % 2
    live = tv_ref[i] > 0
    per = -(-tm // nf)
    last = tm - per * (nf - 1)

    @pl.when(f == 0)
    def _():
        @pl.when(i == 0)
        def _():
            _start_row_gather(h_hbm, src_ref, 0, tm, buf.at[0], sem.at[0])

        _wait_row_gather(h_hbm, tm, buf.at[slot], sem.at[slot])
        hbf_ref[...] = buf[slot].astype(BF16)
        acc_ref[...] = jnp.zeros_like(acc_ref)

    def prefetch(first, count):
        for j in range(count):
            r = first + j
            _row_copy(h_hbm, src_ref, (i + 1) * tm + r, buf.at[1 - slot], r, sem.at[1 - slot]).start()

    def step(count, compute):
        c0 = count // 3
        c1 = (count - c0) // 2
        first = f * per
        if not compute:
            prefetch(first, count)
            return
        h = hbf_ref[...]
        a = _dot(h, wg_ref[...])
        prefetch(first, c0)
        b = _dot(h, wu_ref[...])
        prefetch(first + c0, c1)
        t = (a * _sigmoid(a) * b).astype(BF16)
        acc_ref[...] += _dot(t, wd_ref[...])
        prefetch(first + c0 + c1, count - c0 - c1)

    for is_last, count in ((False, per), (True, last)):
        for compute in (True, False):
            cond = (f == nf - 1) if is_last else (f < nf - 1)
            cond = cond & (live if compute else jnp.logical_not(live))
            pl.when(cond)(functools.partial(step, count, compute))

    @pl.when(f == nf - 1)
    def _():
        o_ref[...] = acc_ref[...]

        @pl.when(i == n_tiles - 1)
        def _():
            _wait_row_gather(h_hbm, tm, buf.at[1 - slot], sem.at[1 - slot])


def _ffn_grouped(h, wg, wu, wd, tile_expert, tile_valid, src_rows, tm):
    rows = src_rows.shape[0] - tm
    d, ff = wg.shape[1], wg.shape[2]
    tf = 1024
    nf = ff // tf
    n_tiles = rows // tm

    def fsel(i, f, tv):
        return jnp.where(tv[i] > 0, f, nf - 1)

    grid_spec = pltpu.PrefetchScalarGridSpec(
        num_scalar_prefetch=3, grid=(n_tiles, nf),
        in_specs=[pl.BlockSpec(memory_space=pl.ANY),
                  pl.BlockSpec((None, d, tf), lambda i, f, te, tv, sr: (te[i], 0, fsel(i, f, tv))),
                  pl.BlockSpec((None, d, tf), lambda i, f, te, tv, sr: (te[i], 0, fsel(i, f, tv))),
                  pl.BlockSpec((None, tf, d), lambda i, f, te, tv, sr: (te[i], fsel(i, f, tv), 0))],
        out_specs=pl.BlockSpec((tm, d), lambda i, f, te, tv, sr: (i, 0)),
        scratch_shapes=[pltpu.VMEM((2, tm, d), F32),
                        pltpu.VMEM((tm, d), BF16),
                        pltpu.VMEM((tm, d), F32),
                        pltpu.SemaphoreType.DMA((2,))])
    return pl.pallas_call(
        functools.partial(_ffn_group_kernel, nf=nf, tm=tm, n_tiles=n_tiles), grid_spec=grid_spec,
        out_shape=jax.ShapeDtypeStruct((rows, d), F32),
        compiler_params=_cparams(("arbitrary", "arbitrary")), name="ffn_grouped",
    )(tile_expert, tile_valid, src_rows, h, wg, wu, wd)


COMBINE_TM = 256


def _combine_kernel(pos_ref, x_ref, y_hbm, r_ref, g_ref, gain_ref, o_ref, buf, sem, *, tm, nt, m, final):
    i = pl.program_id(0)
    slot = i % 2

    def start(tile, s):
        for k in range(2):
            _start_row_gather(y_hbm, pos_ref, k * m + tile * tm, tm, buf.at[s, k], sem.at[s])

    @pl.when(i == 0)
    def _():
        start(0, 0)

    @pl.when(i + 1 < nt)
    def _():
        start(jnp.minimum(i + 1, nt - 1), 1 - slot)

    for k in range(2):
        _wait_row_gather(y_hbm, tm, buf.at[slot, k], sem.at[slot])
    w0 = r_ref[:, 2:3]
    w1 = r_ref[:, 3:4]
    xn = x_ref[...] + g_ref[...] * (w0 * buf[slot, 0] + w1 * buf[slot, 1])
    if final:
        xn = xn * lax.rsqrt(jnp.mean(xn * xn, axis=-1, keepdims=True) + EPS) * gain_ref[...]
    o_ref[...] = xn


def _combine(x, y, pos_rows, route, gate, seq, final_gain=None):
    m, d = x.shape
    tm = COMBINE_TM
    nt = m // tm
    tpb = seq // tm
    final = final_gain is not None
    gain = (final_gain if final else jnp.ones((d,), F32)).reshape(1, d)
    grid_spec = pltpu.PrefetchScalarGridSpec(
        num_scalar_prefetch=1, grid=(nt,),
        in_specs=[pl.BlockSpec((tm, d), lambda i, p: (i, 0)),
                  pl.BlockSpec(memory_space=pl.ANY),
                  pl.BlockSpec((tm, LANES), lambda i, p: (i, 0)),
                  pl.BlockSpec((None, 1, d), lambda i, p: (i // tpb, 0, 0)),
                  pl.BlockSpec((1, d), lambda i, p: (0, 0))],
        out_specs=pl.BlockSpec((tm, d), lambda i, p: (i, 0)),
        scratch_shapes=[pltpu.VMEM((2, 2, tm, d), F32), pltpu.SemaphoreType.DMA((2,))])
    return pl.pallas_call(
        functools.partial(_combine_kernel, tm=tm, nt=nt, m=m, final=final), grid_spec=grid_spec,
        out_shape=jax.ShapeDtypeStruct((m, d), F32),
        compiler_params=_cparams(("arbitrary",)), name="combine",
    )(pos_rows, x, y, route, gate, gain)


def _final_norm_kernel(x_ref, gain_ref, o_ref):
    x = x_ref[...]
    o_ref[...] = x * lax.rsqrt(jnp.mean(x * x, axis=-1, keepdims=True) + EPS) * gain_ref[...]


def _final_norm(x, gain):
    m, d = x.shape
    tm = 512
    blk = pl.BlockSpec((tm, d), lambda i: (i, 0))
    return pl.pallas_call(
        _final_norm_kernel, grid=(m // tm,),
        in_specs=[blk, pl.BlockSpec((1, d), lambda i: (0, 0))],
        out_specs=blk, out_shape=jax.ShapeDtypeStruct((m, d), F32),
        compiler_params=_cparams(("arbitrary",)), name="final_norm",
    )(x, gain.reshape(1, d))


MOE_TM = 512


def _moe(x, h, route, counts, wg, wu, wd, gate, seq, final_gain=None):
    m, d = x.shape
    tm = MOE_TM
    rows = 2 * m + NE * tm
    n_tiles = rows // tm
    e_flat = route[:, 0:2].astype(jnp.int32).reshape(-1)
    rank = route[:, 4:6].astype(jnp.int32).reshape(-1)
    counts = counts[0, :NE].astype(jnp.int32)
    padded = ((counts + tm - 1) // tm) * tm
    ends = jnp.cumsum(padded)
    starts = ends - padded
    pos = jnp.take(starts, e_flat) + rank
    src_token = jnp.zeros((rows + tm,), jnp.int32).at[pos].set(jnp.arange(2 * m, dtype=jnp.int32) // 2)
    tile_start = jnp.arange(n_tiles, dtype=jnp.int32) * tm
    tile_expert = jnp.minimum(jnp.sum((tile_start[:, None] >= ends[None, :]).astype(jnp.int32), axis=1), NE - 1)
    tile_valid = (tile_start < ends[-1]).astype(jnp.int32)

    ys = _ffn_grouped(h, wg, wu, wd, tile_expert, tile_valid, src_token, tm)
    pos_rows = pos.reshape(m, 2).T.reshape(-1)
    return _combine(x, ys, pos_rows, route, gate, seq, final_gain)


REGROUP_TN = 256
REGROUP_SEGS = ((P_GATE, 8472, 6144), (P_DN, 0, 4096), (P_FQ, 4112, 3072), (P_SQ, 7192, 1280))


def _regroup_kernel(row_ref, scale_ref, w_ref, o_ref):
    o_ref[...] = (w_ref[0].T * scale_ref[pl.program_id(0)]).astype(BF16)


def _regroup_w_in(w_in_t, layer):
    d = w_in_t.shape[2]
    tn = REGROUP_TN
    rows, scale = [], []
    for dst, src, width in REGROUP_SEGS:
        for t in range(width // tn):
            assert (src + t * tn) % 8 == 0
            rows.append((src + t * tn) // 8)
            o = dst + t * tn
            scale.append(HD ** -0.5 if P_FQ <= o < P_FK else SWA_D ** -0.5 if P_SQ <= o < P_SK else 1.0)
    n_tiles = len(rows)
    assert n_tiles * tn == NP
    grid_spec = pltpu.PrefetchScalarGridSpec(
        num_scalar_prefetch=2, grid=(n_tiles,),
        in_specs=[pl.BlockSpec((pl.Element(1), pl.Element(tn), pl.Element(d)),
                               lambda j, rw, sc: (layer, rw[j] * 8, 0))],
        out_specs=pl.BlockSpec((d, tn), lambda j, rw, sc: (0, j)))
    return pl.pallas_call(
        _regroup_kernel, grid_spec=grid_spec,
        out_shape=jax.ShapeDtypeStruct((d, NP), BF16),
        compiler_params=_cparams(("arbitrary",)), name="regroup_w_in",
    )(jnp.asarray(rows, jnp.int32), jnp.asarray(scale, F32), w_in_t)


def _small_w_kernel(a_ref, b_ref, o_ref):
    rows = jnp.concatenate([a_ref[...], b_ref[...],
                            jnp.zeros((LANES - L_F - NH, a_ref.shape[1]), F32)], axis=0)
    o_ref[...] = rows.T.astype(BF16)


def _small_w_in(w_in_t, layer):
    d = w_in_t.shape[2]
    return pl.pallas_call(
        _small_w_kernel, grid=(1,),
        in_specs=[pl.BlockSpec((None, L_F, d), lambda i: (layer, 4096 // L_F, 0)),
                  pl.BlockSpec((None, NH, d), lambda i: (layer, 7184 // NH, 0))],
        out_specs=pl.BlockSpec((d, LANES), lambda i: (0, 0)),
        out_shape=jax.ShapeDtypeStruct((d, LANES), BF16),
        compiler_params=_cparams(("arbitrary",)), name="small_w_in",
    )(w_in_t, w_in_t)


def kernel(x, c, positions, w_ada, b_ada, norm_mix, w_in, conv_w, dn_a_log, dn_dt_bias, dn_norm,
           fox_b_forget, swa_sinks, w_branch, w_out, norm_ffn, ffn_w_gate, ffn_w_up, ffn_w_down,
           moe_router, moe_w_gate, moe_w_up, moe_w_down, final_norm):
    bsz, seq, d = x.shape
    depth = w_ada.shape[0]
    m = bsz * seq
    xf = x.reshape(m, d)

    inv_freq = 10000.0 ** (-jnp.arange(0, SWA_D, 2, dtype=F32) / SWA_D)
    ang = positions.astype(F32).reshape(m, 1) * inv_freq[None, :]
    cos, sin = jnp.cos(ang), jnp.sin(ang)
    cos4 = jnp.tile(cos, (1, 4))
    sin4 = jnp.tile(jnp.concatenate([-sin, sin], axis=1), (1, 2))

    w_in_t = jnp.swapaxes(w_in, 1, 2)
    c_pad = jnp.pad(c, ((0, 8 - bsz), (0, 0)))
    mod = _ada(c_pad, w_ada, b_ada)[:, :bsz].reshape(depth, bsz, 6, 1, d)

    for layer in range(depth):
        sh1, sc1, g1, sh2, sc2, g2 = (mod[layer, :, t] for t in range(6))
        w_main = _regroup_w_in(w_in_t, layer)
        w_small = _small_w_in(w_in_t, layer)
        h, small = _norm_modulate(xf, norm_mix[layer], sc1, sh1, seq, w_small=w_small)
        proj = _matmul(h, w_main, BF16, 1024, 2432, "in_proj")
        par = jnp.zeros((8, LANES), F32)
        par = par.at[0, L_G:L_G + NH].set(dn_a_log[layer])
        par = par.at[1, L_G:L_G + NH].set(dn_dt_bias[layer])
        par = par.at[1, L_F:L_F + NH].set(fox_b_forget[layer])
        col, rowf, rowg = _prep(small, par, bsz, seq)
        conv_w8 = jnp.pad(conv_w[layer], ((0, 4), (0, 0)))
        o_a = _delta(proj, conv_w8, col, rowg, dn_norm[layer], bsz, seq)
        o_b = _fox(proj, rowf, bsz, seq)
        o_c = _swa(proj, swa_sinks[layer], cos4, sin4, bsz, seq)
        merged = _merge(o_a, o_b, o_c, proj, w_branch[layer].astype(BF16))
        xf = _matmul_resid(merged, w_out[layer].astype(BF16), xf, g1, seq, 512, 2048, "out_proj")
        if layer % 2 == 0:
            i = layer // 2
            h2 = _norm_modulate(xf, norm_ffn[layer], sc2, sh2, seq)
            xf = _ffn_dense(h2, ffn_w_gate[i].astype(BF16), ffn_w_up[i].astype(BF16),
                            ffn_w_down[i].astype(BF16), xf, g2, seq)
        else:
            i = layer // 2
            h2, route, counts = _norm_modulate(xf, norm_ffn[layer], sc2, sh2, seq, w_router=moe_router[i])
            last = layer == depth - 1
            xf = _moe(xf, h2, route, counts, moe_w_gate[i].astype(BF16), moe_w_up[i].astype(BF16),
                      moe_w_down[i].astype(BF16), g2, seq, final_gain=final_norm if last else None)
            if last:
                return xf.reshape(bsz, seq, d)
    return _final_norm(xf, final_norm).reshape(bsz, seq, d)
```

```python
import functools

import jax
import jax.numpy as jnp
from jax import lax
from jax.experimental import pallas as pl
from jax.experimental.pallas import tpu as pltpu

F32 = jnp.float32
BF16 = jnp.bfloat16
EPS = 1e-6

D = 2048
HD = 128
NH = 8
BW = 1024
CHUNK = 64
SWA_D = 64
SWA_QH = 16
SWA_G = 8
WIN = 128
NE = 8
LANES = 128

P_GATE = 0
P_DN = 6144
P_DNZ = 9216
P_FQ, P_FK, P_FV = 10240, 11264, 12288
P_SQ, P_SK, P_SV = 13312, 14336, 14464
NP = 14592
L_BETA, L_G, L_F = 0, 8, 16

VMEM_LIMIT = 56 * 1024 * 1024


def _cparams(sem):
    return pltpu.CompilerParams(dimension_semantics=sem, vmem_limit_bytes=VMEM_LIMIT)


def _sigmoid(x):
    return 1.0 / (1.0 + jnp.exp(-x))


def _split3(x):
    hi = x.astype(BF16)
    r = x - hi.astype(F32)
    mid = r.astype(BF16)
    lo = (r - mid.astype(F32)).astype(BF16)
    return hi, mid, lo


def _dot(a, b):
    return jnp.dot(a, b, preferred_element_type=F32)


def _dot_hi(a, b):
    a0 = a.astype(BF16)
    a1 = (a - a0.astype(F32)).astype(BF16)
    b0 = b.astype(BF16)
    b1 = (b - b0.astype(F32)).astype(BF16)
    return _dot(a0, b0) + (_dot(a0, b1) + _dot(a1, b0))


def _ada_kernel(c_ref, w_ref, b_ref, o_ref):
    c = c_ref[...]
    o_ref[...] = _dot_hi(c * _sigmoid(c), w_ref[...]) + b_ref[...]


def _ada(c_pad, w_ada, b_ada):
    depth, d, n = w_ada.shape
    tn = 1536
    return pl.pallas_call(
        _ada_kernel,
        grid=(depth, n // tn),
        in_specs=[pl.BlockSpec((8, d), lambda l, j: (0, 0)),
                  pl.BlockSpec((None, d, tn), lambda l, j: (l, 0, j)),
                  pl.BlockSpec((None, 1, tn), lambda l, j: (l, 0, j))],
        out_specs=pl.BlockSpec((None, 8, tn), lambda l, j: (l, 0, j)),
        out_shape=jax.ShapeDtypeStruct((depth, 8, n), F32),
        compiler_params=_cparams(("arbitrary", "arbitrary")),
        name="ada",
    )(c_pad, w_ada, b_ada.reshape(depth, 1, n))


def _normmod(x_ref, gain_ref, sc_ref, sh_ref):
    x = x_ref[...]
    ms = jnp.mean(x * x, axis=-1, keepdims=True)
    y = x * lax.rsqrt(ms + EPS) * gain_ref[...]
    return y * (1.0 + sc_ref[...]) + sh_ref[...]


def _normmod_kernel(x_ref, gain_ref, sc_ref, sh_ref, h_ref):
    h_ref[...] = _normmod(x_ref, gain_ref, sc_ref, sh_ref).astype(BF16)


def _normmod_small_kernel(x_ref, gain_ref, sc_ref, sh_ref, ws_ref, h_ref, s_ref):
    h = _normmod(x_ref, gain_ref, sc_ref, sh_ref).astype(BF16)
    h_ref[...] = h
    s_ref[...] = _dot(h, ws_ref[...])


def _normmod_route_kernel(x_ref, gain_ref, sc_ref, sh_ref, wr_ref, h_ref, r_ref, cnt_ref, carry_ref):
    @pl.when(pl.program_id(0) == 0)
    def _():
        carry_ref[...] = jnp.zeros_like(carry_ref)

    h = _normmod(x_ref, gain_ref, sc_ref, sh_ref)
    h_ref[...] = h
    logits = _dot_hi(h, wr_ref[...])
    lane = lax.broadcasted_iota(jnp.int32, logits.shape, 1)
    neg = jnp.float32(-jnp.inf)
    l1 = jnp.where(lane < NE, logits, neg)
    m1 = jnp.max(l1, axis=-1, keepdims=True)
    i1 = jnp.min(jnp.where(l1 == m1, lane, LANES), axis=-1, keepdims=True)
    l2 = jnp.where(lane == i1, neg, l1)
    m2 = jnp.max(l2, axis=-1, keepdims=True)
    i2 = jnp.min(jnp.where(l2 == m2, lane, LANES), axis=-1, keepdims=True)
    e = jnp.exp(m2 - m1)
    w1 = 1.0 / (1.0 + e)
    w2 = e / (1.0 + e)
    tm = logits.shape[0]
    picked = jnp.where((lane == i1) | (lane == i2), 1.0, 0.0)
    rr = lax.broadcasted_iota(jnp.int32, (tm, tm), 0)
    cc = lax.broadcasted_iota(jnp.int32, (tm, tm), 1)
    before = _dot(jnp.where(cc < rr, 1.0, 0.0).astype(BF16), picked.astype(BF16)) + carry_ref[...]
    rank1 = jnp.sum(jnp.where(lane == i1, before, 0.0), axis=-1, keepdims=True)
    rank2 = jnp.sum(jnp.where(lane == i2, before, 0.0), axis=-1, keepdims=True)
    carry_ref[...] += jnp.sum(picked, axis=0, keepdims=True)
    cnt_ref[...] = jnp.broadcast_to(carry_ref[...], cnt_ref.shape)
    vals = (i1.astype(F32), i2.astype(F32), w1, w2, rank1, rank2)
    out = jnp.zeros_like(logits)
    for pos, v in enumerate(vals):
        out = jnp.where(lane == pos, v, out)
    r_ref[...] = out


def _norm_modulate(x, gain, sc, sh, seq, w_router=None, w_small=None):
    m, d = x.shape
    tm = 512
    tpb = seq // tm
    in_specs = [pl.BlockSpec((tm, d), lambda i: (i, 0)),
                pl.BlockSpec((1, d), lambda i: (0, 0)),
                pl.BlockSpec((None, 1, d), lambda i: (i // tpb, 0, 0)),
                pl.BlockSpec((None, 1, d), lambda i: (i // tpb, 0, 0))]
    args = [x, gain.reshape(1, d), sc, sh]
    if w_small is not None:
        return pl.pallas_call(
            _normmod_small_kernel, grid=(m // tm,),
            in_specs=in_specs + [pl.BlockSpec((d, LANES), lambda i: (0, 0))],
            out_specs=[pl.BlockSpec((tm, d), lambda i: (i, 0)), pl.BlockSpec((tm, LANES), lambda i: (i, 0))],
            out_shape=[jax.ShapeDtypeStruct((m, d), BF16), jax.ShapeDtypeStruct((m, LANES), F32)],
            compiler_params=_cparams(("arbitrary",)), name="normmod_small",
        )(*args, w_small)
    if w_router is None:
        return pl.pallas_call(
            _normmod_kernel, grid=(m // tm,), in_specs=in_specs,
            out_specs=pl.BlockSpec((tm, d), lambda i: (i, 0)),
            out_shape=jax.ShapeDtypeStruct((m, d), BF16),
            compiler_params=_cparams(("arbitrary",)), name="normmod",
        )(*args)
    wr = jnp.pad(w_router, ((0, 0), (0, LANES - NE)))
    return pl.pallas_call(
        _normmod_route_kernel, grid=(m // tm,),
        in_specs=in_specs + [pl.BlockSpec((d, LANES), lambda i: (0, 0))],
        out_specs=[pl.BlockSpec((tm, d), lambda i: (i, 0)),
                   pl.BlockSpec((tm, LANES), lambda i: (i, 0)),
                   pl.BlockSpec((8, LANES), lambda i: (0, 0))],
        out_shape=[jax.ShapeDtypeStruct((m, d), F32),
                   jax.ShapeDtypeStruct((m, LANES), F32),
                   jax.ShapeDtypeStruct((8, LANES), F32)],
        scratch_shapes=[pltpu.VMEM((1, LANES), F32)],
        compiler_params=_cparams(("arbitrary",)), name="normmod_route",
    )(*args, wr)


def _mm_kernel(a_ref, w_ref, o_ref):
    o_ref[...] = _dot(a_ref[...], w_ref[...]).astype(o_ref.dtype)


def _matmul(a, w, out_dtype, tm, tn, name):
    m, k = a.shape
    n = w.shape[1]
    return pl.pallas_call(
        _mm_kernel, grid=(n // tn, m // tm),
        in_specs=[pl.BlockSpec((tm, k), lambda j, i: (i, 0)),
                  pl.BlockSpec((k, tn), lambda j, i: (0, j))],
        out_specs=pl.BlockSpec((tm, tn), lambda j, i: (i, j)),
        out_shape=jax.ShapeDtypeStruct((m, n), out_dtype),
        compiler_params=_cparams(("arbitrary", "arbitrary")), name=name,
    )(a, w)


def _mm_resid_kernel(a_ref, w_ref, x_ref, g_ref, o_ref):
    o_ref[...] = x_ref[...] + g_ref[...] * _dot(a_ref[...], w_ref[...])


def _matmul_resid(a, w, x, gate, seq, tm, tn, name):
    m, k = a.shape
    n = w.shape[1]
    tpb = seq // tm
    return pl.pallas_call(
        _mm_resid_kernel, grid=(n // tn, m // tm),
        in_specs=[pl.BlockSpec((tm, k), lambda j, i: (i, 0)),
                  pl.BlockSpec((k, tn), lambda j, i: (0, j)),
                  pl.BlockSpec((tm, tn), lambda j, i: (i, j)),
                  pl.BlockSpec((None, 1, tn), lambda j, i: (i // tpb, 0, j))],
        out_specs=pl.BlockSpec((tm, tn), lambda j, i: (i, j)),
        out_shape=jax.ShapeDtypeStruct((m, n), F32),
        compiler_params=_cparams(("arbitrary", "arbitrary")), name=name,
    )(a, w, x, gate)


def _prep_kernel(x_ref, par_ref, col_ref, rowf_ref, rowg_ref, carry_ref, *, ts):
    @pl.when(pl.program_id(1) == 0)
    def _():
        carry_ref[...] = jnp.zeros_like(carry_ref)

    x = x_ref[...]
    lane = lax.broadcasted_iota(jnp.int32, x.shape, 1)
    is_g = (lane >= L_G) & (lane < L_G + NH)
    is_f = (lane >= L_F) & (lane < L_F + NH)
    a_rate = jnp.exp(par_ref[0:1, :])
    z = x + par_ref[1:2, :]
    t = jnp.log1p(jnp.exp(-jnp.abs(z)))
    sig = _sigmoid(x)
    g = -a_rate * (jnp.maximum(z, 0.0) + t)
    logf = jnp.minimum(z, 0.0) - t
    vals = jnp.where(is_g, g, jnp.where(is_f, logf, 0.0))
    r = lax.broadcasted_iota(jnp.int32, (ts, ts), 0)
    c = lax.broadcasted_iota(jnp.int32, (ts, ts), 1)
    tri_full = jnp.where(c <= r, 1.0, 0.0).astype(BF16)
    tri_blk = jnp.where((c <= r) & ((r // CHUNK) == (c // CHUNK)), 1.0, 0.0).astype(BF16)
    v0, v1, v2 = _split3(vals)
    cs_full = _dot(tri_full, v0) + _dot(tri_full, v1) + _dot(tri_full, v2)
    cs_blk = _dot(tri_blk, v0) + _dot(tri_blk, v1) + _dot(tri_blk, v2)
    cum = cs_full + carry_ref[...]
    carry_ref[...] = cum[ts - 1:ts, :]
    out = jnp.where(lane < NH, sig, jnp.where(is_g, cs_blk, jnp.where(is_f, cum, 0.0)))
    col_ref[...] = out
    out_t = out.T
    rowf_ref[...] = out_t[L_F:L_F + NH, :]
    for ci in range(ts // CHUNK):
        rowg_ref[ci] = out_t[0:32, ci * CHUNK:(ci + 1) * CHUNK]


def _prep(small, par, bsz, seq):
    m = small.shape[0]
    ts = 256
    nt = seq // ts
    return pl.pallas_call(
        functools.partial(_prep_kernel, ts=ts),
        grid=(bsz, nt),
        in_specs=[pl.BlockSpec((ts, LANES), lambda b, j: (b * nt + j, 0)),
                  pl.BlockSpec((8, LANES), lambda b, j: (0, 0))],
        out_specs=[pl.BlockSpec((ts, LANES), lambda b, j: (b * nt + j, 0)),
                   pl.BlockSpec((NH, ts), lambda b, j: (0, b * nt + j)),
                   pl.BlockSpec((ts // CHUNK, 32, CHUNK), lambda b, j: (b * nt + j, 0, 0))],
        out_shape=[jax.ShapeDtypeStruct((m, LANES), F32),
                   jax.ShapeDtypeStruct((NH, m), F32),
                   jax.ShapeDtypeStruct((m // CHUNK, 32, CHUNK), F32)],
        scratch_shapes=[pltpu.VMEM((1, LANES), F32)],
        compiler_params=_cparams(("arbitrary", "arbitrary")), name="prep",
    )(small, par)


HALO = 16


def _bmm(a, b):
    return lax.dot_general(a.astype(BF16), b.astype(BF16), (((2,), (1,)), ((0,), (0,))),
                           preferred_element_type=F32)


def _bmm_nt(a, b):
    return lax.dot_general(a.astype(BF16), b.astype(BF16), (((2,), (2,)), ((0,), (0,))),
                           preferred_element_type=F32)


def _delta_kernel(qkv_ref, halo_ref, z_ref, cw_ref, col_ref, rowg_ref, gain_ref, o_ref, s_ref, *, nc):
    j = pl.program_id(1)

    @pl.when(j == 0)
    def _():
        s_ref[...] = jnp.zeros_like(s_ref)

    ts = nc * CHUNK
    n = NH * nc
    xb = qkv_ref[...]
    halo = halo_ref[...]
    halo = jnp.where(j == 0, jnp.zeros_like(halo), halo)
    xx = jnp.concatenate([halo, xb], axis=0)
    rr = lax.broadcasted_iota(jnp.int32, (3 * ts, HALO + ts), 0)
    cc = lax.broadcasted_iota(jnp.int32, (3 * ts, HALO + ts), 1)
    sel = jnp.where(cc == (rr % ts) + HALO - 1 - rr // ts, 1.0, 0.0).astype(BF16)
    shifted = _dot(sel, xx)
    cw = cw_ref[...]
    acc = xb.astype(F32) * cw[3:4, :]
    for dlt in (1, 2, 3):
        acc = acc + shifted[(dlt - 1) * ts:dlt * ts, :] * cw[3 - dlt:4 - dlt, :]
    act = acc * _sigmoid(acc)

    cv = col_ref[...]
    rg = rowg_ref[...]
    qs, ks, vs, betas, gcs, grs = [], [], [], [], [], []
    for h in range(NH):
        qh = act[:, h * HD:(h + 1) * HD]
        kh = act[:, BW + h * HD:BW + (h + 1) * HD]
        vh = act[:, 2 * BW + h * HD:2 * BW + (h + 1) * HD]
        qh = qh * (lax.rsqrt(jnp.sum(qh * qh, axis=-1, keepdims=True) + EPS) * (HD ** -0.5))
        kh = kh * lax.rsqrt(jnp.sum(kh * kh, axis=-1, keepdims=True) + EPS)
        qs.append(qh.reshape(nc, CHUNK, HD))
        ks.append(kh.reshape(nc, CHUNK, HD))
        vs.append(vh.reshape(nc, CHUNK, HD))
        betas.append(cv[:, L_BETA + h:L_BETA + h + 1].reshape(nc, CHUNK, 1))
        gcs.append(cv[:, L_G + h:L_G + h + 1].reshape(nc, CHUNK, 1))
        grs.append(rg[:, L_G + h:L_G + h + 1, :])
    q = jnp.concatenate(qs, axis=0)
    k = jnp.concatenate(ks, axis=0)
    v = jnp.concatenate(vs, axis=0)
    beta = jnp.concatenate(betas, axis=0)
    gc = jnp.concatenate(gcs, axis=0)
    gr = jnp.concatenate(grs, axis=0)

    ri = lax.broadcasted_iota(jnp.int32, (CHUNK, CHUNK), 0)
    ci = lax.broadcasted_iota(jnp.int32, (CHUNK, CHUNK), 1)
    causal = (ci <= ri)[None]
    strict = (ci < ri)[None]
    decay = jnp.exp(jnp.where(causal, gc - gr, -jnp.inf))
    kb = k * beta
    both = _bmm_nt(jnp.concatenate([kb, q], axis=1), k)
    lower = jnp.where(strict, both[:, :CHUNK] * decay, 0.0)
    qk = both[:, CHUNK:] * decay

    eye = jnp.where(ci == ri, 1.0, 0.0)[None]
    t_inv = eye - lower
    pw = _bmm(lower, lower)
    for it in range(5):
        t_inv = t_inv + _bmm(t_inv, pw)
        if it < 4:
            pw = _bmm(pw, pw)

    eg = jnp.exp(gc)
    rhs = jnp.concatenate([v * beta, kb * eg], axis=-1)
    sol = _bmm(t_inv, rhs)
    u = sol[:, :, :HD]
    w = sol[:, :, HD:]
    q_dec = q * eg
    g_end = gc[:, CHUNK - 1:CHUNK, :]
    k_dec = k * jnp.exp(g_end - gc)
    g_last = jnp.exp(g_end)

    def pick(t, c):
        return t.reshape((NH, nc) + t.shape[1:])[:, c]

    state = s_ref[...]
    outs = []
    for c in range(nc):
        ws_qs = _bmm(jnp.concatenate([pick(w, c), pick(q_dec, c)], axis=1), state)
        v_new = pick(u, c) - ws_qs[:, :CHUNK]
        o_c = ws_qs[:, CHUNK:] + _bmm(pick(qk, c), v_new)
        kd_t = jnp.swapaxes(pick(k_dec, c), 1, 2)
        state = state * pick(g_last, c) + _bmm(kd_t, v_new)
        outs.append(o_c)
    s_ref[...] = state

    gain = gain_ref[...]
    zf = z_ref[...].astype(F32)
    cols = []
    for h in range(NH):
        oh = jnp.concatenate([outs[c][h] for c in range(nc)], axis=0)
        oh = oh * lax.rsqrt(jnp.mean(oh * oh, axis=-1, keepdims=True) + EPS) * gain
        zh = zf[:, h * HD:(h + 1) * HD]
        cols.append(oh * (zh * _sigmoid(zh)))
    o_ref[...] = jnp.concatenate(cols, axis=1).astype(BF16)


def _delta(proj, conv_w8, col, rowg, gain, bsz, seq):
    m = proj.shape[0]
    nc = 4
    ts = nc * CHUNK
    nt = seq // ts
    qkv_blk = P_DN // (3 * BW)
    return pl.pallas_call(
        functools.partial(_delta_kernel, nc=nc),
        grid=(bsz, nt),
        in_specs=[pl.BlockSpec((ts, 3 * BW), lambda b, j: (b * nt + j, qkv_blk)),
                  pl.BlockSpec((HALO, 3 * BW), lambda b, j: (jnp.maximum((b * nt + j) * (ts // HALO) - 1, 0), qkv_blk)),
                  pl.BlockSpec((ts, BW), lambda b, j: (b * nt + j, P_DNZ // BW)),
                  pl.BlockSpec((8, 3 * BW), lambda b, j: (0, 0)),
                  pl.BlockSpec((ts, LANES), lambda b, j: (b * nt + j, 0)),
                  pl.BlockSpec((nc, 32, CHUNK), lambda b, j: (b * nt + j, 0, 0)),
                  pl.BlockSpec((1, HD), lambda b, j: (0, 0))],
        out_specs=pl.BlockSpec((ts, BW), lambda b, j: (b * nt + j, 0)),
        out_shape=jax.ShapeDtypeStruct((m, BW), BF16),
        scratch_shapes=[pltpu.VMEM((NH, HD, HD), F32)],
        compiler_params=_cparams(("arbitrary", "arbitrary")), name="delta",
    )(proj, proj, proj, conv_w8, col, rowg, gain.reshape(1, HD))


FOX_T = 512
FOX_HPS = 4


def _fox_kernel(q_ref, k_ref, v_ref, crow_ref, o_ref, vx_ref, m_ref, acc_ref):
    t = FOX_T
    hp = pl.program_id(1)
    qi = pl.program_id(2)

    @pl.when(qi == 0)
    def _():
        lane = lax.broadcasted_iota(jnp.int32, (v_ref.shape[0], HD), 1)
        ones_col = jnp.where(lane == 0, 1.0, 0.0).astype(BF16)
        for e in range(FOX_HPS):
            vx_ref[e, :, :HD] = v_ref[:, e * HD:(e + 1) * HD]
            vx_ref[e, :, HD:] = ones_col

    m_ref[...] = jnp.full_like(m_ref, -jnp.inf)
    acc_ref[...] = jnp.zeros_like(acc_ref)

    def block(ki, masked):
        off = pl.multiple_of(ki * t, t)
        for e in range(FOX_HPS):
            s = lax.dot_general(q_ref[:, e * HD:(e + 1) * HD], k_ref[pl.ds(off, t), e * HD:(e + 1) * HD],
                                (((1,), (1,)), ((), ())), preferred_element_type=F32)
            s = s - crow_ref[pl.ds(FOX_HPS * hp + e, 1), pl.ds(off, t)]
            if masked:
                row = lax.broadcasted_iota(jnp.int32, (t, t), 0)
                col = lax.broadcasted_iota(jnp.int32, (t, t), 1)
                s = jnp.where(col <= row, s, -jnp.inf)
            m_prev = m_ref[e]
            m_new = jnp.maximum(m_prev, jnp.max(s, axis=1, keepdims=True))
            alpha = jnp.exp(m_prev - m_new)
            p = jnp.exp((s - jnp.tile(m_new, (1, t // LANES))).astype(BF16))
            acc_ref[e] = jnp.tile(alpha, (1, 2)) * acc_ref[e] + _dot(p, vx_ref[e, pl.ds(off, t), :])
            m_ref[e] = m_new

    def body(ki, carry):
        block(ki, False)
        return carry

    lax.fori_loop(0, qi, body, 0)
    block(qi, True)
    for e in range(FOX_HPS):
        o_ref[:, e * HD:(e + 1) * HD] = (acc_ref[e, :, :HD] / acc_ref[e, :, HD:HD + 1]).astype(BF16)


def _fox(proj, rowf, bsz, seq):
    m = proj.shape[0]
    t = FOX_T
    nq = seq // t
    w = FOX_HPS * HD
    return pl.pallas_call(
        _fox_kernel,
        grid=(bsz, NH // FOX_HPS, nq),
        in_specs=[pl.BlockSpec((t, w), lambda b, h, qi: (b * nq + qi, P_FQ // w + h)),
                  pl.BlockSpec((seq, w), lambda b, h, qi: (b, P_FK // w + h)),
                  pl.BlockSpec((seq, w), lambda b, h, qi: (b, P_FV // w + h)),
                  pl.BlockSpec((NH, seq), lambda b, h, qi: (0, b))],
        out_specs=pl.BlockSpec((t, w), lambda b, h, qi: (b * nq + qi, h)),
        out_shape=jax.ShapeDtypeStruct((m, BW), BF16),
        scratch_shapes=[pltpu.VMEM((FOX_HPS, seq, 2 * HD), BF16), pltpu.VMEM((FOX_HPS, t, HD), F32),
                        pltpu.VMEM((FOX_HPS, t, 2 * HD), F32)],
        compiler_params=_cparams(("arbitrary", "arbitrary", "arbitrary")), name="fox",
    )(proj, proj, proj, rowf)


def _swap_halves(x):
    w = x.shape[-1]
    lane = lax.broadcasted_iota(jnp.int32, x.shape, x.ndim - 1)
    return jnp.where((lane % SWA_D) < SWA_D // 2, pltpu.roll(x, w - SWA_D // 2, axis=x.ndim - 1),
                     pltpu.roll(x, SWA_D // 2, axis=x.ndim - 1))


def _swa_kernel(sink_ref, q_ref, kc_ref, kp_ref, vc_ref, vp_ref, cc_ref, sc_ref, cp_ref, sp_ref, o_ref, *, nblk):
    i = pl.program_id(0)
    first = (i % nblk) == 0
    cos_c, sin_c = cc_ref[...], sc_ref[...]
    q = q_ref[...].astype(F32)
    q = q * jnp.tile(cos_c, (1, SWA_QH // 2)) + _swap_halves(q) * jnp.tile(sin_c, (1, SWA_QH // 2))
    kc = kc_ref[...].astype(F32)
    kc = kc * cos_c + _swap_halves(kc) * sin_c
    kp = kp_ref[...].astype(F32)
    kp = kp * cp_ref[...] + _swap_halves(kp) * sp_ref[...]
    kk = jnp.concatenate([kp, kc], axis=0).astype(BF16)
    vv = jnp.concatenate([vp_ref[...], vc_ref[...]], axis=0)
    r = lax.broadcasted_iota(jnp.int32, (WIN, 2 * WIN), 0)
    c = lax.broadcasted_iota(jnp.int32, (WIN, 2 * WIN), 1)
    mask = (c > r) & (c <= r + WIN) & ((c >= WIN) | jnp.logical_not(first))
    qb = q.astype(BF16)
    outs = []
    for hq in range(SWA_QH):
        g = hq // SWA_G
        qh = qb[:, hq * SWA_D:(hq + 1) * SWA_D]
        kh = kk[:, g * SWA_D:(g + 1) * SWA_D]
        vh = vv[:, g * SWA_D:(g + 1) * SWA_D]
        s = lax.dot_general(qh, kh, (((1,), (1,)), ((), ())), preferred_element_type=F32)
        s = jnp.where(mask, s, -jnp.inf)
        sink = sink_ref[hq]
        mx = jnp.maximum(jnp.max(s, axis=1, keepdims=True), sink)
        p = jnp.exp(s - mx)
        den = jnp.sum(p, axis=1, keepdims=True) + jnp.exp(sink - mx)
        outs.append(_dot(p.astype(BF16), vh) / den)
    o_ref[...] = jnp.concatenate(outs, axis=1).astype(BF16)


def _swa(proj, sinks, cos4, sin4, bsz, seq):
    m = proj.shape[0]
    nblk = seq // WIN
    prev = lambda i: jnp.maximum(i - 1, 0)
    kcol, vcol = P_SK // LANES, P_SV // LANES
    return pl.pallas_call(
        functools.partial(_swa_kernel, nblk=nblk),
        grid=(m // WIN,),
        in_specs=[pl.BlockSpec(memory_space=pltpu.SMEM),
                  pl.BlockSpec((WIN, BW), lambda i: (i, P_SQ // BW)),
                  pl.BlockSpec((WIN, LANES), lambda i: (i, kcol)),
                  pl.BlockSpec((WIN, LANES), lambda i: (prev(i), kcol)),
                  pl.BlockSpec((WIN, LANES), lambda i: (i, vcol)),
                  pl.BlockSpec((WIN, LANES), lambda i: (prev(i), vcol)),
                  pl.BlockSpec((WIN, LANES), lambda i: (i, 0)),
                  pl.BlockSpec((WIN, LANES), lambda i: (i, 0)),
                  pl.BlockSpec((WIN, LANES), lambda i: (prev(i), 0)),
                  pl.BlockSpec((WIN, LANES), lambda i: (prev(i), 0))],
        out_specs=pl.BlockSpec((WIN, BW), lambda i: (i, 0)),
        out_shape=jax.ShapeDtypeStruct((m, BW), BF16),
        compiler_params=_cparams(("arbitrary",)), name="swa",
    )(sinks, proj, proj, proj, proj, proj, cos4, sin4, cos4, sin4)


def _merge_kernel(oa_ref, ob_ref, oc_ref, ga_ref, gb_ref, gc_ref, w_ref, o_ref):
    acc = _sigmoid(ga_ref[...].astype(F32)) * _dot(oa_ref[...], w_ref[0])
    acc = acc + _sigmoid(gb_ref[...].astype(F32)) * _dot(ob_ref[...], w_ref[1])
    acc = acc + _sigmoid(gc_ref[...].astype(F32)) * _dot(oc_ref[...], w_ref[2])
    o_ref[...] = acc.astype(BF16)


def _merge(o_a, o_b, o_c, proj, w_branch):
    m = o_a.shape[0]
    tm, tn = 1024, 1024
    gblk = lambda b: (lambda j, i: (i, (P_GATE + b * D) // tn + j))
    oblk = pl.BlockSpec((tm, BW), lambda j, i: (i, 0))
    return pl.pallas_call(
        _merge_kernel, grid=(D // tn, m // tm),
        in_specs=[oblk, oblk, oblk,
                  pl.BlockSpec((tm, tn), gblk(0)), pl.BlockSpec((tm, tn), gblk(1)), pl.BlockSpec((tm, tn), gblk(2)),
                  pl.BlockSpec((3, BW, tn), lambda j, i: (0, 0, j))],
        out_specs=pl.BlockSpec((tm, tn), lambda j, i: (i, j)),
        out_shape=jax.ShapeDtypeStruct((m, D), BF16),
        compiler_params=_cparams(("arbitrary", "arbitrary")), name="merge",
    )(o_a, o_b, o_c, proj, proj, proj, w_branch)


def _swiglu_acc(h, wg_ref, wu_ref, wd_ref, acc_ref):
    a = _dot(h, wg_ref[...])
    b = _dot(h, wu_ref[...])
    t = (a * _sigmoid(a) * b).astype(BF16)
    acc_ref[...] += _dot(t, wd_ref[...])


def _ffn_kernel(h_ref, wg_ref, wu_ref, wd_ref, x_ref, g_ref, o_ref, acc_ref, *, nf):
    f = pl.program_id(1)

    @pl.when(f == 0)
    def _():
        acc_ref[...] = jnp.zeros_like(acc_ref)

    _swiglu_acc(h_ref[...], wg_ref, wu_ref, wd_ref, acc_ref)

    @pl.when(f == nf - 1)
    def _():
        o_ref[...] = x_ref[...] + g_ref[...] * acc_ref[...]


def _ffn_dense(h, wg, wu, wd, x, gate, seq):
    m, d = h.shape
    ff = wg.shape[1]
    tm, tf = 512, 1024
    nf = ff // tf
    tpb = seq // tm
    return pl.pallas_call(
        functools.partial(_ffn_kernel, nf=nf), grid=(m // tm, nf),
        in_specs=[pl.BlockSpec((tm, d), lambda i, f: (i, 0)),
                  pl.BlockSpec((d, tf), lambda i, f: (0, f)),
                  pl.BlockSpec((d, tf), lambda i, f: (0, f)),
                  pl.BlockSpec((tf, d), lambda i, f: (f, 0)),
                  pl.BlockSpec((tm, d), lambda i, f: (i, 0)),
                  pl.BlockSpec((None, 1, d), lambda i, f: (i // tpb, 0, 0))],
        out_specs=pl.BlockSpec((tm, d), lambda i, f: (i, 0)),
        out_shape=jax.ShapeDtypeStruct((m, d), F32),
        scratch_shapes=[pltpu.VMEM((tm, d), F32)],
        compiler_params=_cparams(("arbitrary", "arbitrary")), name="ffn_dense",
    )(h, wg, wu, wd, x, gate)


def _row_copy(src_hbm, idx_ref, pos, dst, r, sem):
    return pltpu.make_async_copy(src_hbm.at[pl.ds(idx_ref[pos], 1), :], dst.at[pl.ds(r, 1), :], sem)


def _start_row_gather(src_hbm, idx_ref, base, n, dst, sem):
    def body(r, carry):
        _row_copy(src_hbm, idx_ref, base + r, dst, r, sem).start()
        return carry

    lax.fori_loop(0, n, body, 0, unroll=8)


def _wait_row_gather(src_hbm, n, dst, sem):
    pltpu.make_async_copy(src_hbm.at[pl.ds(0, n), :], dst, sem).wait()


def _ffn_group_kernel(te_ref, tv_ref, src_ref, h_hbm, wg_ref, wu_ref, wd_ref, o_ref,
                      buf, hbf_ref, acc_ref, sem, *, nf, tm, n_tiles):
    i = pl.program_id(0)
    f = pl.program_id(1)
    slot = i % 2
    live = tv_ref[i] > 0
    per = -(-tm // nf)
    last = tm - per * (nf - 1)

    @pl.when(f == 0)
    def _():
        @pl.when(i == 0)
        def _():
            _start_row_gather(h_hbm, src_ref, 0, tm, buf.at[0], sem.at[0])

        _wait_row_gather(h_hbm, tm, buf.at[slot], sem.at[slot])
        hbf_ref[...] = buf[slot].astype(BF16)
        acc_ref[...] = jnp.zeros_like(acc_ref)

    def prefetch(first, count):
        for j in range(count):
            r = first + j
            _row_copy(h_hbm, src_ref, (i + 1) * tm + r, buf.at[1 - slot], r, sem.at[1 - slot]).start()

    def step(count, compute):
        c0 = count // 3
        c1 = (count - c0) // 2
        first = f * per
        if not compute:
            prefetch(first, count)
            return
        h = hbf_ref[...]
        a = _dot(h, wg_ref[...])
        prefetch(first, c0)
        b = _dot(h, wu_ref[...])
        prefetch(first + c0, c1)
        t = (a * _sigmoid(a) * b).astype(BF16)
        acc_ref[...] += _dot(t, wd_ref[...])
        prefetch(first + c0 + c1, count - c0 - c1)

    for is_last, count in ((False, per), (True, last)):
        for compute in (True, False):
            cond = (f == nf - 1) if is_last else (f < nf - 1)
            cond = cond & (live if compute else jnp.logical_not(live))
            pl.when(cond)(functools.partial(step, count, compute))

    @pl.when(f == nf - 1)
    def _():
        o_ref[...] = acc_ref[...]

        @pl.when(i == n_tiles - 1)
        def _():
            _wait_row_gather(h_hbm, tm, buf.at[1 - slot], sem.at[1 - slot])


def _ffn_grouped(h, wg, wu, wd, tile_expert, tile_valid, src_rows, tm):
    rows = src_rows.shape[0] - tm
    d, ff = wg.shape[1], wg.shape[2]
    tf = 1024
    nf = ff // tf
    n_tiles = rows // tm

    def fsel(i, f, tv):
        return jnp.where(tv[i] > 0, f, nf - 1)

    grid_spec = pltpu.PrefetchScalarGridSpec(
        num_scalar_prefetch=3, grid=(n_tiles, nf),
        in_specs=[pl.BlockSpec(memory_space=pl.ANY),
                  pl.BlockSpec((None, d, tf), lambda i, f, te, tv, sr: (te[i], 0, fsel(i, f, tv))),
                  pl.BlockSpec((None, d, tf), lambda i, f, te, tv, sr: (te[i], 0, fsel(i, f, tv))),
                  pl.BlockSpec((None, tf, d), lambda i, f, te, tv, sr: (te[i], fsel(i, f, tv), 0))],
        out_specs=pl.BlockSpec((tm, d), lambda i, f, te, tv, sr: (i, 0)),
        scratch_shapes=[pltpu.VMEM((2, tm, d), F32),
                        pltpu.VMEM((tm, d), BF16),
                        pltpu.VMEM((tm, d), F32),
                        pltpu.SemaphoreType.DMA((2,))])
    return pl.pallas_call(
        functools.partial(_ffn_group_kernel, nf=nf, tm=tm, n_tiles=n_tiles), grid_spec=grid_spec,
        out_shape=jax.ShapeDtypeStruct((rows, d), F32),
        compiler_params=_cparams(("arbitrary", "arbitrary")), name="ffn_grouped",
    )(tile_expert, tile_valid, src_rows, h, wg, wu, wd)


COMBINE_TM = 512


def _combine_kernel(pos_ref, x_ref, y_hbm, r_ref, g_ref, gain_ref, o_ref, buf, sem, *, tm, nt, m, final):
    i = pl.program_id(0)
    slot = i % 2

    def start(tile, s):
        for k in range(2):
            _start_row_gather(y_hbm, pos_ref, k * m + tile * tm, tm, buf.at[s, k], sem.at[s])

    @pl.when(i == 0)
    def _():
        start(0, 0)

    @pl.when(i + 1 < nt)
    def _():
        start(jnp.minimum(i + 1, nt - 1), 1 - slot)

    for k in range(2):
        _wait_row_gather(y_hbm, tm, buf.at[slot, k], sem.at[slot])
    w0 = r_ref[:, 2:3]
    w1 = r_ref[:, 3:4]
    xn = x_ref[...] + g_ref[...] * (w0 * buf[slot, 0] + w1 * buf[slot, 1])
    if final:
        xn = xn * lax.rsqrt(jnp.mean(xn * xn, axis=-1, keepdims=True) + EPS) * gain_ref[...]
    o_ref[...] = xn


def _combine(x, y, pos_rows, route, gate, seq, final_gain=None):
    m, d = x.shape
    tm = COMBINE_TM
    nt = m // tm
    tpb = seq // tm
    final = final_gain is not None
    gain = (final_gain if final else jnp.ones((d,), F32)).reshape(1, d)
    grid_spec = pltpu.PrefetchScalarGridSpec(
        num_scalar_prefetch=1, grid=(nt,),
        in_specs=[pl.BlockSpec((tm, d), lambda i, p: (i, 0)),
                  pl.BlockSpec(memory_space=pl.ANY),
                  pl.BlockSpec((tm, LANES), lambda i, p: (i, 0)),
                  pl.BlockSpec((None, 1, d), lambda i, p: (i // tpb, 0, 0)),
                  pl.BlockSpec((1, d), lambda i, p: (0, 0))],
        out_specs=pl.BlockSpec((tm, d), lambda i, p: (i, 0)),
        scratch_shapes=[pltpu.VMEM((2, 2, tm, d), F32), pltpu.SemaphoreType.DMA((2,))])
    return pl.pallas_call(
        functools.partial(_combine_kernel, tm=tm, nt=nt, m=m, final=final), grid_spec=grid_spec,
        out_shape=jax.ShapeDtypeStruct((m, d), F32),
        compiler_params=_cparams(("arbitrary",)), name="combine",
    )(pos_rows, x, y, route, gate, gain)


def _final_norm_kernel(x_ref, gain_ref, o_ref):
    x = x_ref[...]
    o_ref[...] = x * lax.rsqrt(jnp.mean(x * x, axis=-1, keepdims=True) + EPS) * gain_ref[...]


def _final_norm(x, gain):
    m, d = x.shape
    tm = 512
    blk = pl.BlockSpec((tm, d), lambda i: (i, 0))
    return pl.pallas_call(
        _final_norm_kernel, grid=(m // tm,),
        in_specs=[blk, pl.BlockSpec((1, d), lambda i: (0, 0))],
        out_specs=blk, out_shape=jax.ShapeDtypeStruct((m, d), F32),
        compiler_params=_cparams(("arbitrary",)), name="final_norm",
    )(x, gain.reshape(1, d))


MOE_TM = 512


def _moe(x, h, route, counts, wg, wu, wd, gate, seq, final_gain=None):
    m, d = x.shape
    tm = MOE_TM
    rows = 2 * m + NE * tm
    n_tiles = rows // tm
    e_flat = route[:, 0:2].astype(jnp.int32).reshape(-1)
    rank = route[:, 4:6].astype(jnp.int32).reshape(-1)
    counts = counts[0, :NE].astype(jnp.int32)
    padded = ((counts + tm - 1) // tm) * tm
    ends = jnp.cumsum(padded)
    starts = ends - padded
    pos = jnp.take(starts, e_flat) + rank
    src_token = jnp.zeros((rows + tm,), jnp.int32).at[pos].set(jnp.arange(2 * m, dtype=jnp.int32) // 2)
    tile_start = jnp.arange(n_tiles, dtype=jnp.int32) * tm
    tile_expert = jnp.minimum(jnp.sum((tile_start[:, None] >= ends[None, :]).astype(jnp.int32), axis=1), NE - 1)
    tile_valid = (tile_start < ends[-1]).astype(jnp.int32)

    ys = _ffn_grouped(h, wg, wu, wd, tile_expert, tile_valid, src_token, tm)
    pos_rows = pos.reshape(m, 2).T.reshape(-1)
    return _combine(x, ys, pos_rows, route, gate, seq, final_gain)


REGROUP_TN = 256
REGROUP_SEGS = ((P_GATE, 8472, 6144), (P_DN, 0, 4096), (P_FQ, 4112, 3072), (P_SQ, 7192, 1280))


def _regroup_kernel(row_ref, scale_ref, w_ref, o_ref):
    o_ref[...] = (w_ref[0].T * scale_ref[pl.program_id(0)]).astype(BF16)


def _regroup_w_in(w_in_t, layer):
    d = w_in_t.shape[2]
    tn = REGROUP_TN
    rows, scale = [], []
    for dst, src, width in REGROUP_SEGS:
        for t in range(width // tn):
            assert (src + t * tn) % 8 == 0
            rows.append((src + t * tn) // 8)
            o = dst + t * tn
            scale.append(HD ** -0.5 if P_FQ <= o < P_FK else SWA_D ** -0.5 if P_SQ <= o < P_SK else 1.0)
    n_tiles = len(rows)
    assert n_tiles * tn == NP
    grid_spec = pltpu.PrefetchScalarGridSpec(
        num_scalar_prefetch=2, grid=(n_tiles,),
        in_specs=[pl.BlockSpec((pl.Element(1), pl.Element(tn), pl.Element(d)),
                               lambda j, rw, sc: (layer, rw[j] * 8, 0))],
        out_specs=pl.BlockSpec((d, tn), lambda j, rw, sc: (0, j)))
    return pl.pallas_call(
        _regroup_kernel, grid_spec=grid_spec,
        out_shape=jax.ShapeDtypeStruct((d, NP), BF16),
        compiler_params=_cparams(("arbitrary",)), name="regroup_w_in",
    )(jnp.asarray(rows, jnp.int32), jnp.asarray(scale, F32), w_in_t)


def _small_w_kernel(a_ref, b_ref, o_ref):
    rows = jnp.concatenate([a_ref[...], b_ref[...],
                            jnp.zeros((LANES - L_F - NH, a_ref.shape[1]), F32)], axis=0)
    o_ref[...] = rows.T.astype(BF16)


def _small_w_in(w_in_t, layer):
    d = w_in_t.shape[2]
    return pl.pallas_call(
        _small_w_kernel, grid=(1,),
        in_specs=[pl.BlockSpec((None, L_F, d), lambda i: (layer, 4096 // L_F, 0)),
                  pl.BlockSpec((None, NH, d), lambda i: (layer, 7184 // NH, 0))],
        out_specs=pl.BlockSpec((d, LANES), lambda i: (0, 0)),
        out_shape=jax.ShapeDtypeStruct((d, LANES), BF16),
        compiler_params=_cparams(("arbitrary",)), name="small_w_in",
    )(w_in_t, w_in_t)


def kernel(x, c, positions, w_ada, b_ada, norm_mix, w_in, conv_w, dn_a_log, dn_dt_bias, dn_norm,
           fox_b_forget, swa_sinks, w_branch, w_out, norm_ffn, ffn_w_gate, ffn_w_up, ffn_w_down,
           moe_router, moe_w_gate, moe_w_up, moe_w_down, final_norm):
    bsz, seq, d = x.shape
    depth = w_ada.shape[0]
    m = bsz * seq
    xf = x.reshape(m, d)

    inv_freq = 10000.0 ** (-jnp.arange(0, SWA_D, 2, dtype=F32) / SWA_D)
    ang = positions.astype(F32).reshape(m, 1) * inv_freq[None, :]
    cos, sin = jnp.cos(ang), jnp.sin(ang)
    cos4 = jnp.tile(cos, (1, 4))
    sin4 = jnp.tile(jnp.concatenate([-sin, sin], axis=1), (1, 2))

    w_in_t = jnp.swapaxes(w_in, 1, 2)
    c_pad = jnp.pad(c, ((0, 8 - bsz), (0, 0)))
    mod = _ada(c_pad, w_ada, b_ada)[:, :bsz].reshape(depth, bsz, 6, 1, d)

    for layer in range(depth):
        sh1, sc1, g1, sh2, sc2, g2 = (mod[layer, :, t] for t in range(6))
        w_main = _regroup_w_in(w_in_t, layer)
        w_small = _small_w_in(w_in_t, layer)
        h, small = _norm_modulate(xf, norm_mix[layer], sc1, sh1, seq, w_small=w_small)
        proj = _matmul(h, w_main, BF16, 1024, 2432, "in_proj")
        par = jnp.zeros((8, LANES), F32)
        par = par.at[0, L_G:L_G + NH].set(dn_a_log[layer])
        par = par.at[1, L_G:L_G + NH].set(dn_dt_bias[layer])
        par = par.at[1, L_F:L_F + NH].set(fox_b_forget[layer])
        col, rowf, rowg = _prep(small, par, bsz, seq)
        conv_w8 = jnp.pad(conv_w[layer], ((0, 4), (0, 0)))
        o_a = _delta(proj, conv_w8, col, rowg, dn_norm[layer], bsz, seq)
        o_b = _fox(proj, rowf, bsz, seq)
        o_c = _swa(proj, swa_sinks[layer], cos4, sin4, bsz, seq)
        merged = _merge(o_a, o_b, o_c, proj, w_branch[layer].astype(BF16))
        xf = _matmul_resid(merged, w_out[layer].astype(BF16), xf, g1, seq, 512, 2048, "out_proj")
        if layer % 2 == 0:
            i = layer // 2
            h2 = _norm_modulate(xf, norm_ffn[layer], sc2, sh2, seq)
            xf = _ffn_dense(h2, ffn_w_gate[i].astype(BF16), ffn_w_up[i].astype(BF16),
                            ffn_w_down[i].astype(BF16), xf, g2, seq)
        else:
            i = layer // 2
            h2, route, counts = _norm_modulate(xf, norm_ffn[layer], sc2, sh2, seq, w_router=moe_router[i])
            last = layer == depth - 1
            xf = _moe(xf, h2, route, counts, moe_w_gate[i].astype(BF16), moe_w_up[i].astype(BF16),
                      moe_w_down[i].astype(BF16), g2, seq, final_gain=final_norm if last else None)
            if last:
                return xf.reshape(bsz, seq, d)
    return _final_norm(xf, final_norm).reshape(bsz, seq, d)
```

```python
import functools

import jax
import jax.numpy as jnp
from jax import lax
from jax.experimental import pallas as pl
from jax.experimental.pallas import tpu as pltpu

F32 = jnp.float32
BF16 = jnp.bfloat16
EPS = 1e-6

D = 2048
HD = 128
NH = 8
BW = 1024
CHUNK = 64
SWA_D = 64
SWA_QH = 16
SWA_G = 8
WIN = 128
NE = 8
LANES = 128

P_GATE = 0
P_DN = 6144
P_DNZ = 9216
P_FQ, P_FK, P_FV = 10240, 11264, 12288
P_SQ, P_SK, P_SV = 13312, 14336, 14464
NP = 14592
L_BETA, L_G, L_F = 0, 8, 16

VMEM_LIMIT = 56 * 1024 * 1024


def _cparams(sem):
    return pltpu.CompilerParams(dimension_semantics=sem, vmem_limit_bytes=VMEM_LIMIT)


def _sigmoid(x):
    return 1.0 / (1.0 + jnp.exp(-x))


def _split3(x):
    hi = x.astype(BF16)
    r = x - hi.astype(F32)
    mid = r.astype(BF16)
    lo = (r - mid.astype(F32)).astype(BF16)
    return hi, mid, lo


def _dot(a, b):
    return jnp.dot(a, b, preferred_element_type=F32)


def _dot_hi(a, b):
    a0 = a.astype(BF16)
    a1 = (a - a0.astype(F32)).astype(BF16)
    b0 = b.astype(BF16)
    b1 = (b - b0.astype(F32)).astype(BF16)
    return _dot(a0, b0) + (_dot(a0, b1) + _dot(a1, b0))


def _ada_kernel(c_ref, w_ref, b_ref, o_ref):
    c = c_ref[...]
    o_ref[...] = _dot_hi(c * _sigmoid(c), w_ref[...]) + b_ref[...]


def _ada(c_pad, w_ada, b_ada):
    depth, d, n = w_ada.shape
    tn = 1536
    return pl.pallas_call(
        _ada_kernel,
        grid=(depth, n // tn),
        in_specs=[pl.BlockSpec((8, d), lambda l, j: (0, 0)),
                  pl.BlockSpec((None, d, tn), lambda l, j: (l, 0, j)),
                  pl.BlockSpec((None, 1, tn), lambda l, j: (l, 0, j))],
        out_specs=pl.BlockSpec((None, 8, tn), lambda l, j: (l, 0, j)),
        out_shape=jax.ShapeDtypeStruct((depth, 8, n), F32),
        compiler_params=_cparams(("arbitrary", "arbitrary")),
        name="ada",
    )(c_pad, w_ada, b_ada.reshape(depth, 1, n))


def _normmod(x_ref, gain_ref, sc_ref, sh_ref):
    x = x_ref[...]
    ms = jnp.mean(x * x, axis=-1, keepdims=True)
    y = x * lax.rsqrt(ms + EPS) * gain_ref[...]
    return y * (1.0 + sc_ref[...]) + sh_ref[...]


def _normmod_kernel(x_ref, gain_ref, sc_ref, sh_ref, h_ref):
    h_ref[...] = _normmod(x_ref, gain_ref, sc_ref, sh_ref).astype(BF16)


def _normmod_small_kernel(x_ref, gain_ref, sc_ref, sh_ref, ws_ref, h_ref, s_ref):
    h = _normmod(x_ref, gain_ref, sc_ref, sh_ref).astype(BF16)
    h_ref[...] = h
    s_ref[...] = _dot(h, ws_ref[...])


def _normmod_route_kernel(x_ref, gain_ref, sc_ref, sh_ref, wr_ref, h_ref, r_ref, cnt_ref, carry_ref):
    @pl.when(pl.program_id(0) == 0)
    def _():
        carry_ref[...] = jnp.zeros_like(carry_ref)

    h = _normmod(x_ref, gain_ref, sc_ref, sh_ref)
    h_ref[...] = h
    logits = _dot_hi(h, wr_ref[...])
    lane = lax.broadcasted_iota(jnp.int32, logits.shape, 1)
    neg = jnp.float32(-jnp.inf)
    l1 = jnp.where(lane < NE, logits, neg)
    m1 = jnp.max(l1, axis=-1, keepdims=True)
    i1 = jnp.min(jnp.where(l1 == m1, lane, LANES), axis=-1, keepdims=True)
    l2 = jnp.where(lane == i1, neg, l1)
    m2 = jnp.max(l2, axis=-1, keepdims=True)
    i2 = jnp.min(jnp.where(l2 == m2, lane, LANES), axis=-1, keepdims=True)
    e = jnp.exp(m2 - m1)
    w1 = 1.0 / (1.0 + e)
    w2 = e / (1.0 + e)
    tm = logits.shape[0]
    picked = jnp.where((lane == i1) | (lane == i2), 1.0, 0.0)
    rr = lax.broadcasted_iota(jnp.int32, (tm, tm), 0)
    cc = lax.broadcasted_iota(jnp.int32, (tm, tm), 1)
    before = _dot(jnp.where(cc < rr, 1.0, 0.0).astype(BF16), picked.astype(BF16)) + carry_ref[...]
    rank1 = jnp.sum(jnp.where(lane == i1, before, 0.0), axis=-1, keepdims=True)
    rank2 = jnp.sum(jnp.where(lane == i2, before, 0.0), axis=-1, keepdims=True)
    carry_ref[...] += jnp.sum(picked, axis=0, keepdims=True)
    cnt_ref[...] = jnp.broadcast_to(carry_ref[...], cnt_ref.shape)
    vals = (i1.astype(F32), i2.astype(F32), w1, w2, rank1, rank2)
    out = jnp.zeros_like(logits)
    for pos, v in enumerate(vals):
        out = jnp.where(lane == pos, v, out)
    r_ref[...] = out


def _norm_modulate(x, gain, sc, sh, seq, w_router=None, w_small=None):
    m, d = x.shape
    tm = 512
    tpb = seq // tm
    in_specs = [pl.BlockSpec((tm, d), lambda i: (i, 0)),
                pl.BlockSpec((1, d), lambda i: (0, 0)),
                pl.BlockSpec((None, 1, d), lambda i: (i // tpb, 0, 0)),
                pl.BlockSpec((None, 1, d), lambda i: (i // tpb, 0, 0))]
    args = [x, gain.reshape(1, d), sc, sh]
    if w_small is not None:
        return pl.pallas_call(
            _normmod_small_kernel, grid=(m // tm,),
            in_specs=in_specs + [pl.BlockSpec((d, LANES), lambda i: (0, 0))],
            out_specs=[pl.BlockSpec((tm, d), lambda i: (i, 0)), pl.BlockSpec((tm, LANES), lambda i: (i, 0))],
            out_shape=[jax.ShapeDtypeStruct((m, d), BF16), jax.ShapeDtypeStruct((m, LANES), F32)],
            compiler_params=_cparams(("arbitrary",)), name="normmod_small",
        )(*args, w_small)
    if w_router is None:
        return pl.pallas_call(
            _normmod_kernel, grid=(m // tm,), in_specs=in_specs,
            out_specs=pl.BlockSpec((tm, d), lambda i: (i, 0)),
            out_shape=jax.ShapeDtypeStruct((m, d), BF16),
            compiler_params=_cparams(("arbitrary",)), name="normmod",
        )(*args)
    wr = jnp.pad(w_router, ((0, 0), (0, LANES - NE)))
    return pl.pallas_call(
        _normmod_route_kernel, grid=(m // tm,),
        in_specs=in_specs + [pl.BlockSpec((d, LANES), lambda i: (0, 0))],
        out_specs=[pl.BlockSpec((tm, d), lambda i: (i, 0)),
                   pl.BlockSpec((tm, LANES), lambda i: (i, 0)),
                   pl.BlockSpec((8, LANES), lambda i: (0, 0))],
        out_shape=[jax.ShapeDtypeStruct((m, d), F32),
                   jax.ShapeDtypeStruct((m, LANES), F32),
                   jax.ShapeDtypeStruct((8, LANES), F32)],
        scratch_shapes=[pltpu.VMEM((1, LANES), F32)],
        compiler_params=_cparams(("arbitrary",)), name="normmod_route",
    )(*args, wr)


def _mm_kernel(a_ref, w_ref, o_ref):
    o_ref[...] = _dot(a_ref[...], w_ref[...]).astype(o_ref.dtype)


def _matmul(a, w, out_dtype, tm, tn, name):
    m, k = a.shape
    n = w.shape[1]
    return pl.pallas_call(
        _mm_kernel, grid=(n // tn, m // tm),
        in_specs=[pl.BlockSpec((tm, k), lambda j, i: (i, 0)),
                  pl.BlockSpec((k, tn), lambda j, i: (0, j))],
        out_specs=pl.BlockSpec((tm, tn), lambda j, i: (i, j)),
        out_shape=jax.ShapeDtypeStruct((m, n), out_dtype),
        compiler_params=_cparams(("arbitrary", "arbitrary")), name=name,
    )(a, w)


def _mm_resid_kernel(a_ref, w_ref, x_ref, g_ref, o_ref):
    o_ref[...] = x_ref[...] + g_ref[...] * _dot(a_ref[...], w_ref[...])


def _matmul_resid(a, w, x, gate, seq, tm, tn, name):
    m, k = a.shape
    n = w.shape[1]
    tpb = seq // tm
    return pl.pallas_call(
        _mm_resid_kernel, grid=(n // tn, m // tm),
        in_specs=[pl.BlockSpec((tm, k), lambda j, i: (i, 0)),
                  pl.BlockSpec((k, tn), lambda j, i: (0, j)),
                  pl.BlockSpec((tm, tn), lambda j, i: (i, j)),
                  pl.BlockSpec((None, 1, tn), lambda j, i: (i // tpb, 0, j))],
        out_specs=pl.BlockSpec((tm, tn), lambda j, i: (i, j)),
        out_shape=jax.ShapeDtypeStruct((m, n), F32),
        compiler_params=_cparams(("arbitrary", "arbitrary")), name=name,
    )(a, w, x, gate)


def _prep_kernel(x_ref, par_ref, col_ref, rowf_ref, rowg_ref, carry_ref, *, ts):
    @pl.when(pl.program_id(1) == 0)
    def _():
        carry_ref[...] = jnp.zeros_like(carry_ref)

    x = x_ref[...]
    lane = lax.broadcasted_iota(jnp.int32, x.shape, 1)
    is_g = (lane >= L_G) & (lane < L_G + NH)
    is_f = (lane >= L_F) & (lane < L_F + NH)
    a_rate = jnp.exp(par_ref[0:1, :])
    z = x + par_ref[1:2, :]
    t = jnp.log1p(jnp.exp(-jnp.abs(z)))
    sig = _sigmoid(x)
    g = -a_rate * (jnp.maximum(z, 0.0) + t)
    logf = jnp.minimum(z, 0.0) - t
    vals = jnp.where(is_g, g, jnp.where(is_f, logf, 0.0))
    r = lax.broadcasted_iota(jnp.int32, (ts, ts), 0)
    c = lax.broadcasted_iota(jnp.int32, (ts, ts), 1)
    tri_full = jnp.where(c <= r, 1.0, 0.0).astype(BF16)
    tri_blk = jnp.where((c <= r) & ((r // CHUNK) == (c // CHUNK)), 1.0, 0.0).astype(BF16)
    v0, v1, v2 = _split3(vals)
    cs_full = _dot(tri_full, v0) + _dot(tri_full, v1) + _dot(tri_full, v2)
    cs_blk = _dot(tri_blk, v0) + _dot(tri_blk, v1) + _dot(tri_blk, v2)
    cum = cs_full + carry_ref[...]
    carry_ref[...] = cum[ts - 1:ts, :]
    out = jnp.where(lane < NH, sig, jnp.where(is_g, cs_blk, jnp.where(is_f, cum, 0.0)))
    col_ref[...] = out
    out_t = out.T
    rowf_ref[...] = out_t[L_F:L_F + NH, :]
    for ci in range(ts // CHUNK):
        rowg_ref[ci] = out_t[0:32, ci * CHUNK:(ci + 1) * CHUNK]


def _prep(small, par, bsz, seq):
    m = small.shape[0]
    ts = 256
    nt = seq // ts
    return pl.pallas_call(
        functools.partial(_prep_kernel, ts=ts),
        grid=(bsz, nt),
        in_specs=[pl.BlockSpec((ts, LANES), lambda b, j: (b * nt + j, 0)),
                  pl.BlockSpec((8, LANES), lambda b, j: (0, 0))],
        out_specs=[pl.BlockSpec((ts, LANES), lambda b, j: (b * nt + j, 0)),
                   pl.BlockSpec((NH, ts), lambda b, j: (0, b * nt + j)),
                   pl.BlockSpec((ts // CHUNK, 32, CHUNK), lambda b, j: (b * nt + j, 0, 0))],
        out_shape=[jax.ShapeDtypeStruct((m, LANES), F32),
                   jax.ShapeDtypeStruct((NH, m), F32),
                   jax.ShapeDtypeStruct((m // CHUNK, 32, CHUNK), F32)],
        scratch_shapes=[pltpu.VMEM((1, LANES), F32)],
        compiler_params=_cparams(("arbitrary", "arbitrary")), name="prep",
    )(small, par)


HALO = 16


def _bmm(a, b):
    return lax.dot_general(a.astype(BF16), b.astype(BF16), (((2,), (1,)), ((0,), (0,))),
                           preferred_element_type=F32)


def _bmm_nt(a, b):
    return lax.dot_general(a.astype(BF16), b.astype(BF16), (((2,), (2,)), ((0,), (0,))),
                           preferred_element_type=F32)


def _delta_kernel(qkv_ref, halo_ref, z_ref, cw_ref, col_ref, rowg_ref, gain_ref, o_ref, s_ref, *, nc):
    j = pl.program_id(1)

    @pl.when(j == 0)
    def _():
        s_ref[...] = jnp.zeros_like(s_ref)

    ts = nc * CHUNK
    n = NH * nc
    xb = qkv_ref[...]
    halo = halo_ref[...]
    halo = jnp.where(j == 0, jnp.zeros_like(halo), halo)
    xx = jnp.concatenate([halo, xb], axis=0)
    rr = lax.broadcasted_iota(jnp.int32, (3 * ts, HALO + ts), 0)
    cc = lax.broadcasted_iota(jnp.int32, (3 * ts, HALO + ts), 1)
    sel = jnp.where(cc == (rr % ts) + HALO - 1 - rr // ts, 1.0, 0.0).astype(BF16)
    shifted = _dot(sel, xx)
    cw = cw_ref[...]
    acc = xb.astype(F32) * cw[3:4, :]
    for dlt in (1, 2, 3):
        acc = acc + shifted[(dlt - 1) * ts:dlt * ts, :] * cw[3 - dlt:4 - dlt, :]
    act = acc * _sigmoid(acc)

    cv = col_ref[...]
    rg = rowg_ref[...]
    qs, ks, vs, betas, gcs, grs = [], [], [], [], [], []
    for h in range(NH):
        qh = act[:, h * HD:(h + 1) * HD]
        kh = act[:, BW + h * HD:BW + (h + 1) * HD]
        vh = act[:, 2 * BW + h * HD:2 * BW + (h + 1) * HD]
        qh = qh * (lax.rsqrt(jnp.sum(qh * qh, axis=-1, keepdims=True) + EPS) * (HD ** -0.5))
        kh = kh * lax.rsqrt(jnp.sum(kh * kh, axis=-1, keepdims=True) + EPS)
        qs.append(qh.reshape(nc, CHUNK, HD))
        ks.append(kh.reshape(nc, CHUNK, HD))
        vs.append(vh.reshape(nc, CHUNK, HD))
        betas.append(cv[:, L_BETA + h:L_BETA + h + 1].reshape(nc, CHUNK, 1))
        gcs.append(cv[:, L_G + h:L_G + h + 1].reshape(nc, CHUNK, 1))
        grs.append(rg[:, L_G + h:L_G + h + 1, :])
    q = jnp.concatenate(qs, axis=0)
    k = jnp.concatenate(ks, axis=0)
    v = jnp.concatenate(vs, axis=0)
    beta = jnp.concatenate(betas, axis=0)
    gc = jnp.concatenate(gcs, axis=0)
    gr = jnp.concatenate(grs, axis=0)

    ri = lax.broadcasted_iota(jnp.int32, (CHUNK, CHUNK), 0)
    ci = lax.broadcasted_iota(jnp.int32, (CHUNK, CHUNK), 1)
    causal = (ci <= ri)[None]
    strict = (ci < ri)[None]
    decay = jnp.exp(jnp.where(causal, gc - gr, -jnp.inf))
    kb = k * beta
    both = _bmm_nt(jnp.concatenate([kb, q], axis=1), k)
    lower = jnp.where(strict, both[:, :CHUNK] * decay, 0.0)
    qk = both[:, CHUNK:] * decay

    eye = jnp.where(ci == ri, 1.0, 0.0)[None]
    t_inv = eye - lower
    pw = _bmm(lower, lower)
    for it in range(5):
        t_inv = t_inv + _bmm(t_inv, pw)
        if it < 4:
            pw = _bmm(pw, pw)

    eg = jnp.exp(gc)
    rhs = jnp.concatenate([v * beta, kb * eg], axis=-1)
    sol = _bmm(t_inv, rhs)
    u = sol[:, :, :HD]
    w = sol[:, :, HD:]
    q_dec = q * eg
    g_end = gc[:, CHUNK - 1:CHUNK, :]
    k_dec = k * jnp.exp(g_end - gc)
    g_last = jnp.exp(g_end)

    def pick(t, c):
        return t.reshape((NH, nc) + t.shape[1:])[:, c]

    state = s_ref[...]
    outs = []
    for c in range(nc):
        ws_qs = _bmm(jnp.concatenate([pick(w, c), pick(q_dec, c)], axis=1), state)
        v_new = pick(u, c) - ws_qs[:, :CHUNK]
        o_c = ws_qs[:, CHUNK:] + _bmm(pick(qk, c), v_new)
        kd_t = jnp.swapaxes(pick(k_dec, c), 1, 2)
        state = state * pick(g_last, c) + _bmm(kd_t, v_new)
        outs.append(o_c)
    s_ref[...] = state

    gain = gain_ref[...]
    zf = z_ref[...].astype(F32)
    cols = []
    for h in range(NH):
        oh = jnp.concatenate([outs[c][h] for c in range(nc)], axis=0)
        oh = oh * lax.rsqrt(jnp.mean(oh * oh, axis=-1, keepdims=True) + EPS) * gain
        zh = zf[:, h * HD:(h + 1) * HD]
        cols.append(oh * (zh * _sigmoid(zh)))
    o_ref[...] = jnp.concatenate(cols, axis=1).astype(BF16)


def _delta(proj, conv_w8, col, rowg, gain, bsz, seq):
    m = proj.shape[0]
    nc = 4
    ts = nc * CHUNK
    nt = seq // ts
    qkv_blk = P_DN // (3 * BW)
    return pl.pallas_call(
        functools.partial(_delta_kernel, nc=nc),
        grid=(bsz, nt),
        in_specs=[pl.BlockSpec((ts, 3 * BW), lambda b, j: (b * nt + j, qkv_blk)),
                  pl.BlockSpec((HALO, 3 * BW), lambda b, j: (jnp.maximum((b * nt + j) * (ts // HALO) - 1, 0), qkv_blk)),
                  pl.BlockSpec((ts, BW), lambda b, j: (b * nt + j, P_DNZ // BW)),
                  pl.BlockSpec((8, 3 * BW), lambda b, j: (0, 0)),
                  pl.BlockSpec((ts, LANES), lambda b, j: (b * nt + j, 0)),
                  pl.BlockSpec((nc, 32, CHUNK), lambda b, j: (b * nt + j, 0, 0)),
                  pl.BlockSpec((1, HD), lambda b, j: (0, 0))],
        out_specs=pl.BlockSpec((ts, BW), lambda b, j: (b * nt + j, 0)),
        out_shape=jax.ShapeDtypeStruct((m, BW), BF16),
        scratch_shapes=[pltpu.VMEM((NH, HD, HD), F32)],
        compiler_params=_cparams(("arbitrary", "arbitrary")), name="delta",
    )(proj, proj, proj, conv_w8, col, rowg, gain.reshape(1, HD))


FOX_T = 512
FOX_HPS = 4


def _fox_kernel(q_ref, k_ref, v_ref, crow_ref, o_ref, vx_ref, m_ref, acc_ref):
    t = FOX_T
    hp = pl.program_id(1)
    qi = pl.program_id(2)

    @pl.when(qi == 0)
    def _():
        lane = lax.broadcasted_iota(jnp.int32, (v_ref.shape[0], HD), 1)
        ones_col = jnp.where(lane == 0, 1.0, 0.0).astype(BF16)
        for e in range(FOX_HPS):
            vx_ref[e, :, :HD] = v_ref[:, e * HD:(e + 1) * HD]
            vx_ref[e, :, HD:] = ones_col

    m_ref[...] = jnp.full_like(m_ref, -jnp.inf)
    acc_ref[...] = jnp.zeros_like(acc_ref)

    def block(ki, masked):
        off = pl.multiple_of(ki * t, t)
        for e in range(FOX_HPS):
            s = lax.dot_general(q_ref[:, e * HD:(e + 1) * HD], k_ref[pl.ds(off, t), e * HD:(e + 1) * HD],
                                (((1,), (1,)), ((), ())), preferred_element_type=F32)
            s = s - crow_ref[pl.ds(FOX_HPS * hp + e, 1), pl.ds(off, t)]
            if masked:
                row = lax.broadcasted_iota(jnp.int32, (t, t), 0)
                col = lax.broadcasted_iota(jnp.int32, (t, t), 1)
                s = jnp.where(col <= row, s, -jnp.inf)
            m_prev = m_ref[e]
            m_new = jnp.maximum(m_prev, jnp.max(s, axis=1, keepdims=True))
            alpha = jnp.exp(m_prev - m_new)
            p = jnp.exp((s - jnp.tile(m_new, (1, t // LANES))).astype(BF16))
            acc_ref[e] = jnp.tile(alpha, (1, 2)) * acc_ref[e] + _dot(p, vx_ref[e, pl.ds(off, t), :])
            m_ref[e] = m_new

    def body(ki, carry):
        block(ki, False)
        return carry

    lax.fori_loop(0, qi, body, 0)
    block(qi, True)
    for e in range(FOX_HPS):
        o_ref[:, e * HD:(e + 1) * HD] = (acc_ref[e, :, :HD] / acc_ref[e, :, HD:HD + 1]).astype(BF16)


def _fox(proj, rowf, bsz, seq):
    m = proj.shape[0]
    t = FOX_T
    nq = seq // t
    w = FOX_HPS * HD
    return pl.pallas_call(
        _fox_kernel,
        grid=(bsz, NH // FOX_HPS, nq),
        in_specs=[pl.BlockSpec((t, w), lambda b, h, qi: (b * nq + qi, P_FQ // w + h)),
                  pl.BlockSpec((seq, w), lambda b, h, qi: (b, P_FK // w + h)),
                  pl.BlockSpec((seq, w), lambda b, h, qi: (b, P_FV // w + h)),
                  pl.BlockSpec((NH, seq), lambda b, h, qi: (0, b))],
        out_specs=pl.BlockSpec((t, w), lambda b, h, qi: (b * nq + qi, h)),
        out_shape=jax.ShapeDtypeStruct((m, BW), BF16),
        scratch_shapes=[pltpu.VMEM((FOX_HPS, seq, 2 * HD), BF16), pltpu.VMEM((FOX_HPS, t, HD), F32),
                        pltpu.VMEM((FOX_HPS, t, 2 * HD), F32)],
        compiler_params=_cparams(("arbitrary", "arbitrary", "arbitrary")), name="fox",
    )(proj, proj, proj, rowf)


def _swap_halves(x):
    w = x.shape[-1]
    lane = lax.broadcasted_iota(jnp.int32, x.shape, x.ndim - 1)
    return jnp.where((lane % SWA_D) < SWA_D // 2, pltpu.roll(x, w - SWA_D // 2, axis=x.ndim - 1),
                     pltpu.roll(x, SWA_D // 2, axis=x.ndim - 1))


def _swa_kernel(sink_ref, q_ref, kc_ref, kp_ref, vc_ref, vp_ref, cc_ref, sc_ref, cp_ref, sp_ref, o_ref, *, nblk):
    i = pl.program_id(0)
    first = (i % nblk) == 0
    cos_c, sin_c = cc_ref[...], sc_ref[...]
    q = q_ref[...].astype(F32)
    q = q * jnp.tile(cos_c, (1, SWA_QH // 2)) + _swap_halves(q) * jnp.tile(sin_c, (1, SWA_QH // 2))
    kc = kc_ref[...].astype(F32)
    kc = kc * cos_c + _swap_halves(kc) * sin_c
    kp = kp_ref[...].astype(F32)
    kp = kp * cp_ref[...] + _swap_halves(kp) * sp_ref[...]
    kk = jnp.concatenate([kp, kc], axis=0).astype(BF16)
    vv = jnp.concatenate([vp_ref[...], vc_ref[...]], axis=0)
    r = lax.broadcasted_iota(jnp.int32, (WIN, 2 * WIN), 0)
    c = lax.broadcasted_iota(jnp.int32, (WIN, 2 * WIN), 1)
    mask = (c > r) & (c <= r + WIN) & ((c >= WIN) | jnp.logical_not(first))
    qb = q.astype(BF16)
    outs = []
    for hq in range(SWA_QH):
        g = hq // SWA_G
        qh = qb[:, hq * SWA_D:(hq + 1) * SWA_D]
        kh = kk[:, g * SWA_D:(g + 1) * SWA_D]
        vh = vv[:, g * SWA_D:(g + 1) * SWA_D]
        s = lax.dot_general(qh, kh, (((1,), (1,)), ((), ())), preferred_element_type=F32)
        s = jnp.where(mask, s, -jnp.inf)
        sink = sink_ref[hq]
        mx = jnp.maximum(jnp.max(s, axis=1, keepdims=True), sink)
        p = jnp.exp(s - mx)
        den = jnp.sum(p, axis=1, keepdims=True) + jnp.exp(sink - mx)
        outs.append(_dot(p.astype(BF16), vh) / den)
    o_ref[...] = jnp.concatenate(outs, axis=1).astype(BF16)


def _swa(proj, sinks, cos4, sin4, bsz, seq):
    m = proj.shape[0]
    nblk = seq // WIN
    prev = lambda i: jnp.maximum(i - 1, 0)
    kcol, vcol = P_SK // LANES, P_SV // LANES
    return pl.pallas_call(
        functools.partial(_swa_kernel, nblk=nblk),
        grid=(m // WIN,),
        in_specs=[pl.BlockSpec(memory_space=pltpu.SMEM),
                  pl.BlockSpec((WIN, BW), lambda i: (i, P_SQ // BW)),
                  pl.BlockSpec((WIN, LANES), lambda i: (i, kcol)),
                  pl.BlockSpec((WIN, LANES), lambda i: (prev(i), kcol)),
                  pl.BlockSpec((WIN, LANES), lambda i: (i, vcol)),
                  pl.BlockSpec((WIN, LANES), lambda i: (prev(i), vcol)),
                  pl.BlockSpec((WIN, LANES), lambda i: (i, 0)),
                  pl.BlockSpec((WIN, LANES), lambda i: (i, 0)),
                  pl.BlockSpec((WIN, LANES), lambda i: (prev(i), 0)),
                  pl.BlockSpec((WIN, LANES), lambda i: (prev(i), 0))],
        out_specs=pl.BlockSpec((WIN, BW), lambda i: (i, 0)),
        out_shape=jax.ShapeDtypeStruct((m, BW), BF16),
        compiler_params=_cparams(("arbitrary",)), name="swa",
    )(sinks, proj, proj, proj, proj, proj, cos4, sin4, cos4, sin4)


def _merge_kernel(oa_ref, ob_ref, oc_ref, ga_ref, gb_ref, gc_ref, w_ref, o_ref):
    acc = _sigmoid(ga_ref[...].astype(F32)) * _dot(oa_ref[...], w_ref[0])
    acc = acc + _sigmoid(gb_ref[...].astype(F32)) * _dot(ob_ref[...], w_ref[1])
    acc = acc + _sigmoid(gc_ref[...].astype(F32)) * _dot(oc_ref[...], w_ref[2])
    o_ref[...] = acc.astype(BF16)


def _merge(o_a, o_b, o_c, proj, w_branch):
    m = o_a.shape[0]
    tm, tn = 1024, 1024
    gblk = lambda b: (lambda j, i: (i, (P_GATE + b * D) // tn + j))
    oblk = pl.BlockSpec((tm, BW), lambda j, i: (i, 0))
    return pl.pallas_call(
        _merge_kernel, grid=(D // tn, m // tm),
        in_specs=[oblk, oblk, oblk,
                  pl.BlockSpec((tm, tn), gblk(0)), pl.BlockSpec((tm, tn), gblk(1)), pl.BlockSpec((tm, tn), gblk(2)),
                  pl.BlockSpec((3, BW, tn), lambda j, i: (0, 0, j))],
        out_specs=pl.BlockSpec((tm, tn), lambda j, i: (i, j)),
        out_shape=jax.ShapeDtypeStruct((m, D), BF16),
        compiler_params=_cparams(("arbitrary", "arbitrary")), name="merge",
    )(o_a, o_b, o_c, proj, proj, proj, w_branch)


def _swiglu_acc(h, wg_ref, wu_ref, wd_ref, acc_ref):
    a = _dot(h, wg_ref[...])
    b = _dot(h, wu_ref[...])
    t = (a * _sigmoid(a) * b).astype(BF16)
    acc_ref[...] += _dot(t, wd_ref[...])


def _ffn_kernel(h_ref, wg_ref, wu_ref, wd_ref, x_ref, g_ref, *rest, nf, n_cast):
    cast_in, o_ref, cast_out, acc_ref = rest[:n_cast], rest[n_cast], rest[n_cast + 1:2 * n_cast + 1], rest[-1]
    f = pl.program_id(1)

    @pl.when(f == 0)
    def _():
        acc_ref[...] = jnp.zeros_like(acc_ref)

    _swiglu_acc(h_ref[...], wg_ref, wu_ref, wd_ref, acc_ref)
    for src, dst in zip(cast_in, cast_out):
        dst[...] = src[...].astype(BF16)

    @pl.when(f == nf - 1)
    def _():
        o_ref[...] = x_ref[...] + g_ref[...] * acc_ref[...]


def _ffn_dense(h, wg, wu, wd, x, gate, seq, cast=()):
    m, d = h.shape
    ff = wg.shape[1]
    tm = 512
    tf = 512 if cast else 1024
    ni, nf = m // tm, ff // tf
    tpb = seq // tm
    cast_specs = []
    for a in cast:
        r, c = a.shape
        if r % ni == 0 and c % nf == 0 and (c // nf) % LANES == 0:
            cast_specs.append(pl.BlockSpec((r // ni, c // nf), lambda i, f: (i, f)))
        else:
            assert r % (ni * nf * 8) == 0
            cast_specs.append(pl.BlockSpec((r // (ni * nf), c), lambda i, f: (i * nf + f, 0)))
    return pl.pallas_call(
        functools.partial(_ffn_kernel, nf=nf, n_cast=len(cast)), grid=(ni, nf),
        in_specs=[pl.BlockSpec((tm, d), lambda i, f: (i, 0)),
                  pl.BlockSpec((d, tf), lambda i, f: (0, f)),
                  pl.BlockSpec((d, tf), lambda i, f: (0, f)),
                  pl.BlockSpec((tf, d), lambda i, f: (f, 0)),
                  pl.BlockSpec((tm, d), lambda i, f: (i, 0)),
                  pl.BlockSpec((None, 1, d), lambda i, f: (i // tpb, 0, 0))] + cast_specs,
        out_specs=[pl.BlockSpec((tm, d), lambda i, f: (i, 0))] + cast_specs,
        out_shape=[jax.ShapeDtypeStruct((m, d), F32)] + [jax.ShapeDtypeStruct(a.shape, BF16) for a in cast],
        scratch_shapes=[pltpu.VMEM((tm, d), F32)],
        compiler_params=_cparams(("arbitrary", "arbitrary")), name="ffn_dense",
    )(h, wg, wu, wd, x, gate, *cast)


def _row_copy(src_hbm, idx_ref, pos, dst, r, sem):
    return pltpu.make_async_copy(src_hbm.at[pl.ds(idx_ref[pos], 1), :], dst.at[pl.ds(r, 1), :], sem)


def _start_row_gather(src_hbm, idx_ref, base, n, dst, sem):
    def body(r, carry):
        _row_copy(src_hbm, idx_ref, base + r, dst, r, sem).start()
        return carry

    lax.fori_loop(0, n, body, 0, unroll=8)


def _wait_row_gather(src_hbm, n, dst, sem):
    pltpu.make_async_copy(src_hbm.at[pl.ds(0, n), :], dst, sem).wait()


def _ffn_group_kernel(te_ref, tv_ref, src_ref, h_hbm, wg_ref, wu_ref, wd_ref, o_ref,
                      buf, hbf_ref, acc_ref, sem, *, nf, tm, n_tiles):
    i = pl.program_id(0)
    f = pl.program_id(1)
    slot = i % 2
    live = tv_ref[i] > 0
    per = -(-tm // nf)
    last = tm - per * (nf - 1)

    @pl.when(f == 0)
    def _():
        @pl.when(i == 0)
        def _():
            _start_row_gather(h_hbm, src_ref, 0, tm, buf.at[0], sem.at[0])

        _wait_row_gather(h_hbm, tm, buf.at[slot], sem.at[slot])
        hbf_ref[...] = buf[slot].astype(BF16)
        acc_ref[...] = jnp.zeros_like(acc_ref)

    def prefetch(first, count):
        for j in range(count):
            r = first + j
            _row_copy(h_hbm, src_ref, (i + 1) * tm + r, buf.at[1 - slot], r, sem.at[1 - slot]).start()

    def step(count, compute):
        c0 = count // 3
        c1 = (count - c0) // 2
        first = f * per
        if not compute:
            prefetch(first, count)
            return
        h = hbf_ref[...]
        a = _dot(h, wg_ref[...])
        prefetch(first, c0)
        b = _dot(h, wu_ref[...])
        prefetch(first + c0, c1)
        t = (a * _sigmoid(a) * b).astype(BF16)
        acc_ref[...] += _dot(t, wd_ref[...])
        prefetch(first + c0 + c1, count - c0 - c1)

    for is_last, count in ((False, per), (True, last)):
        for compute in (True, False):
            cond = (f == nf - 1) if is_last else (f < nf - 1)
            cond = cond & (live if compute else jnp.logical_not(live))
            pl.when(cond)(functools.partial(step, count, compute))

    @pl.when(f == nf - 1)
    def _():
        o_ref[...] = acc_ref[...]

        @pl.when(i == n_tiles - 1)
        def _():
            _wait_row_gather(h_hbm, tm, buf.at[1 - slot], sem.at[1 - slot])


def _ffn_grouped(h, wg, wu, wd, tile_expert, tile_valid, src_rows, tm):
    rows = src_rows.shape[0] - tm
    d, ff = wg.shape[1], wg.shape[2]
    tf = 1024
    nf = ff // tf
    n_tiles = rows // tm

    def fsel(i, f, tv):
        return jnp.where(tv[i] > 0, f, nf - 1)

    grid_spec = pltpu.PrefetchScalarGridSpec(
        num_scalar_prefetch=3, grid=(n_tiles, nf),
        in_specs=[pl.BlockSpec(memory_space=pl.ANY),
                  pl.BlockSpec((None, d, tf), lambda i, f, te, tv, sr: (te[i], 0, fsel(i, f, tv))),
                  pl.BlockSpec((None, d, tf), lambda i, f, te, tv, sr: (te[i], 0, fsel(i, f, tv))),
                  pl.BlockSpec((None, tf, d), lambda i, f, te, tv, sr: (te[i], fsel(i, f, tv), 0))],
        out_specs=pl.BlockSpec((tm, d), lambda i, f, te, tv, sr: (i, 0)),
        scratch_shapes=[pltpu.VMEM((2, tm, d), F32),
                        pltpu.VMEM((tm, d), BF16),
                        pltpu.VMEM((tm, d), F32),
                        pltpu.SemaphoreType.DMA((2,))])
    return pl.pallas_call(
        functools.partial(_ffn_group_kernel, nf=nf, tm=tm, n_tiles=n_tiles), grid_spec=grid_spec,
        out_shape=jax.ShapeDtypeStruct((rows, d), F32),
        compiler_params=_cparams(("arbitrary", "arbitrary")), name="ffn_grouped",
    )(tile_expert, tile_valid, src_rows, h, wg, wu, wd)


COMBINE_TM = 512


def _combine_kernel(pos_ref, x_ref, y_hbm, r_ref, g_ref, gain_ref, o_ref, buf, sem, *, tm, nt, m, final):
    i = pl.program_id(0)
    slot = i % 2

    def start(tile, s):
        for k in range(2):
            _start_row_gather(y_hbm, pos_ref, k * m + tile * tm, tm, buf.at[s, k], sem.at[s])

    @pl.when(i == 0)
    def _():
        start(0, 0)

    @pl.when(i + 1 < nt)
    def _():
        start(jnp.minimum(i + 1, nt - 1), 1 - slot)

    for k in range(2):
        _wait_row_gather(y_hbm, tm, buf.at[slot, k], sem.at[slot])
    w0 = r_ref[:, 2:3]
    w1 = r_ref[:, 3:4]
    xn = x_ref[...] + g_ref[...] * (w0 * buf[slot, 0] + w1 * buf[slot, 1])
    if final:
        xn = xn * lax.rsqrt(jnp.mean(xn * xn, axis=-1, keepdims=True) + EPS) * gain_ref[...]
    o_ref[...] = xn


def _combine(x, y, pos_rows, route, gate, seq, final_gain=None):
    m, d = x.shape
    tm = COMBINE_TM
    nt = m // tm
    tpb = seq // tm
    final = final_gain is not None
    gain = (final_gain if final else jnp.ones((d,), F32)).reshape(1, d)
    grid_spec = pltpu.PrefetchScalarGridSpec(
        num_scalar_prefetch=1, grid=(nt,),
        in_specs=[pl.BlockSpec((tm, d), lambda i, p: (i, 0)),
                  pl.BlockSpec(memory_space=pl.ANY),
                  pl.BlockSpec((tm, LANES), lambda i, p: (i, 0)),
                  pl.BlockSpec((None, 1, d), lambda i, p: (i // tpb, 0, 0)),
                  pl.BlockSpec((1, d), lambda i, p: (0, 0))],
        out_specs=pl.BlockSpec((tm, d), lambda i, p: (i, 0)),
        scratch_shapes=[pltpu.VMEM((2, 2, tm, d), F32), pltpu.SemaphoreType.DMA((2,))])
    return pl.pallas_call(
        functools.partial(_combine_kernel, tm=tm, nt=nt, m=m, final=final), grid_spec=grid_spec,
        out_shape=jax.ShapeDtypeStruct((m, d), F32),
        compiler_params=_cparams(("arbitrary",)), name="combine",
    )(pos_rows, x, y, route, gate, gain)


def _final_norm_kernel(x_ref, gain_ref, o_ref):
    x = x_ref[...]
    o_ref[...] = x * lax.rsqrt(jnp.mean(x * x, axis=-1, keepdims=True) + EPS) * gain_ref[...]


def _final_norm(x, gain):
    m, d = x.shape
    tm = 512
    blk = pl.BlockSpec((tm, d), lambda i: (i, 0))
    return pl.pallas_call(
        _final_norm_kernel, grid=(m // tm,),
        in_specs=[blk, pl.BlockSpec((1, d), lambda i: (0, 0))],
        out_specs=blk, out_shape=jax.ShapeDtypeStruct((m, d), F32),
        compiler_params=_cparams(("arbitrary",)), name="final_norm",
    )(x, gain.reshape(1, d))


MOE_TM = 512


def _moe(x, h, route, counts, wg, wu, wd, gate, seq, final_gain=None):
    m, d = x.shape
    tm = MOE_TM
    rows = 2 * m + NE * tm
    n_tiles = rows // tm
    e_flat = route[:, 0:2].astype(jnp.int32).reshape(-1)
    rank = route[:, 4:6].astype(jnp.int32).reshape(-1)
    counts = counts[0, :NE].astype(jnp.int32)
    padded = ((counts + tm - 1) // tm) * tm
    ends = jnp.cumsum(padded)
    starts = ends - padded
    pos = jnp.take(starts, e_flat) + rank
    src_token = jnp.zeros((rows + tm,), jnp.int32).at[pos].set(jnp.arange(2 * m, dtype=jnp.int32) // 2)
    tile_start = jnp.arange(n_tiles, dtype=jnp.int32) * tm
    tile_expert = jnp.minimum(jnp.sum((tile_start[:, None] >= ends[None, :]).astype(jnp.int32), axis=1), NE - 1)
    tile_valid = (tile_start < ends[-1]).astype(jnp.int32)

    ys = _ffn_grouped(h, wg, wu, wd, tile_expert, tile_valid, src_token, tm)
    pos_rows = pos.reshape(m, 2).T.reshape(-1)
    return _combine(x, ys, pos_rows, route, gate, seq, final_gain)


REGROUP_TN = 256
REGROUP_SEGS = ((P_GATE, 8472, 6144), (P_DN, 0, 4096), (P_FQ, 4112, 3072), (P_SQ, 7192, 1280))


def _regroup_kernel(row_ref, scale_ref, w_ref, o_ref):
    o_ref[...] = (w_ref[0].T * scale_ref[pl.program_id(0)]).astype(BF16)


def _regroup_w_in(w_in_t, layer):
    d = w_in_t.shape[2]
    tn = REGROUP_TN
    rows, scale = [], []
    for dst, src, width in REGROUP_SEGS:
        for t in range(width // tn):
            assert (src + t * tn) % 8 == 0
            rows.append((src + t * tn) // 8)
            o = dst + t * tn
            scale.append(HD ** -0.5 if P_FQ <= o < P_FK else SWA_D ** -0.5 if P_SQ <= o < P_SK else 1.0)
    n_tiles = len(rows)
    assert n_tiles * tn == NP
    grid_spec = pltpu.PrefetchScalarGridSpec(
        num_scalar_prefetch=2, grid=(n_tiles,),
        in_specs=[pl.BlockSpec((pl.Element(1), pl.Element(tn), pl.Element(d)),
                               lambda j, rw, sc: (layer, rw[j] * 8, 0))],
        out_specs=pl.BlockSpec((d, tn), lambda j, rw, sc: (0, j)))
    return pl.pallas_call(
        _regroup_kernel, grid_spec=grid_spec,
        out_shape=jax.ShapeDtypeStruct((d, NP), BF16),
        compiler_params=_cparams(("arbitrary",)), name="regroup_w_in",
    )(jnp.asarray(rows, jnp.int32), jnp.asarray(scale, F32), w_in_t)


def _small_w_kernel(a_ref, b_ref, o_ref):
    rows = jnp.concatenate([a_ref[...], b_ref[...],
                            jnp.zeros((LANES - L_F - NH, a_ref.shape[1]), F32)], axis=0)
    o_ref[...] = rows.T.astype(BF16)


def _small_w_in(w_in_t, layer):
    d = w_in_t.shape[2]
    return pl.pallas_call(
        _small_w_kernel, grid=(1,),
        in_specs=[pl.BlockSpec((None, L_F, d), lambda i: (layer, 4096 // L_F, 0)),
                  pl.BlockSpec((None, NH, d), lambda i: (layer, 7184 // NH, 0))],
        out_specs=pl.BlockSpec((d, LANES), lambda i: (0, 0)),
        out_shape=jax.ShapeDtypeStruct((d, LANES), BF16),
        compiler_params=_cparams(("arbitrary",)), name="small_w_in",
    )(w_in_t, w_in_t)


def kernel(x, c, positions, w_ada, b_ada, norm_mix, w_in, conv_w, dn_a_log, dn_dt_bias, dn_norm,
           fox_b_forget, swa_sinks, w_branch, w_out, norm_ffn, ffn_w_gate, ffn_w_up, ffn_w_down,
           moe_router, moe_w_gate, moe_w_up, moe_w_down, final_norm):
    bsz, seq, d = x.shape
    depth = w_ada.shape[0]
    m = bsz * seq
    xf = x.reshape(m, d)

    inv_freq = 10000.0 ** (-jnp.arange(0, SWA_D, 2, dtype=F32) / SWA_D)
    ang = positions.astype(F32).reshape(m, 1) * inv_freq[None, :]
    cos, sin = jnp.cos(ang), jnp.sin(ang)
    cos4 = jnp.tile(cos, (1, 4))
    sin4 = jnp.tile(jnp.concatenate([-sin, sin], axis=1), (1, 2))

    w_in_t = jnp.swapaxes(w_in, 1, 2)
    c_pad = jnp.pad(c, ((0, 8 - bsz), (0, 0)))
    mod = _ada(c_pad, w_ada, b_ada)[:, :bsz].reshape(depth, bsz, 6, 1, d)

    moe_bf16 = {}
    for layer in range(depth):
        sh1, sc1, g1, sh2, sc2, g2 = (mod[layer, :, t] for t in range(6))
        w_main = _regroup_w_in(w_in_t, layer)
        w_small = _small_w_in(w_in_t, layer)
        h, small = _norm_modulate(xf, norm_mix[layer], sc1, sh1, seq, w_small=w_small)
        proj = _matmul(h, w_main, BF16, 1024, 2432, "in_proj")
        par = jnp.zeros((8, LANES), F32)
        par = par.at[0, L_G:L_G + NH].set(dn_a_log[layer])
        par = par.at[1, L_G:L_G + NH].set(dn_dt_bias[layer])
        par = par.at[1, L_F:L_F + NH].set(fox_b_forget[layer])
        col, rowf, rowg = _prep(small, par, bsz, seq)
        conv_w8 = jnp.pad(conv_w[layer], ((0, 4), (0, 0)))
        o_a = _delta(proj, conv_w8, col, rowg, dn_norm[layer], bsz, seq)
        o_b = _fox(proj, rowf, bsz, seq)
        o_c = _swa(proj, swa_sinks[layer], cos4, sin4, bsz, seq)
        merged = _merge(o_a, o_b, o_c, proj, w_branch[layer].astype(BF16))
        xf = _matmul_resid(merged, w_out[layer].astype(BF16), xf, g1, seq, 512, 2048, "out_proj")
        if layer % 2 == 0:
            i = layer // 2
            h2 = _norm_modulate(xf, norm_ffn[layer], sc2, sh2, seq)
            cast = ()
            if layer + 1 < depth:
                j = (layer + 1) // 2
                cast = (moe_w_gate[j].reshape(NE * d, -1), moe_w_up[j].reshape(NE * d, -1),
                        moe_w_down[j].reshape(-1, d))
            xf, *converted = _ffn_dense(h2, ffn_w_gate[i].astype(BF16), ffn_w_up[i].astype(BF16),
                                        ffn_w_down[i].astype(BF16), xf, g2, seq, cast=cast)
            if converted:
                moe_bf16[j] = (converted[0].reshape(NE, d, -1), converted[1].reshape(NE, d, -1),
                               converted[2].reshape(NE, -1, d))
        else:
            i = layer // 2
            h2, route, counts = _norm_modulate(xf, norm_ffn[layer], sc2, sh2, seq, w_router=moe_router[i])
            last = layer == depth - 1
            if i not in moe_bf16:
                moe_bf16[i] = (moe_w_gate[i].astype(BF16), moe_w_up[i].astype(BF16), moe_w_down[i].astype(BF16))
            xf = _moe(xf, h2, route, counts, *moe_bf16[i], g2, seq,
                      final_gain=final_norm if last else None)
            if last:
                return xf.reshape(bsz, seq, d)
    return _final_norm(xf, final_norm).reshape(bsz, seq, d)
```
